```python
import math
import jax, jax.numpy as jnp
from jax import lax
import numpy as np

D_MODEL = 1024
BATCH = 8
SEQ = 2048
DEPTH = 4
DEC_BATCH = 128
DEC_SEQ = 4
PAST_LEN = 16384
PAGE_SIZE = 128

BR_W = D_MODEL // 2
N_BRANCH = 3
A_HD = 64
A_HEADS = BR_W // A_HD
LORA_W = 64
LORA_A = 64
LORA_G = 128
A_COLS = 3 * BR_W + LORA_W + LORA_A + LORA_G
B_HD = 128
B_HEADS = BR_W // B_HD
C_HD = 128
C_HEADS = BR_W // C_HD
B_COLS = 4 * BR_W
C_COLS = 4 * BR_W
GATE_COLS = N_BRANCH * D_MODEL
IN_COLS = A_COLS + B_COLS + C_COLS + GATE_COLS
D_FF = ((8 * D_MODEL // 3 + 255) // 256) * 256
CHUNK = 64
RMS_EPS = 1e-6
RWKV_GN_EPS = 64e-5
GN_EPS = 1e-5
F_FLOOR = 1e-30
ROPE_BASE = 10000.0

kernel_name = "hybrid_rwkv7_retnet_hgrn2_step"


def _rmsnorm(x, g):
    xf = x.astype(jnp.float32)
    y = xf * lax.rsqrt(jnp.mean(xf * xf, axis=-1, keepdims=True) + RMS_EPS)
    return (y * g.astype(jnp.float32)).astype(x.dtype)


def _heads(x, n_heads):
    return x.reshape(x.shape[:-1] + (n_heads, x.shape[-1] // n_heads))


def _head_groupnorm(o, eps):
    mu = jnp.mean(o, axis=-1, keepdims=True)
    var = jnp.mean(jnp.square(o - mu), axis=-1, keepdims=True)
    return (o - mu) * lax.rsqrt(var + eps)


def _head_rmsnorm(o, eps):
    return o * lax.rsqrt(jnp.mean(o * o, axis=-1, keepdims=True) + eps)


def _rope(x, pos):
    half = x.shape[-1] // 2
    inv = ROPE_BASE ** (-jnp.arange(half, dtype=jnp.float32) / half)
    ang = pos.astype(jnp.float32)[:, None] * inv[None, :]
    cos = jnp.cos(ang)[None, :, None, :]
    sin = jnp.sin(ang)[None, :, None, :]
    x1, x2 = x[..., :half], x[..., half:]
    return jnp.concatenate([x1 * cos - x2 * sin, x1 * sin + x2 * cos], axis=-1)


def _rwkv7(u, u_prev, s0, mu, w0, w2, a0, a2, g2, k_k, k_a, r_k, ln_w, ln_b):
    f32 = jnp.float32
    B, T, _ = u.shape
    u = u.astype(f32)
    shifted = jnp.concatenate([u_prev.astype(f32)[:, None], u[:, :-1]], axis=1)
    xm = u + mu * (shifted - u)
    cuts = [BR_W, 2 * BR_W, 3 * BR_W, 3 * BR_W + LORA_W, 3 * BR_W + LORA_W + LORA_A]
    r, k, v, wl, al, gl = jnp.split(xm, cuts, axis=-1)
    log_w = -jax.nn.softplus(-(w0 + jnp.tanh(wl) @ w2)) - 0.5
    decay = jnp.exp(-jnp.exp(log_w))
    a = jax.nn.sigmoid(a0 + al @ a2)
    g = jax.nn.sigmoid(gl) @ g2
    kk = _heads(k * k_k, A_HEADS)
    kk = kk / jnp.maximum(jnp.sqrt(jnp.sum(kk * kk, axis=-1, keepdims=True)), 1e-12)
    k = k * (1.0 + (a - 1.0) * k_a)
    r_h, k_h, v_h = _heads(r, A_HEADS), _heads(k, A_HEADS), _heads(v, A_HEADS)
    w_h, a_h = _heads(decay, A_HEADS), _heads(a, A_HEADS)

    def step(S, inp):
        r_t, k_t, v_t, w_t, kk_t, a_t = inp
        sa = jnp.einsum('bhvk,bhk->bhv', S, -kk_t)
        S = (S * w_t[:, :, None, :] + sa[..., :, None] * (kk_t * a_t)[..., None, :]
             + v_t[..., :, None] * k_t[..., None, :])
        return S, jnp.einsum('bhvk,bhk->bhv', S, r_t)

    xs = tuple(jnp.moveaxis(t, 1, 0) for t in (r_h, k_h, v_h, w_h, kk, a_h))
    s_end, o = lax.scan(step, s0.astype(f32), xs)
    o = jnp.moveaxis(o, 0, 1)
    o = _head_groupnorm(o, RWKV_GN_EPS) * _heads(ln_w, A_HEADS) + _heads(ln_b, A_HEADS)
    o = o + jnp.sum(r_h * k_h * _heads(r_k, A_HEADS), axis=-1, keepdims=True) * v_h
    return o.reshape(B, T, BR_W) * g, u[:, -1], s_end


def _retention(q, k, v, s0, log_gamma):
    B, T, H, N = q.shape
    C = math.gcd(T, CHUNK)
    n = T // C
    chunks = lambda t: t.reshape(B, n, C, H, t.shape[-1]).transpose(1, 0, 3, 2, 4)
    idx = jnp.arange(C, dtype=jnp.float32)
    diff = idx[:, None] - idx[None, :]
    lg = log_gamma[:, None, None]
    causal = (diff >= 0).astype(jnp.float32)
    intra = jnp.exp(lg * jnp.maximum(diff, 0.0)) * causal
    q_dec = jnp.exp(log_gamma[:, None] * (idx + 1.0))
    k_dec = jnp.exp(log_gamma[:, None] * (C - 1.0 - idx))
    s_dec = jnp.exp(log_gamma * C)

    def step(S, inp):
        qc, kc, vc = inp
        scores = jnp.einsum('bhtd,bhsd->bhts', qc, kc) * intra
        o = (jnp.einsum('bhts,bhsv->bhtv', scores, vc)
             + jnp.einsum('bhtd,bhdv->bhtv', qc * q_dec[..., None], S))
        S = S * s_dec[:, None, None] + jnp.einsum('bhsd,bhsv->bhdv', kc * k_dec[..., None], vc)
        return S, o

    S, o = lax.scan(step, s0, (chunks(q), chunks(k), chunks(v)))
    return o.transpose(1, 0, 3, 2, 4).reshape(B, T, H, N), S


def _hgrn2(q, lf, k, v, s0):
    B, T, H, N = q.shape
    C = math.gcd(T, CHUNK)
    n = T // C
    chunks = lambda t: t.reshape(B, n, C, H, N).transpose(1, 0, 3, 2, 4)
    mask = jnp.tril(jnp.ones((C, C), dtype=jnp.float32))[None, None, :, :, None]

    def step(S, inp):
        qc, lfc, kc, vc = inp
        b = jnp.cumsum(lfc, axis=2)
        rel = jnp.minimum(b[:, :, :, None, :] - b[:, :, None, :, :], 0.0)
        scores = jnp.einsum('bhtd,bhsd,bhtsd->bhts', qc, kc, jnp.exp(rel) * mask)
        o = (jnp.einsum('bhts,bhsv->bhtv', scores, vc)
             + jnp.einsum('bhtd,bhdv->bhtv', qc * jnp.exp(b), S))
        b_last = b[:, :, -1:, :]
        S = (S * jnp.exp(b_last)[:, :, 0, :, None]
             + jnp.einsum('bhsd,bhsv->bhdv', kc * jnp.exp(b_last - b), vc))
        return S, o

    S, o = lax.scan(step, s0, (chunks(q), chunks(lf), chunks(k), chunks(v)))
    return o.transpose(1, 0, 3, 2, 4).reshape(B, T, H, N), S


def _layer(x, pos, shift, wkv, ret, hg, lb, log_gamma, P):
    f32 = jnp.float32
    B, T, _ = x.shape
    h = _rmsnorm(x, P['norm_mix'])
    u = h @ P['w_in']
    u_a, u_b, u_c, u_g = jnp.split(u, [A_COLS, A_COLS + B_COLS, A_COLS + B_COLS + C_COLS], axis=-1)
    o_a, shift_new, wkv_new = _rwkv7(u_a, shift, wkv, P['rwkv_mu'], P['rwkv_w0'], P['rwkv_w2'],
                                     P['rwkv_a0'], P['rwkv_a2'], P['rwkv_g2'], P['rwkv_kk'],
                                     P['rwkv_ka'], P['rwkv_rk'], P['rwkv_ln_w'], P['rwkv_ln_b'])
    qb, kb, vb, gb = jnp.split(u_b.astype(f32), 4, axis=-1)
    qb = _rope(_heads(qb, B_HEADS), pos)
    kb = _rope(_heads(kb, B_HEADS), pos) * (B_HD ** -0.5)
    ob, ret_new = _retention(qb, kb, _heads(vb, B_HEADS), ret.astype(f32), log_gamma)
    ob = _head_groupnorm(ob, GN_EPS).reshape(B, T, BR_W) * P['ret_gn'] * jax.nn.silu(gb)
    qc, fc, ic, gc = jnp.split(u_c.astype(f32), 4, axis=-1)
    sig_f = jax.nn.sigmoid(fc)
    f_gate = lb + (1.0 - lb) * sig_f
    lf = jnp.log(jnp.maximum(f_gate, F_FLOOR))
    kc = (1.0 - lb) * (1.0 - sig_f)
    oc, hg_new = _hgrn2(_heads(qc, C_HEADS), _heads(lf, C_HEADS), _heads(kc, C_HEADS),
                        _heads(ic, C_HEADS), hg.astype(f32))
    oc = _head_rmsnorm(oc, GN_EPS).reshape(B, T, BR_W) * P['hgrn_gn'] * jax.nn.silu(gc)
    o = jnp.stack([o_a, ob, oc], axis=2).astype(x.dtype)
    p = jnp.einsum('btcw,cwd->btcd', o, P['w_branch'])
    gates = jax.nn.sigmoid(u_g.reshape(B, T, N_BRANCH, D_MODEL))
    m = jnp.sum(gates * p, axis=2)
    x = x + (m @ P['w_out']).astype(x.dtype)
    h2 = _rmsnorm(x, P['norm_ffn'])
    gt, up = jnp.split(h2 @ P['w_ffn_in'], 2, axis=-1)
    x = x + ((jax.nn.silu(gt) * up) @ P['w_ffn_out']).astype(x.dtype)
    dt = x.dtype
    return x, shift_new.astype(dt), wkv_new.astype(dt), ret_new.astype(dt), hg_new.astype(dt)


def setup_inputs(seed: int = 0) -> dict:
    key = jax.random.key(seed)
    ks = iter(jax.random.split(key, 32))
    nrm = lambda shape, s: jax.random.normal(next(ks), shape, jnp.float32) * s
    return {
        "x_prompt": nrm((BATCH, SEQ, D_MODEL), 1.0),
        "x_sample": nrm((DEC_BATCH, DEC_SEQ, D_MODEL), 1.0),
        "state_rwkv_shift": nrm((DEPTH, DEC_BATCH, A_COLS), 1.0),
        "state_rwkv_wkv": nrm((DEPTH, DEC_BATCH, A_HEADS, A_HD, A_HD), 0.3),
        "state_ret": nrm((DEPTH, DEC_BATCH, B_HEADS, B_HD, B_HD), 0.3),
        "state_hgrn": nrm((DEPTH, DEC_BATCH, C_HEADS, C_HD, C_HD), 0.3),
        "norm_mix": 1.0 + nrm((DEPTH, D_MODEL), 0.02),
        "w_in": nrm((DEPTH, D_MODEL, IN_COLS), D_MODEL ** -0.5),
        "rwkv_mu": jax.random.uniform(next(ks), (DEPTH, A_COLS), jnp.float32),
        "rwkv_w0": jax.random.uniform(next(ks), (DEPTH, BR_W), jnp.float32, -6.0, -1.0),
        "rwkv_w2": nrm((DEPTH, LORA_W, BR_W), 0.5 * LORA_W ** -0.5),
        "rwkv_a0": nrm((DEPTH, BR_W), 0.1),
        "rwkv_a2": nrm((DEPTH, LORA_A, BR_W), LORA_A ** -0.5),
        "rwkv_g2": nrm((DEPTH, LORA_G, BR_W), LORA_G ** -0.5),
        "rwkv_kk": 0.85 + nrm((DEPTH, BR_W), 0.02),
        "rwkv_ka": 1.0 + nrm((DEPTH, BR_W), 0.02),
        "rwkv_rk": nrm((DEPTH, BR_W), 0.1),
        "rwkv_ln_w": 1.0 + nrm((DEPTH, BR_W), 0.02),
        "rwkv_ln_b": nrm((DEPTH, BR_W), 0.02),
        "ret_gn": 1.0 + nrm((DEPTH, BR_W), 0.02),
        "hgrn_lb": nrm((DEPTH, BR_W), 1.0),
        "hgrn_gn": 1.0 + nrm((DEPTH, BR_W), 0.02),
        "w_branch": nrm((DEPTH, N_BRANCH, BR_W, D_MODEL), BR_W ** -0.5),
        "w_out": nrm((DEPTH, D_MODEL, D_MODEL), D_MODEL ** -0.5),
        "norm_ffn": 1.0 + nrm((DEPTH, D_MODEL), 0.02),
        "w_ffn_in": nrm((DEPTH, D_MODEL, 2 * D_FF), D_MODEL ** -0.5),
        "w_ffn_out": nrm((DEPTH, D_FF, D_MODEL), D_FF ** -0.5),
        "norm_final": 1.0 + nrm((D_MODEL,), 0.02),
    }


def reference(x_prompt, x_sample, state_rwkv_shift, state_rwkv_wkv, state_ret, state_hgrn,
              norm_mix, w_in, rwkv_mu, rwkv_w0, rwkv_w2, rwkv_a0, rwkv_a2, rwkv_g2, rwkv_kk,
              rwkv_ka, rwkv_rk, rwkv_ln_w, rwkv_ln_b, ret_gn, hgrn_lb, hgrn_gn, w_branch, w_out,
              norm_ffn, w_ffn_in, w_ffn_out, norm_final):
    f32 = jnp.float32
    sm = jax.nn.softmax(hgrn_lb.astype(f32), axis=0)
    lower = jnp.cumsum(sm, axis=0) - sm[0:1]
    log_gamma = jnp.log(1.0 - 2.0 ** (-5.0 - jnp.arange(B_HEADS, dtype=f32)))
    pos_p = jnp.arange(x_prompt.shape[1], dtype=jnp.int32)
    pos_s = PAST_LEN + jnp.arange(x_sample.shape[1], dtype=jnp.int32)
    bp = x_prompt.shape[0]
    dt = x_prompt.dtype
    z_shift = jnp.zeros((bp, A_COLS), dt)
    z_wkv = jnp.zeros((bp, A_HEADS, A_HD, A_HD), dt)
    z_ret = jnp.zeros((bp, B_HEADS, B_HD, B_HD), dt)
    z_hg = jnp.zeros((bp, C_HEADS, C_HD, C_HD), dt)
    xp, xs = x_prompt, x_sample
    sp, wp, rp, hp, ss, ws, rs, hs = [], [], [], [], [], [], [], []
    for l in range(DEPTH):
        P = {
            'norm_mix': norm_mix[l], 'w_in': w_in[l], 'rwkv_mu': rwkv_mu[l],
            'rwkv_w0': rwkv_w0[l], 'rwkv_w2': rwkv_w2[l], 'rwkv_a0': rwkv_a0[l],
            'rwkv_a2': rwkv_a2[l], 'rwkv_g2': rwkv_g2[l], 'rwkv_kk': rwkv_kk[l],
            'rwkv_ka': rwkv_ka[l], 'rwkv_rk': rwkv_rk[l], 'rwkv_ln_w': rwkv_ln_w[l],
            'rwkv_ln_b': rwkv_ln_b[l], 'ret_gn': ret_gn[l], 'hgrn_gn': hgrn_gn[l],
            'w_branch': w_branch[l], 'w_out': w_out[l], 'norm_ffn': norm_ffn[l],
            'w_ffn_in': w_ffn_in[l], 'w_ffn_out': w_ffn_out[l],
        }
        xp, a1, a2, a3, a4 = _layer(xp, pos_p, z_shift, z_wkv, z_ret, z_hg, lower[l], log_gamma, P)
        xs, b1, b2, b3, b4 = _layer(xs, pos_s, state_rwkv_shift[l], state_rwkv_wkv[l],
                                    state_ret[l], state_hgrn[l], lower[l], log_gamma, P)
        sp.append(a1); wp.append(a2); rp.append(a3); hp.append(a4)
        ss.append(b1); ws.append(b2); rs.append(b3); hs.append(b4)
    y_prompt = _rmsnorm(xp, norm_final)
    y_sample = _rmsnorm(xs, norm_final)
    return (y_prompt, y_sample, jnp.stack(sp), jnp.stack(wp), jnp.stack(rp), jnp.stack(hp),
            jnp.stack(ss), jnp.stack(ws), jnp.stack(rs), jnp.stack(hs))
```

```python
import functools
import math

import jax
import jax.numpy as jnp
from jax import lax
from jax.experimental import pallas as pl
from jax.experimental.pallas import tpu as pltpu

F32 = jnp.float32
BF16 = jnp.bfloat16

D_MODEL = 1024
BR_W = 512
A_HD = 64
A_HEADS = BR_W // A_HD
LORA_W = 64
LORA_A = 64
LORA_G = 128
A_COLS = 3 * BR_W + LORA_W + LORA_A + LORA_G
B_HD = 128
B_HEADS = BR_W // B_HD
C_HD = 128
C_HEADS = BR_W // C_HD
B_COLS = 4 * BR_W
C_COLS = 4 * BR_W
GATE_COLS = 3 * D_MODEL
IN_COLS = A_COLS + B_COLS + C_COLS + GATE_COLS
D_FF = 2816
PAST_LEN = 16384
RMS_EPS = 1e-6
RWKV_GN_EPS = 64e-5
GN_EPS = 1e-5
F_FLOOR = 1e-30
ROPE_BASE = 10000.0

U_B_OFF = 0
U_C_OFF = B_COLS
U_G_OFF = B_COLS + C_COLS
U_A_OFF = B_COLS + C_COLS + GATE_COLS

PROMPT_CHUNK = 64
SAMPLE_CHUNK = 8
SUB = 8

VMEM_LIMIT = 56 * 1024 * 1024

NN = (((1,), (0,)), ((), ()))
NT = (((1,), (1,)), ((), ()))
TN = (((0,), (0,)), ((), ()))


def _bdot(a, b, dims=NN):
    return lax.dot_general(a.astype(BF16), b.astype(BF16), dims, preferred_element_type=F32)


def _fdot(a, b, dims=NN):
    return lax.dot_general(a, b, dims, precision=lax.Precision.HIGHEST, preferred_element_type=F32)


def _sigmoid(x):
    return 1.0 / (1.0 + jnp.exp(-x))


def _silu(x):
    return x * _sigmoid(x)


def _cparams(sem):
    return pltpu.CompilerParams(dimension_semantics=sem, vmem_limit_bytes=VMEM_LIMIT)


def _lower_kernel(lb_ref, o_ref):
    x = lb_ref[...]
    depth = x.shape[0]
    m = x[0:1]
    for l in range(1, depth):
        m = jnp.maximum(m, x[l:l + 1])
    e = jnp.exp(x - m)
    tot = e[0:1]
    for l in range(1, depth):
        tot = tot + e[l:l + 1]
    sm = e / tot
    acc = jnp.zeros_like(m)
    for l in range(depth):
        acc = acc + sm[l:l + 1]
        o_ref[l:l + 1, :] = acc - sm[0:1]


def _lower_bounds(hgrn_lb):
    return pl.pallas_call(
        _lower_kernel, out_shape=jax.ShapeDtypeStruct(hgrn_lb.shape, F32), name="hgrn_lower",
    )(hgrn_lb.astype(F32))


def _in_proj_kernel(x_ref, g_ref, w_ref, o_ref, h_ref):
    @pl.when(pl.program_id(1) == 0)
    def _():
        x = x_ref[...]
        ms = jnp.mean(x * x, axis=-1, keepdims=True)
        h_ref[...] = (x * lax.rsqrt(ms + RMS_EPS) * g_ref[...]).astype(BF16)

    o_ref[...] = jnp.dot(h_ref[...], w_ref[...], preferred_element_type=F32)


def _in_proj(x, g, w, tm, tn):
    n, d = x.shape
    cols = w.shape[1]
    return pl.pallas_call(
        _in_proj_kernel,
        grid=(n // tm, cols // tn),
        in_specs=[
            pl.BlockSpec((tm, d), lambda i, j: (i, 0)),
            pl.BlockSpec((1, d), lambda i, j: (0, 0)),
            pl.BlockSpec((d, tn), lambda i, j: (0, j)),
        ],
        out_specs=pl.BlockSpec((tm, tn), lambda i, j: (i, j)),
        out_shape=jax.ShapeDtypeStruct((n, cols), F32),
        scratch_shapes=[pltpu.VMEM((tm, d), BF16)],
        compiler_params=_cparams(("parallel", "arbitrary")),
        name="in_proj",
    )(x, g, w)


def _neumann_inverse(a, size):
    r = lax.broadcasted_iota(jnp.int32, (size, size), 0)
    c = lax.broadcasted_iota(jnp.int32, (size, size), 1)
    t = jnp.where(r == c, 1.0, 0.0).astype(F32) + a
    p = a
    span = 2
    while span < size:
        p = _bdot(p, p)
        t = t + _bdot(t, p)
        span *= 2
    return t


def _rwkv_kernel(u_ref, shift_ref, s0_ref, mu_ref, w0_ref, w2_ref, a0_ref, a2_ref, g2_ref,
                 kk_ref, ka_ref, rk_ref, lnw_ref, lnb_ref, hsum_ref,
                 o_ref, shift_out_ref, s_out_ref, s_scr, prev_scr, *, chunk, valid, n_chunks):
    n = pl.program_id(1)

    @pl.when(n == 0)
    def _():
        s_scr[...] = s0_ref[...]
        prev_scr[...] = shift_ref[...]

    u = u_ref[...]
    row = lax.broadcasted_iota(jnp.int32, (chunk, 1), 0)
    shifted = jnp.where(row == 0, prev_scr[...], pltpu.roll(u, 1, axis=0))
    xm = u + mu_ref[...] * (shifted - u)
    prev_scr[...] = u[chunk - 1:chunk, :]

    r = xm[:, 0:BR_W]
    k = xm[:, BR_W:2 * BR_W]
    v = xm[:, 2 * BR_W:3 * BR_W]
    wl = xm[:, 3 * BR_W:3 * BR_W + LORA_W]
    al = xm[:, 3 * BR_W + LORA_W:3 * BR_W + LORA_W + LORA_A]
    gl = xm[:, 3 * BR_W + LORA_W + LORA_A:A_COLS]

    z = w0_ref[...] + _bdot(jnp.tanh(wl), w2_ref[...])
    ld = -math.exp(-0.5) * _sigmoid(z)
    a = _sigmoid(a0_ref[...] + _bdot(al, a2_ref[...]))
    g = _bdot(_sigmoid(gl), g2_ref[...])
    kk = k * kk_ref[...]
    ss = _fdot(kk * kk, hsum_ref[...])
    kk = kk / jnp.maximum(jnp.sqrt(ss), 1e-12)
    k = k * (1.0 + (a - 1.0) * ka_ref[...])
    if valid < chunk:
        ok = row < valid
        ld = jnp.where(ok, ld, 0.0)
        kk = jnp.where(ok, kk, 0.0)
        k = jnp.where(ok, k, 0.0)
    bv = kk * a

    tr = lax.broadcasted_iota(jnp.int32, (chunk, chunk), 0)
    tc = lax.broadcasted_iota(jnp.int32, (chunk, chunk), 1)
    lower_incl = tr >= tc
    lower_strict = tr > tc
    cl = _fdot(jnp.where(lower_incl, 1.0, 0.0).astype(F32), ld)
    cl_last = cl[chunk - 1:chunk, :]
    e_cl = jnp.exp(cl)
    e_ncl = jnp.exp(-cl)
    alpha_t = -kk * jnp.exp(cl - ld)
    r_t = r * e_cl
    b_bar = bv * e_ncl
    k_bar = k * e_ncl
    e_rem = jnp.exp(cl_last - cl)
    b_hat = bv * e_rem
    k_hat = k * e_rem
    e_last = jnp.exp(cl_last)
    rkk = r * k * rk_ref[...]

    for h in range(A_HEADS):
        sl = slice(h * A_HD, (h + 1) * A_HD)
        s0 = s_scr[h]
        a_ab = jnp.where(lower_strict, _bdot(alpha_t[:, sl], b_bar[:, sl], NT), 0.0)
        a_ak = jnp.where(lower_strict, _bdot(alpha_t[:, sl], k_bar[:, sl], NT), 0.0)
        a_rb = jnp.where(lower_incl, _bdot(r_t[:, sl], b_bar[:, sl], NT), 0.0)
        a_rk = jnp.where(lower_incl, _bdot(r_t[:, sl], k_bar[:, sl], NT), 0.0)
        tinv = _neumann_inverse(a_ab, chunk)
        vh = v[:, sl]
        uu = _bdot(tinv, _bdot(alpha_t[:, sl], s0, NT) + _bdot(a_ak, vh))
        o = _bdot(r_t[:, sl], s0, NT) + _bdot(a_rb, uu) + _bdot(a_rk, vh)
        s_scr[h] = s0 * e_last[:, sl] + _bdot(uu, b_hat[:, sl], TN) + _bdot(vh, k_hat[:, sl], TN)
        mu = jnp.mean(o, axis=-1, keepdims=True)
        var = jnp.mean(jnp.square(o - mu), axis=-1, keepdims=True)
        o = (o - mu) * lax.rsqrt(var + RWKV_GN_EPS) * lnw_ref[:, sl] + lnb_ref[:, sl]
        o = o + jnp.sum(rkk[:, sl], axis=-1, keepdims=True) * vh
        o_ref[:, sl] = o * g[:, sl]

    @pl.when(n == n_chunks - 1)
    def _():
        last = (valid - 1) if n_chunks == 1 else (chunk - 1)
        shift_out_ref[...] = u[last:last + 1, :]
        s_out_ref[...] = s_scr[...]


def _rwkv(u, shift, wkv, p, batch, seq, chunk, valid):
    n_chunks = seq // chunk
    a_blk = U_A_OFF // A_COLS
    vec = lambda w: pl.BlockSpec((1, w), lambda b, n: (0, 0))
    full = lambda s: pl.BlockSpec(s, lambda b, n: (0,) * len(s))
    kern = functools.partial(_rwkv_kernel, chunk=chunk, valid=valid, n_chunks=n_chunks)
    return pl.pallas_call(
        kern,
        grid=(batch, n_chunks),
        in_specs=[
            pl.BlockSpec((chunk, A_COLS), lambda b, n: (b * n_chunks + n, a_blk)),
            pl.BlockSpec((None, 1, A_COLS), lambda b, n: (b, 0, 0)),
            pl.BlockSpec((None, A_HEADS, A_HD, A_HD), lambda b, n: (b, 0, 0, 0)),
            vec(A_COLS), vec(BR_W), full((LORA_W, BR_W)), vec(BR_W), full((LORA_A, BR_W)),
            full((LORA_G, BR_W)), vec(BR_W), vec(BR_W), vec(BR_W), vec(BR_W), vec(BR_W),
            full((BR_W, BR_W)),
        ],
        out_specs=[
            pl.BlockSpec((chunk, BR_W), lambda b, n: (b * n_chunks + n, 0)),
            pl.BlockSpec((None, 1, A_COLS), lambda b, n: (b, 0, 0)),
            pl.BlockSpec((None, A_HEADS, A_HD, A_HD), lambda b, n: (b, 0, 0, 0)),
        ],
        out_shape=[
            jax.ShapeDtypeStruct((batch * seq, BR_W), F32),
            jax.ShapeDtypeStruct((batch, 1, A_COLS), F32),
            jax.ShapeDtypeStruct((batch, A_HEADS, A_HD, A_HD), F32),
        ],
        scratch_shapes=[pltpu.VMEM((A_HEADS, A_HD, A_HD), F32), pltpu.VMEM((1, A_COLS), F32)],
        compiler_params=_cparams(("parallel", "arbitrary")),
        name="rwkv7",
    )(u, shift, wkv, p["mu"], p["w0"], p["w2"], p["a0"], p["a2"], p["g2"], p["kk"], p["ka"],
      p["rk"], p["lnw"], p["lnb"], p["hsum"])


def _ret_kernel(u_ref, cos_ref, sin_ref, s0_ref, gn_ref, o_ref, s_out_ref, s_scr,
                *, chunk, valid, n_chunks):
    n = pl.program_id(1)

    @pl.when(n == 0)
    def _():
        s_scr[...] = s0_ref[...]

    u = u_ref[...]
    cos = cos_ref[...]
    sin = sin_ref[...]
    tr = lax.broadcasted_iota(jnp.int32, (chunk, chunk), 0)
    tc = lax.broadcasted_iota(jnp.int32, (chunk, chunk), 1)
    diff = (tr - tc).astype(F32)
    causal = tr >= tc
    pos = lax.broadcasted_iota(jnp.int32, (chunk, 1), 0)
    posf = pos.astype(F32)

    for h in range(B_HEADS):
        lg = math.log(1.0 - 2.0 ** (-5.0 - h))
        sl = slice(h * B_HD, (h + 1) * B_HD)
        q = u[:, sl]
        k = u[:, BR_W + h * B_HD:BR_W + (h + 1) * B_HD]
        v = u[:, 2 * BR_W + h * B_HD:2 * BR_W + (h + 1) * B_HD]
        gt = u[:, 3 * BR_W + h * B_HD:3 * BR_W + (h + 1) * B_HD]
        q = q * cos + pltpu.roll(q, B_HD // 2, axis=1) * sin
        k = (k * cos + pltpu.roll(k, B_HD // 2, axis=1) * sin) * (B_HD ** -0.5)
        intra = jnp.where(causal, jnp.exp(lg * jnp.maximum(diff, 0.0)), 0.0)
        q_dec = jnp.exp(lg * (posf + 1.0))
        k_dec = jnp.where(pos < valid, jnp.exp(lg * (valid - 1.0 - posf)), 0.0)
        s_dec = math.exp(lg * valid)
        s0 = s_scr[h]
        scores = _bdot(q, k, NT) * intra
        o = _bdot(scores, v) + _bdot(q * q_dec, s0)
        s_scr[h] = s0 * s_dec + _bdot(k * k_dec, v, TN)
        mu = jnp.mean(o, axis=-1, keepdims=True)
        var = jnp.mean(jnp.square(o - mu), axis=-1, keepdims=True)
        o = (o - mu) * lax.rsqrt(var + GN_EPS)
        o_ref[:, sl] = o * gn_ref[:, sl] * _silu(gt)

    @pl.when(n == n_chunks - 1)
    def _():
        s_out_ref[...] = s_scr[...]


def _retention(u, cos, sin, state, gn, batch, seq, chunk, valid):
    n_chunks = seq // chunk
    kern = functools.partial(_ret_kernel, chunk=chunk, valid=valid, n_chunks=n_chunks)
    return pl.pallas_call(
        kern,
        grid=(batch, n_chunks),
        in_specs=[
            pl.BlockSpec((chunk, B_COLS), lambda b, n: (b * n_chunks + n, U_B_OFF // B_COLS)),
            pl.BlockSpec((chunk, B_HD), lambda b, n: (n, 0)),
            pl.BlockSpec((chunk, B_HD), lambda b, n: (n, 0)),
            pl.BlockSpec((None, B_HEADS, B_HD, B_HD), lambda b, n: (b, 0, 0, 0)),
            pl.BlockSpec((1, BR_W), lambda b, n: (0, 0)),
        ],
        out_specs=[
            pl.BlockSpec((chunk, BR_W), lambda b, n: (b * n_chunks + n, 0)),
            pl.BlockSpec((None, B_HEADS, B_HD, B_HD), lambda b, n: (b, 0, 0, 0)),
        ],
        out_shape=[
            jax.ShapeDtypeStruct((batch * seq, BR_W), F32),
            jax.ShapeDtypeStruct((batch, B_HEADS, B_HD, B_HD), F32),
        ],
        scratch_shapes=[pltpu.VMEM((B_HEADS, B_HD, B_HD), F32)],
        compiler_params=_cparams(("parallel", "arbitrary")),
        name="retention",
    )(u, cos, sin, state, gn)


def _hgrn_kernel(u_ref, lb_ref, s0_ref, gn_ref, o_ref, s_out_ref, st_scr,
                 *, chunk, valid, n_chunks):
    n = pl.program_id(1)

    @pl.when(n == 0)
    def _():
        for h in range(C_HEADS):
            st_scr[h] = s0_ref[h].T

    u = u_ref[...]
    lb = lb_ref[...]
    row = lax.broadcasted_iota(jnp.int32, (chunk, 1), 0)
    sig_f = _sigmoid(u[:, BR_W:2 * BR_W])
    lf = jnp.log(jnp.maximum(lb + (1.0 - lb) * sig_f, F_FLOOR))
    kc = (1.0 - lb) * (1.0 - sig_f)
    if valid < chunk:
        ok = row < valid
        lf = jnp.where(ok, lf, 0.0)
        kc = jnp.where(ok, kc, 0.0)
    tr = lax.broadcasted_iota(jnp.int32, (chunk, chunk), 0)
    tc = lax.broadcasted_iota(jnp.int32, (chunk, chunk), 1)
    bc = _fdot(jnp.where(tr >= tc, 1.0, 0.0).astype(F32), lf)
    b_last = bc[chunk - 1:chunk, :]
    n_sub = chunk // SUB
    lane = lax.broadcasted_iota(jnp.int32, (SUB, chunk), 1)

    for h in range(C_HEADS):
        sl = slice(h * C_HD, (h + 1) * C_HD)
        q = u[:, sl]
        k = kc[:, sl]
        v = u[:, 2 * BR_W + h * C_HD:2 * BR_W + (h + 1) * C_HD]
        gt = u[:, 3 * BR_W + h * C_HD:3 * BR_W + (h + 1) * C_HD]
        b = bc[:, sl]
        st = st_scr[h]

        scores = jnp.zeros((chunk, chunk), F32)
        size = SUB
        while size < chunk:
            qs, ks = [], []
            for j in range(chunk // size):
                blk = slice(j * size, (j + 1) * size)
                if j % 2 == 1:
                    ref = b[j * size - 1:j * size, :]
                    qs.append(q[blk] * jnp.exp(b[blk] - ref))
                    ks.append(jnp.zeros((size, C_HD), F32))
                else:
                    ref = b[(j + 1) * size - 1:(j + 1) * size, :]
                    qs.append(jnp.zeros((size, C_HD), F32))
                    ks.append(k[blk] * jnp.exp(ref - b[blk]))
            lvl = _bdot(jnp.concatenate(qs, axis=0), jnp.concatenate(ks, axis=0), NT)
            sh = (2 * size).bit_length() - 1
            same_pair = (tr >> sh) == (tc >> sh)
            scores = scores + jnp.where(same_pair, lvl, 0.0)
            size *= 2

        diag_rows = []
        for i in range(n_sub):
            blk = slice(i * SUB, (i + 1) * SUB)
            qb, kb, bb = q[blk], k[blk], b[blk]
            d = jnp.zeros((SUB, chunk), F32)
            for s in range(SUB):
                w = jnp.exp(jnp.minimum(bb - bb[s:s + 1, :], 0.0))
                col = jnp.sum(qb * kb[s:s + 1, :] * w, axis=-1, keepdims=True)
                t_ok = lax.broadcasted_iota(jnp.int32, (SUB, 1), 0) >= s
                d = d + jnp.where((lane == i * SUB + s) & t_ok, col, 0.0)
            diag_rows.append(d)
        scores = scores + jnp.concatenate(diag_rows, axis=0)

        o = _bdot(scores, v) + _bdot(q * jnp.exp(b), st, NT)
        st_scr[h] = st * jnp.exp(b_last[:, sl]) + _bdot(v, k * jnp.exp(b_last[:, sl] - b), TN)
        o = o * lax.rsqrt(jnp.mean(o * o, axis=-1, keepdims=True) + GN_EPS)
        o_ref[:, sl] = o * gn_ref[:, sl] * _silu(gt)

    @pl.when(n == n_chunks - 1)
    def _():
        for h in range(C_HEADS):
            s_out_ref[h] = st_scr[h].T


def _hgrn(u, lb, state, gn, batch, seq, chunk, valid):
    n_chunks = seq // chunk
    kern = functools.partial(_hgrn_kernel, chunk=chunk, valid=valid, n_chunks=n_chunks)
    return pl.pallas_call(
        kern,
        grid=(batch, n_chunks),
        in_specs=[
            pl.BlockSpec((chunk, C_COLS), lambda b, n: (b * n_chunks + n, U_C_OFF // C_COLS)),
            pl.BlockSpec((1, BR_W), lambda b, n: (0, 0)),
            pl.BlockSpec((None, C_HEADS, C_HD, C_HD), lambda b, n: (b, 0, 0, 0)),
            pl.BlockSpec((1, BR_W), lambda b, n: (0, 0)),
        ],
        out_specs=[
            pl.BlockSpec((chunk, BR_W), lambda b, n: (b * n_chunks + n, 0)),
            pl.BlockSpec((None, C_HEADS, C_HD, C_HD), lambda b, n: (b, 0, 0, 0)),
        ],
        out_shape=[
            jax.ShapeDtypeStruct((batch * seq, BR_W), F32),
            jax.ShapeDtypeStruct((batch, C_HEADS, C_HD, C_HD), F32),
        ],
        scratch_shapes=[pltpu.VMEM((C_HEADS, C_HD, C_HD), F32)],
        compiler_params=_cparams(("parallel", "arbitrary")),
        name="hgrn2",
    )(u, lb, state, gn)


def _merge_kernel(oa_ref, ob_ref, oc_ref, g0_ref, g1_ref, g2_ref, x_ref, wb_ref, wo_ref, o_ref):
    m = _sigmoid(g0_ref[...]) * _bdot(oa_ref[...], wb_ref[0])
    m = m + _sigmoid(g1_ref[...]) * _bdot(ob_ref[...], wb_ref[1])
    m = m + _sigmoid(g2_ref[...]) * _bdot(oc_ref[...], wb_ref[2])
    o_ref[...] = x_ref[...] + _bdot(m, wo_ref[...])


def _merge(oa, ob, oc, u, x, wb, wo, tm):
    n = x.shape[0]
    g_blk = U_G_OFF // D_MODEL
    br = pl.BlockSpec((tm, BR_W), lambda i: (i, 0))
    gate = lambda c: pl.BlockSpec((tm, D_MODEL), lambda i: (i, g_blk + c))
    return pl.pallas_call(
        _merge_kernel,
        grid=(n // tm,),
        in_specs=[br, br, br, gate(0), gate(1), gate(2),
                  pl.BlockSpec((tm, D_MODEL), lambda i: (i, 0)),
                  pl.BlockSpec((3, BR_W, D_MODEL), lambda i: (0, 0, 0)),
                  pl.BlockSpec((D_MODEL, D_MODEL), lambda i: (0, 0))],
        out_specs=pl.BlockSpec((tm, D_MODEL), lambda i: (i, 0)),
        out_shape=jax.ShapeDtypeStruct((n, D_MODEL), F32),
        compiler_params=_cparams(("parallel",)),
        name="merge",
    )(oa, ob, oc, u, u, u, x, wb, wo)


def _ffn_kernel(x_ref, g_ref, wg_ref, wu_ref, wo_ref, o_ref, h_ref, *, n_ff):
    j = pl.program_id(1)

    @pl.when(j == 0)
    def _():
        x = x_ref[...]
        ms = jnp.mean(x * x, axis=-1, keepdims=True)
        h_ref[...] = (x * lax.rsqrt(ms + RMS_EPS) * g_ref[...]).astype(BF16)
        o_ref[...] = x

    h = h_ref[...]
    gt = jnp.dot(h, wg_ref[...], preferred_element_type=F32)
    up = jnp.dot(h, wu_ref[...], preferred_element_type=F32)
    o_ref[...] += _bdot(_silu(gt) * up, wo_ref[...])


def _ffn(x, g, w_in, w_out, tm, tf):
    n = x.shape[0]
    n_ff = D_FF // tf
    kern = functools.partial(_ffn_kernel, n_ff=n_ff)
    return pl.pallas_call(
        kern,
        grid=(n // tm, n_ff),
        in_specs=[
            pl.BlockSpec((tm, D_MODEL), lambda i, j: (i, 0)),
            pl.BlockSpec((1, D_MODEL), lambda i, j: (0, 0)),
            pl.BlockSpec((D_MODEL, tf), lambda i, j: (0, j)),
            pl.BlockSpec((D_MODEL, tf), lambda i, j: (0, n_ff + j)),
            pl.BlockSpec((tf, D_MODEL), lambda i, j: (j, 0)),
        ],
        out_specs=pl.BlockSpec((tm, D_MODEL), lambda i, j: (i, 0)),
        out_shape=jax.ShapeDtypeStruct((n, D_MODEL), F32),
        scratch_shapes=[pltpu.VMEM((tm, D_MODEL), BF16)],
        compiler_params=_cparams(("parallel", "arbitrary")),
        name="ffn",
    )(x, g, w_in, w_in, w_out)


def _final_norm_kernel(x_ref, g_ref, o_ref):
    x = x_ref[...]
    ms = jnp.mean(x * x, axis=-1, keepdims=True)
    o_ref[...] = x * lax.rsqrt(ms + RMS_EPS) * g_ref[...]


def _final_norm(x, g, tm):
    n = x.shape[0]
    return pl.pallas_call(
        _final_norm_kernel,
        grid=(n // tm,),
        in_specs=[pl.BlockSpec((tm, D_MODEL), lambda i: (i, 0)),
                  pl.BlockSpec((1, D_MODEL), lambda i: (0, 0))],
        out_specs=pl.BlockSpec((tm, D_MODEL), lambda i: (i, 0)),
        out_shape=jax.ShapeDtypeStruct((n, D_MODEL), F32),
        compiler_params=_cparams(("parallel",)),
        name="final_norm",
    )(x, g)


def _rope_tables(pos):
    half = B_HD // 2
    inv = ROPE_BASE ** (-jnp.arange(half, dtype=F32) / half)
    ang = pos.astype(F32)[:, None] * inv[None, :]
    cos, sin = jnp.cos(ang), jnp.sin(ang)
    return jnp.concatenate([cos, cos], axis=-1), jnp.concatenate([-sin, sin], axis=-1)


def _row_tile(n):
    for t in (512, 256, 128, 64, 32, 16, 8):
        if n % t == 0:
            return t
    raise ValueError(f"row count {n} is not a multiple of 8")


def _layer(x, shift, wkv, ret, hg, cos, sin, p, batch, seq, chunk, valid):
    tm = _row_tile(x.shape[0])
    u = _in_proj(x, p["norm_mix"], p["w_in"], tm, 1280)
    oa, shift_new, wkv_new = _rwkv(u, shift, wkv, p, batch, seq, chunk, valid)
    ob, ret_new = _retention(u, cos, sin, ret, p["ret_gn"], batch, seq, chunk, valid)
    oc, hg_new = _hgrn(u, p["lower"], hg, p["hgrn_gn"], batch, seq, chunk, valid)
    x = _merge(oa, ob, oc, u, x, p["w_branch"], p["w_out"], tm)
    x = _ffn(x, p["norm_ffn"], p["w_ffn_in"], p["w_ffn_out"], tm, 256)
    return x, shift_new, wkv_new, ret_new, hg_new


def kernel(x_prompt, x_sample, state_rwkv_shift, state_rwkv_wkv, state_ret, state_hgrn,
           norm_mix, w_in, rwkv_mu, rwkv_w0, rwkv_w2, rwkv_a0, rwkv_a2, rwkv_g2, rwkv_kk,
           rwkv_ka, rwkv_rk, rwkv_ln_w, rwkv_ln_b, ret_gn, hgrn_lb, hgrn_gn, w_branch, w_out,
           norm_ffn, w_ffn_in, w_ffn_out, norm_final):
    depth = w_in.shape[0]
    bp, tp, _ = x_prompt.shape
    bs, ts, _ = x_sample.shape
    assert tp % PROMPT_CHUNK == 0 and ts <= SAMPLE_CHUNK
    dt = x_prompt.dtype

    lower = _lower_bounds(hgrn_lb)
    w_in_r = jnp.concatenate(
        [w_in[:, :, A_COLS:A_COLS + B_COLS + C_COLS + GATE_COLS], w_in[:, :, :A_COLS]], axis=-1).astype(BF16)
    w_branch_b = w_branch.astype(BF16)
    w_out_b = w_out.astype(BF16)
    w_ffn_in_b = w_ffn_in.astype(BF16)
    w_ffn_out_b = w_ffn_out.astype(BF16)
    hsum = (jnp.arange(BR_W)[:, None] // A_HD == jnp.arange(BR_W)[None, :] // A_HD).astype(F32)
    row = lambda a: a.reshape(1, -1).astype(F32)

    cos_p, sin_p = _rope_tables(jnp.arange(tp, dtype=jnp.int32))
    cos_s, sin_s = _rope_tables(PAST_LEN + jnp.arange(SAMPLE_CHUNK, dtype=jnp.int32))

    xp = x_prompt.reshape(bp * tp, D_MODEL)
    xs = jnp.pad(x_sample, ((0, 0), (0, SAMPLE_CHUNK - ts), (0, 0))).reshape(bs * SAMPLE_CHUNK, D_MODEL)
    z_shift = jnp.zeros((bp, 1, A_COLS), F32)
    z_wkv = jnp.zeros((bp, A_HEADS, A_HD, A_HD), F32)
    z_ret = jnp.zeros((bp, B_HEADS, B_HD, B_HD), F32)
    z_hg = jnp.zeros((bp, C_HEADS, C_HD, C_HD), F32)

    outs_p = [[], [], [], []]
    outs_s = [[], [], [], []]
    for l in range(depth):
        p = {
            "norm_mix": row(norm_mix[l]), "w_in": w_in_r[l], "mu": row(rwkv_mu[l]),
            "w0": row(rwkv_w0[l]), "w2": rwkv_w2[l].astype(BF16), "a0": row(rwkv_a0[l]),
            "a2": rwkv_a2[l].astype(BF16), "g2": rwkv_g2[l].astype(BF16), "kk": row(rwkv_kk[l]),
            "ka": row(rwkv_ka[l]), "rk": row(rwkv_rk[l]), "lnw": row(rwkv_ln_w[l]),
            "lnb": row(rwkv_ln_b[l]), "hsum": hsum, "ret_gn": row(ret_gn[l]),
            "lower": lower[l:l + 1], "hgrn_gn": row(hgrn_gn[l]), "w_branch": w_branch_b[l],
            "w_out": w_out_b[l], "norm_ffn": row(norm_ffn[l]), "w_ffn_in": w_ffn_in_b[l],
            "w_ffn_out": w_ffn_out_b[l],
        }
        xp, a1, a2, a3, a4 = _layer(xp, z_shift, z_wkv, z_ret, z_hg, cos_p, sin_p, p,
                                    bp, tp, PROMPT_CHUNK, PROMPT_CHUNK)
        xs, b1, b2, b3, b4 = _layer(xs, state_rwkv_shift[l].reshape(bs, 1, A_COLS), state_rwkv_wkv[l],
                                    state_ret[l], state_hgrn[l], cos_s, sin_s, p,
                                    bs, SAMPLE_CHUNK, SAMPLE_CHUNK, ts)
        for acc, val in zip(outs_p, (a1.reshape(bp, A_COLS), a2, a3, a4)):
            acc.append(val.astype(dt))
        for acc, val in zip(outs_s, (b1.reshape(bs, A_COLS), b2, b3, b4)):
            acc.append(val.astype(dt))

    g_fin = row(norm_final)
    y_prompt = _final_norm(xp, g_fin, _row_tile(xp.shape[0])).reshape(bp, tp, D_MODEL)
    y_sample = _final_norm(xs, g_fin, _row_tile(xs.shape[0])).reshape(bs, SAMPLE_CHUNK, D_MODEL)[:, :ts]
    return (y_prompt, y_sample, *[jnp.stack(o) for o in outs_p], *[jnp.stack(o) for o in outs_s])
```

```python
import functools
import math

import jax
import jax.numpy as jnp
from jax import lax
from jax.experimental import pallas as pl
from jax.experimental.pallas import tpu as pltpu

F32 = jnp.float32
BF16 = jnp.bfloat16

D_MODEL = 1024
BR_W = 512
A_HD = 64
A_HEADS = BR_W // A_HD
LORA_W = 64
LORA_A = 64
LORA_G = 128
A_COLS = 3 * BR_W + LORA_W + LORA_A + LORA_G
B_HD = 128
B_HEADS = BR_W // B_HD
C_HD = 128
C_HEADS = BR_W // C_HD
B_COLS = 4 * BR_W
C_COLS = 4 * BR_W
GATE_COLS = 3 * D_MODEL
IN_COLS = A_COLS + B_COLS + C_COLS + GATE_COLS
D_FF = 2816
PAST_LEN = 16384
RMS_EPS = 1e-6
RWKV_GN_EPS = 64e-5
GN_EPS = 1e-5
F_FLOOR = 1e-30
ROPE_BASE = 10000.0

U_B_OFF = 0
U_C_OFF = B_COLS
U_G_OFF = B_COLS + C_COLS
U_A_OFF = B_COLS + C_COLS + GATE_COLS

PROMPT_CHUNK = 64
SAMPLE_CHUNK = 8
PROMPT_SEQS = 1
SAMPLE_SEQS = 4
SUB = 8

VMEM_LIMIT = 56 * 1024 * 1024

NN = (((1,), (0,)), ((), ()))
NT = (((1,), (1,)), ((), ()))
TN = (((0,), (0,)), ((), ()))


def _bdot(a, b, dims=NN):
    return lax.dot_general(a.astype(BF16), b.astype(BF16), dims, preferred_element_type=F32)


def _sel_dot(sel, x, passes, sel_first=True):
    sel = sel.astype(BF16)
    acc = None
    for _ in range(passes):
        piece = x.astype(BF16)
        ops = (sel, piece) if sel_first else (piece, sel)
        part = lax.dot_general(*ops, NN, preferred_element_type=F32)
        acc = part if acc is None else acc + part
        x = x - piece.astype(F32)
    return acc


def _sigmoid(x):
    return 1.0 / (1.0 + jnp.exp(-x))


def _silu(x):
    return x * _sigmoid(x)


def _cparams(sem):
    return pltpu.CompilerParams(dimension_semantics=sem, vmem_limit_bytes=VMEM_LIMIT)


def _chunk_cumsum(x, chunk):
    rows = x.shape[0]
    tr = lax.broadcasted_iota(jnp.int32, (rows, rows), 0)
    tc = lax.broadcasted_iota(jnp.int32, (rows, rows), 1)
    sh = chunk.bit_length() - 1
    tri = (tr >= tc) & ((tr >> sh) == (tc >> sh))
    return _sel_dot(jnp.where(tri, 1.0, 0.0), x, 3)


def _last_rows(x, chunk, bb):
    return jnp.concatenate(
        [jnp.broadcast_to(x[(i + 1) * chunk - 1:(i + 1) * chunk, :], (chunk, x.shape[1])) for i in range(bb)],
        axis=0)


def _lower_kernel(lb_ref, o_ref):
    x = lb_ref[...]
    depth = x.shape[0]
    m = x[0:1]
    for l in range(1, depth):
        m = jnp.maximum(m, x[l:l + 1])
    e = jnp.exp(x - m)
    tot = e[0:1]
    for l in range(1, depth):
        tot = tot + e[l:l + 1]
    sm = e / tot
    acc = jnp.zeros_like(m)
    for l in range(depth):
        acc = acc + sm[l:l + 1]
        o_ref[l:l + 1, :] = acc - sm[0:1]


def _lower_bounds(hgrn_lb):
    return pl.pallas_call(
        _lower_kernel, out_shape=jax.ShapeDtypeStruct(hgrn_lb.shape, F32), name="hgrn_lower",
    )(hgrn_lb.astype(F32))


def _in_proj_kernel(x_ref, g_ref, w_ref, o_ref, h_ref):
    @pl.when(pl.program_id(1) == 0)
    def _():
        x = x_ref[...]
        ms = jnp.mean(x * x, axis=-1, keepdims=True)
        h_ref[...] = (x * lax.rsqrt(ms + RMS_EPS) * g_ref[...]).astype(BF16)

    o_ref[...] = jnp.dot(h_ref[...], w_ref[...], preferred_element_type=F32)


def _in_proj(x, g, w, tm, tn):
    n, d = x.shape
    cols = w.shape[1]
    return pl.pallas_call(
        _in_proj_kernel,
        grid=(n // tm, cols // tn),
        in_specs=[
            pl.BlockSpec((tm, d), lambda i, j: (i, 0)),
            pl.BlockSpec((1, d), lambda i, j: (0, 0)),
            pl.BlockSpec((d, tn), lambda i, j: (0, j)),
        ],
        out_specs=pl.BlockSpec((tm, tn), lambda i, j: (i, j)),
        out_shape=jax.ShapeDtypeStruct((n, cols), F32),
        scratch_shapes=[pltpu.VMEM((tm, d), BF16)],
        compiler_params=_cparams(("parallel", "arbitrary")),
        name="in_proj",
    )(x, g, w)


def _rwkv_kernel(u_ref, shift_ref, s0_ref, mu_ref, w0_ref, w2_ref, a0_ref, a2_ref, g2_ref,
                 kk_ref, ka_ref, rk_ref, lnw_ref, lnb_ref, hsum_ref,
                 o_ref, shift_out_ref, s_out_ref, st_scr, prev_scr, *, chunk, valid, n_chunks, bb):
    n = pl.program_id(1)
    rows = bb * chunk
    units = [(i, h) for i in range(bb) for h in range(A_HEADS)]

    @pl.when(n == 0)
    def _():
        for i, h in units:
            st_scr[i, h] = s0_ref[i, h].T
        prev_scr[...] = shift_ref[...]

    u = u_ref[...].reshape(rows, A_COLS)
    row = lax.broadcasted_iota(jnp.int32, (rows, 1), 0)
    pos = row & (chunk - 1)
    prev = jnp.concatenate([jnp.broadcast_to(prev_scr[i], (chunk, A_COLS)) for i in range(bb)], axis=0)
    shifted = jnp.where(pos == 0, prev, pltpu.roll(u, 1, axis=0))
    xm = u + mu_ref[...] * (shifted - u)
    for i in range(bb):
        prev_scr[i] = u[(i + 1) * chunk - 1:(i + 1) * chunk, :]

    r = xm[:, 0:BR_W]
    k = xm[:, BR_W:2 * BR_W]
    v = xm[:, 2 * BR_W:3 * BR_W]
    wl = xm[:, 3 * BR_W:3 * BR_W + LORA_W]
    al = xm[:, 3 * BR_W + LORA_W:3 * BR_W + LORA_W + LORA_A]
    gl = xm[:, 3 * BR_W + LORA_W + LORA_A:A_COLS]

    z = w0_ref[...] + _bdot(jnp.tanh(wl), w2_ref[...])
    ld = -math.exp(-0.5) * _sigmoid(z)
    a = _sigmoid(a0_ref[...] + _bdot(al, a2_ref[...]))
    g = _bdot(_sigmoid(gl), g2_ref[...])
    kk = k * kk_ref[...]
    ss = _sel_dot(hsum_ref[...], kk * kk, 2, sel_first=False)
    kk = kk / jnp.maximum(jnp.sqrt(ss), 1e-12)
    k = k * (1.0 + (a - 1.0) * ka_ref[...])
    if valid < chunk:
        ok = pos < valid
        ld = jnp.where(ok, ld, 0.0)
        kk = jnp.where(ok, kk, 0.0)
        k = jnp.where(ok, k, 0.0)
    bv = kk * a

    cl = _chunk_cumsum(ld, chunk)
    cl_last = _last_rows(cl, chunk, bb)
    e_ncl = jnp.exp(-cl)
    e_rem = jnp.exp(cl_last - cl)
    alpha_t = -kk * jnp.exp(cl - ld)
    r_t = r * jnp.exp(cl)
    b_bar = bv * e_ncl
    k_bar = k * e_ncl
    b_hat = bv * e_rem
    k_hat = k * e_rem
    e_last = jnp.exp(cl_last)
    rkk = r * k * rk_ref[...]

    qr = lax.broadcasted_iota(jnp.int32, (2 * chunk, 2 * chunk), 0)
    qc = lax.broadcasted_iota(jnp.int32, (2 * chunk, 2 * chunk), 1)
    qt, qs = qr & (chunk - 1), qc & (chunk - 1)
    keep = (qt > qs) | ((qr >= chunk) & (qt == qs))

    def blk(t, i, h):
        return t[i * chunk:(i + 1) * chunk, h * A_HD:(h + 1) * A_HD]

    aa, st0, vv, x, p = {}, {}, {}, {}, {}
    for i, h in units:
        lhs = jnp.concatenate([blk(alpha_t, i, h), blk(r_t, i, h)], axis=0)
        rhs = jnp.concatenate([blk(b_bar, i, h), blk(k_bar, i, h)], axis=0)
        aa[i, h] = jnp.where(keep, _bdot(lhs, rhs, NT), 0.0)
        st0[i, h] = st_scr[i, h]
        vv[i, h] = blk(v, i, h)
    for i, h in units:
        lhs = jnp.concatenate([blk(alpha_t, i, h), aa[i, h][0:chunk, chunk:2 * chunk]], axis=1)
        x[i, h] = _bdot(lhs, jnp.concatenate([st0[i, h], vv[i, h]], axis=0))
        p[i, h] = aa[i, h][0:chunk, 0:chunk]
    for i, h in units:
        x[i, h] = x[i, h] + _bdot(p[i, h], x[i, h])
    span = 2
    while span < chunk:
        for i, h in units:
            p[i, h] = _bdot(p[i, h], p[i, h])
        for i, h in units:
            x[i, h] = x[i, h] + _bdot(p[i, h], x[i, h])
        span *= 2
    for i, h in units:
        sl = slice(h * A_HD, (h + 1) * A_HD)
        lhs = jnp.concatenate([blk(r_t, i, h), aa[i, h][chunk:2 * chunk, :]], axis=1)
        o = _bdot(lhs, jnp.concatenate([st0[i, h], x[i, h], vv[i, h]], axis=0))
        upd = _bdot(jnp.concatenate([blk(b_hat, i, h), blk(k_hat, i, h)], axis=0),
                    jnp.concatenate([x[i, h], vv[i, h]], axis=0), TN)
        dec = jnp.broadcast_to(e_last[i * chunk:i * chunk + 1, sl], (A_HD, A_HD)).T
        st_scr[i, h] = st0[i, h] * dec + upd
        mu = jnp.mean(o, axis=-1, keepdims=True)
        var = jnp.mean(jnp.square(o - mu), axis=-1, keepdims=True)
        o = (o - mu) * lax.rsqrt(var + RWKV_GN_EPS) * lnw_ref[:, sl] + lnb_ref[:, sl]
        o = o + jnp.sum(blk(rkk, i, h), axis=-1, keepdims=True) * vv[i, h]
        o_ref[i, :, sl] = o * blk(g, i, h)

    @pl.when(n == n_chunks - 1)
    def _():
        last = (valid - 1) if n_chunks == 1 else (chunk - 1)
        for i in range(bb):
            shift_out_ref[i] = u[i * chunk + last:i * chunk + last + 1, :]
        for i, h in units:
            s_out_ref[i, h] = st_scr[i, h].T


def _rwkv(u, shift, wkv, layer, p, chunk, valid, bb):
    batch, seq, _ = u.shape
    n_chunks = seq // chunk
    assert valid == chunk or n_chunks == 1
    a_blk = U_A_OFF // A_COLS
    vec = lambda w: pl.BlockSpec((1, w), lambda b, n: (0, 0))
    full = lambda s: pl.BlockSpec(s, lambda b, n: (0,) * len(s))
    kern = functools.partial(_rwkv_kernel, chunk=chunk, valid=valid, n_chunks=n_chunks, bb=bb)
    return pl.pallas_call(
        kern,
        grid=(batch // bb, n_chunks),
        in_specs=[
            pl.BlockSpec((bb, chunk, A_COLS), lambda b, n: (b, n, a_blk)),
            pl.BlockSpec((None, bb, 1, A_COLS), lambda b, n: (layer, b, 0, 0)),
            pl.BlockSpec((None, bb, A_HEADS, A_HD, A_HD), lambda b, n: (layer, b, 0, 0, 0)),
            vec(A_COLS), vec(BR_W), full((LORA_W, BR_W)), vec(BR_W), full((LORA_A, BR_W)),
            full((LORA_G, BR_W)), vec(BR_W), vec(BR_W), vec(BR_W), vec(BR_W), vec(BR_W),
            full((BR_W, BR_W)),
        ],
        out_specs=[
            pl.BlockSpec((bb, chunk, BR_W), lambda b, n: (b, n, 0)),
            pl.BlockSpec((bb, 1, A_COLS), lambda b, n: (b, 0, 0)),
            pl.BlockSpec((bb, A_HEADS, A_HD, A_HD), lambda b, n: (b, 0, 0, 0)),
        ],
        out_shape=[
            jax.ShapeDtypeStruct((batch, seq, BR_W), F32),
            jax.ShapeDtypeStruct((batch, 1, A_COLS), F32),
            jax.ShapeDtypeStruct((batch, A_HEADS, A_HD, A_HD), F32),
        ],
        scratch_shapes=[pltpu.VMEM((bb, A_HEADS, A_HD, A_HD), F32), pltpu.VMEM((bb, 1, A_COLS), F32)],
        compiler_params=_cparams(("parallel", "arbitrary")),
        name="rwkv7",
    )(u, shift, wkv, p["mu"], p["w0"], p["w2"], p["a0"], p["a2"], p["g2"], p["kk"], p["ka"],
      p["rk"], p["lnw"], p["lnb"], p["hsum"])


def _ret_kernel(u_ref, cos_ref, sin_ref, s0_ref, gn_ref, o_ref, s_out_ref, s_scr,
                *, chunk, valid, n_chunks, bb):
    n = pl.program_id(1)
    units = [(i, h) for i in range(bb) for h in range(B_HEADS)]

    @pl.when(n == 0)
    def _():
        s_scr[...] = s0_ref[...]

    cos = cos_ref[...]
    sin = sin_ref[...]
    tr = lax.broadcasted_iota(jnp.int32, (chunk, chunk), 0)
    tc = lax.broadcasted_iota(jnp.int32, (chunk, chunk), 1)
    diff = (tr - tc).astype(F32)
    causal = tr >= tc
    pos = lax.broadcasted_iota(jnp.int32, (chunk, 1), 0)
    posf = pos.astype(F32)
    lgs = [math.log(1.0 - 2.0 ** (-5.0 - h)) for h in range(B_HEADS)]
    intra = [jnp.where(causal, jnp.exp(lg * jnp.maximum(diff, 0.0)), 0.0) for lg in lgs]
    q_dec = [jnp.exp(lg * (posf + 1.0)) for lg in lgs]
    k_dec = [jnp.where(pos < valid, jnp.exp(lg * (valid - 1.0 - posf)), 0.0) for lg in lgs]
    s_dec = [math.exp(lg * valid) for lg in lgs]

    q, k, v, s0, sc = {}, {}, {}, {}, {}
    for i, h in units:
        ui = u_ref[i]
        qh = ui[:, h * B_HD:(h + 1) * B_HD]
        kh = ui[:, BR_W + h * B_HD:BR_W + (h + 1) * B_HD]
        q[i, h] = qh * cos + pltpu.roll(qh, B_HD // 2, axis=1) * sin
        k[i, h] = (kh * cos + pltpu.roll(kh, B_HD // 2, axis=1) * sin) * (B_HD ** -0.5)
        v[i, h] = ui[:, 2 * BR_W + h * B_HD:2 * BR_W + (h + 1) * B_HD]
        s0[i, h] = s_scr[i, h]
    for i, h in units:
        sc[i, h] = _bdot(q[i, h], k[i, h], NT) * intra[h]
    for i, h in units:
        sl = slice(h * B_HD, (h + 1) * B_HD)
        o = _bdot(sc[i, h], v[i, h]) + _bdot(q[i, h] * q_dec[h], s0[i, h])
        s_scr[i, h] = s0[i, h] * s_dec[h] + _bdot(k[i, h] * k_dec[h], v[i, h], TN)
        mu = jnp.mean(o, axis=-1, keepdims=True)
        var = jnp.mean(jnp.square(o - mu), axis=-1, keepdims=True)
        o = (o - mu) * lax.rsqrt(var + GN_EPS)
        gt = u_ref[i, :, 3 * BR_W + h * B_HD:3 * BR_W + (h + 1) * B_HD]
        o_ref[i, :, sl] = o * gn_ref[:, sl] * _silu(gt)

    @pl.when(n == n_chunks - 1)
    def _():
        s_out_ref[...] = s_scr[...]


def _retention(u, cos, sin, state, layer, gn, chunk, valid, bb):
    batch, seq, _ = u.shape
    n_chunks = seq // chunk
    assert valid == chunk or n_chunks == 1
    kern = functools.partial(_ret_kernel, chunk=chunk, valid=valid, n_chunks=n_chunks, bb=bb)
    return pl.pallas_call(
        kern,
        grid=(batch // bb, n_chunks),
        in_specs=[
            pl.BlockSpec((bb, chunk, B_COLS), lambda b, n: (b, n, U_B_OFF // B_COLS)),
            pl.BlockSpec((chunk, B_HD), lambda b, n: (n, 0)),
            pl.BlockSpec((chunk, B_HD), lambda b, n: (n, 0)),
            pl.BlockSpec((None, bb, B_HEADS, B_HD, B_HD), lambda b, n: (layer, b, 0, 0, 0)),
            pl.BlockSpec((1, BR_W), lambda b, n: (0, 0)),
        ],
        out_specs=[
            pl.BlockSpec((bb, chunk, BR_W), lambda b, n: (b, n, 0)),
            pl.BlockSpec((bb, B_HEADS, B_HD, B_HD), lambda b, n: (b, 0, 0, 0)),
        ],
        out_shape=[
            jax.ShapeDtypeStruct((batch, seq, BR_W), F32),
            jax.ShapeDtypeStruct((batch, B_HEADS, B_HD, B_HD), F32),
        ],
        scratch_shapes=[pltpu.VMEM((bb, B_HEADS, B_HD, B_HD), F32)],
        compiler_params=_cparams(("parallel", "arbitrary")),
        name="retention",
    )(u, cos, sin, state, gn)


def _hgrn_kernel(u_ref, lb_ref, s0_ref, gn_ref, o_ref, s_out_ref, st_scr,
                 *, chunk, valid, n_chunks, bb):
    n = pl.program_id(1)
    rows = bb * chunk
    units = [(i, h) for i in range(bb) for h in range(C_HEADS)]

    @pl.when(n == 0)
    def _():
        for i, h in units:
            st_scr[i, h] = s0_ref[i, h].T

    lb = lb_ref[...]
    row = lax.broadcasted_iota(jnp.int32, (rows, 1), 0)
    sig_f = _sigmoid(u_ref[:, :, BR_W:2 * BR_W].reshape(rows, BR_W))
    lf = jnp.log(jnp.maximum(lb + (1.0 - lb) * sig_f, F_FLOOR))
    kc = (1.0 - lb) * (1.0 - sig_f)
    if valid < chunk:
        ok = (row & (chunk - 1)) < valid
        lf = jnp.where(ok, lf, 0.0)
        kc = jnp.where(ok, kc, 0.0)
    bc = _chunk_cumsum(lf, chunk)
    b_last = _last_rows(bc, chunk, bb)
    e_b = jnp.exp(bc)
    e_last = jnp.exp(b_last)
    k_hat = kc * jnp.exp(b_last - bc)
    n_sub = chunk // SUB
    tr = lax.broadcasted_iota(jnp.int32, (chunk, chunk), 0)
    tc = lax.broadcasted_iota(jnp.int32, (chunk, chunk), 1)
    lane = lax.broadcasted_iota(jnp.int32, (SUB, chunk), 1)
    t_sub = lax.broadcasted_iota(jnp.int32, (SUB, 1), 0)

    def blk(t, i, h):
        return t[i * chunk:(i + 1) * chunk, h * C_HD:(h + 1) * C_HD]

    q, k, v, b, st, scores = {}, {}, {}, {}, {}, {}
    for i, h in units:
        q[i, h] = u_ref[i, :, h * C_HD:(h + 1) * C_HD]
        v[i, h] = u_ref[i, :, 2 * BR_W + h * C_HD:2 * BR_W + (h + 1) * C_HD]
        k[i, h] = blk(kc, i, h)
        b[i, h] = blk(bc, i, h)
        st[i, h] = st_scr[i, h]
        scores[i, h] = jnp.zeros((chunk, chunk), F32)

    size = SUB
    while size < chunk:
        sh = (2 * size).bit_length() - 1
        same_pair = (tr >> sh) == (tc >> sh)
        for i, h in units:
            qs, ks = [], []
            for j in range(chunk // size):
                sel = slice(j * size, (j + 1) * size)
                if j % 2 == 1:
                    ref = b[i, h][j * size - 1:j * size, :]
                    qs.append(q[i, h][sel] * jnp.exp(b[i, h][sel] - ref))
                    ks.append(jnp.zeros((size, C_HD), F32))
                else:
                    ref = b[i, h][(j + 1) * size - 1:(j + 1) * size, :]
                    qs.append(jnp.zeros((size, C_HD), F32))
                    ks.append(k[i, h][sel] * jnp.exp(ref - b[i, h][sel]))
            lvl = _bdot(jnp.concatenate(qs, axis=0), jnp.concatenate(ks, axis=0), NT)
            scores[i, h] = scores[i, h] + jnp.where(same_pair, lvl, 0.0)
        size *= 2

    for i, h in units:
        diag_rows = []
        for j in range(n_sub):
            sel = slice(j * SUB, (j + 1) * SUB)
            qb, kb, bb_ = q[i, h][sel], k[i, h][sel], b[i, h][sel]
            d = jnp.zeros((SUB, chunk), F32)
            for s in range(SUB):
                w = jnp.exp(jnp.minimum(bb_ - bb_[s:s + 1, :], 0.0))
                col = jnp.sum(qb * kb[s:s + 1, :] * w, axis=-1, keepdims=True)
                d = d + jnp.where((lane == j * SUB + s) & (t_sub >= s), col, 0.0)
            diag_rows.append(d)
        scores[i, h] = scores[i, h] + jnp.concatenate(diag_rows, axis=0)

    for i, h in units:
        sl = slice(h * C_HD, (h + 1) * C_HD)
        o = _bdot(scores[i, h], v[i, h]) + _bdot(q[i, h] * blk(e_b, i, h), st[i, h], NT)
        st_scr[i, h] = st[i, h] * e_last[i * chunk:i * chunk + 1, sl] + _bdot(v[i, h], blk(k_hat, i, h), TN)
        o = o * lax.rsqrt(jnp.mean(o * o, axis=-1, keepdims=True) + GN_EPS)
        gt = u_ref[i, :, 3 * BR_W + h * C_HD:3 * BR_W + (h + 1) * C_HD]
        o_ref[i, :, sl] = o * gn_ref[:, sl] * _silu(gt)

    @pl.when(n == n_chunks - 1)
    def _():
        for i, h in units:
            s_out_ref[i, h] = st_scr[i, h].T


def _hgrn(u, lb, state, layer, gn, chunk, valid, bb):
    batch, seq, _ = u.shape
    n_chunks = seq // chunk
    assert valid == chunk or n_chunks == 1
    kern = functools.partial(_hgrn_kernel, chunk=chunk, valid=valid, n_chunks=n_chunks, bb=bb)
    return pl.pallas_call(
        kern,
        grid=(batch // bb, n_chunks),
        in_specs=[
            pl.BlockSpec((bb, chunk, C_COLS), lambda b, n: (b, n, U_C_OFF // C_COLS)),
            pl.BlockSpec((1, BR_W), lambda b, n: (0, 0)),
            pl.BlockSpec((None, bb, C_HEADS, C_HD, C_HD), lambda b, n: (layer, b, 0, 0, 0)),
            pl.BlockSpec((1, BR_W), lambda b, n: (0, 0)),
        ],
        out_specs=[
            pl.BlockSpec((bb, chunk, BR_W), lambda b, n: (b, n, 0)),
            pl.BlockSpec((bb, C_HEADS, C_HD, C_HD), lambda b, n: (b, 0, 0, 0)),
        ],
        out_shape=[
            jax.ShapeDtypeStruct((batch, seq, BR_W), F32),
            jax.ShapeDtypeStruct((batch, C_HEADS, C_HD, C_HD), F32),
        ],
        scratch_shapes=[pltpu.VMEM((bb, C_HEADS, C_HD, C_HD), F32)],
        compiler_params=_cparams(("parallel", "arbitrary")),
        name="hgrn2",
    )(u, lb, state, gn)


def _merge_kernel(oa_ref, ob_ref, oc_ref, g0_ref, g1_ref, g2_ref, x_ref, wb_ref, wo_ref, o_ref):
    m = _sigmoid(g0_ref[...]) * _bdot(oa_ref[...], wb_ref[0])
    m = m + _sigmoid(g1_ref[...]) * _bdot(ob_ref[...], wb_ref[1])
    m = m + _sigmoid(g2_ref[...]) * _bdot(oc_ref[...], wb_ref[2])
    o_ref[...] = x_ref[...] + _bdot(m, wo_ref[...])


def _merge(oa, ob, oc, u, x, wb, wo, tm):
    n = x.shape[0]
    g_blk = U_G_OFF // D_MODEL
    br = pl.BlockSpec((tm, BR_W), lambda i: (i, 0))
    gate = lambda c: pl.BlockSpec((tm, D_MODEL), lambda i: (i, g_blk + c))
    return pl.pallas_call(
        _merge_kernel,
        grid=(n // tm,),
        in_specs=[br, br, br, gate(0), gate(1), gate(2),
                  pl.BlockSpec((tm, D_MODEL), lambda i: (i, 0)),
                  pl.BlockSpec((3, BR_W, D_MODEL), lambda i: (0, 0, 0)),
                  pl.BlockSpec((D_MODEL, D_MODEL), lambda i: (0, 0))],
        out_specs=pl.BlockSpec((tm, D_MODEL), lambda i: (i, 0)),
        out_shape=jax.ShapeDtypeStruct((n, D_MODEL), F32),
        compiler_params=_cparams(("parallel",)),
        name="merge",
    )(oa, ob, oc, u, u, u, x, wb, wo)


def _ffn_kernel(x_ref, g_ref, wg_ref, wu_ref, wo_ref, o_ref, h_ref):
    @pl.when(pl.program_id(1) == 0)
    def _():
        x = x_ref[...]
        ms = jnp.mean(x * x, axis=-1, keepdims=True)
        h_ref[...] = (x * lax.rsqrt(ms + RMS_EPS) * g_ref[...]).astype(BF16)
        o_ref[...] = x

    h = h_ref[...]
    gt = jnp.dot(h, wg_ref[...], preferred_element_type=F32)
    up = jnp.dot(h, wu_ref[...], preferred_element_type=F32)
    o_ref[...] += _bdot(_silu(gt) * up, wo_ref[...])


def _ffn(x, g, w_in, w_out, tm, tf):
    n = x.shape[0]
    n_ff = D_FF // tf
    return pl.pallas_call(
        _ffn_kernel,
        grid=(n // tm, n_ff),
        in_specs=[
            pl.BlockSpec((tm, D_MODEL), lambda i, j: (i, 0)),
            pl.BlockSpec((1, D_MODEL), lambda i, j: (0, 0)),
            pl.BlockSpec((D_MODEL, tf), lambda i, j: (0, j)),
            pl.BlockSpec((D_MODEL, tf), lambda i, j: (0, n_ff + j)),
            pl.BlockSpec((tf, D_MODEL), lambda i, j: (j, 0)),
        ],
        out_specs=pl.BlockSpec((tm, D_MODEL), lambda i, j: (i, 0)),
        out_shape=jax.ShapeDtypeStruct((n, D_MODEL), F32),
        scratch_shapes=[pltpu.VMEM((tm, D_MODEL), BF16)],
        compiler_params=_cparams(("parallel", "arbitrary")),
        name="ffn",
    )(x, g, w_in, w_in, w_out)


def _final_norm_kernel(x_ref, g_ref, o_ref):
    x = x_ref[...]
    ms = jnp.mean(x * x, axis=-1, keepdims=True)
    o_ref[...] = x * lax.rsqrt(ms + RMS_EPS) * g_ref[...]


def _final_norm(x, g, tm):
    n = x.shape[0]
    return pl.pallas_call(
        _final_norm_kernel,
        grid=(n // tm,),
        in_specs=[pl.BlockSpec((tm, D_MODEL), lambda i: (i, 0)),
                  pl.BlockSpec((1, D_MODEL), lambda i: (0, 0))],
        out_specs=pl.BlockSpec((tm, D_MODEL), lambda i: (i, 0)),
        out_shape=jax.ShapeDtypeStruct((n, D_MODEL), F32),
        compiler_params=_cparams(("parallel",)),
        name="final_norm",
    )(x, g)


def _rope_tables(pos):
    half = B_HD // 2
    inv = ROPE_BASE ** (-jnp.arange(half, dtype=F32) / half)
    ang = pos.astype(F32)[:, None] * inv[None, :]
    cos, sin = jnp.cos(ang), jnp.sin(ang)
    return jnp.concatenate([cos, cos], axis=-1), jnp.concatenate([-sin, sin], axis=-1)


def _row_tile(n):
    for t in (512, 256, 128, 64, 32, 16, 8):
        if n % t == 0:
            return t
    raise ValueError(f"row count {n} is not a multiple of 8")


def _layer(x, states, layer, cos, sin, p, batch, seq, chunk, valid, bb):
    shift, wkv, ret, hg = states
    tm = _row_tile(x.shape[0])
    u2 = _in_proj(x, p["norm_mix"], p["w_in"], tm, 1280)
    u = u2.reshape(batch, seq, IN_COLS)
    oa, shift_new, wkv_new = _rwkv(u, shift, wkv, layer, p, chunk, valid, bb)
    ob, ret_new = _retention(u, cos, sin, ret, layer, p["ret_gn"], chunk, valid, bb)
    oc, hg_new = _hgrn(u, p["lower"], hg, layer, p["hgrn_gn"], chunk, valid, bb)
    flat = lambda o: o.reshape(batch * seq, BR_W)
    x = _merge(flat(oa), flat(ob), flat(oc), u2, x, p["w_branch"], p["w_out"], tm)
    x = _ffn(x, p["norm_ffn"], p["w_ffn_in"], p["w_ffn_out"], tm, 256)
    return x, shift_new, wkv_new, ret_new, hg_new


def kernel(x_prompt, x_sample, state_rwkv_shift, state_rwkv_wkv, state_ret, state_hgrn,
           norm_mix, w_in, rwkv_mu, rwkv_w0, rwkv_w2, rwkv_a0, rwkv_a2, rwkv_g2, rwkv_kk,
           rwkv_ka, rwkv_rk, rwkv_ln_w, rwkv_ln_b, ret_gn, hgrn_lb, hgrn_gn, w_branch, w_out,
           norm_ffn, w_ffn_in, w_ffn_out, norm_final):
    depth = w_in.shape[0]
    bp, tp, _ = x_prompt.shape
    bs, ts, _ = x_sample.shape
    assert tp % PROMPT_CHUNK == 0 and ts <= SAMPLE_CHUNK
    assert bp % PROMPT_SEQS == 0 and bs % SAMPLE_SEQS == 0
    dt = x_prompt.dtype

    lower = _lower_bounds(hgrn_lb)
    w_in_r = jnp.concatenate(
        [w_in[:, :, A_COLS:A_COLS + B_COLS + C_COLS + GATE_COLS], w_in[:, :, :A_COLS]], axis=-1).astype(BF16)
    w_branch_b = w_branch.astype(BF16)
    w_out_b = w_out.astype(BF16)
    w_ffn_in_b = w_ffn_in.astype(BF16)
    w_ffn_out_b = w_ffn_out.astype(BF16)
    hsum = (jnp.arange(BR_W)[:, None] // A_HD == jnp.arange(BR_W)[None, :] // A_HD).astype(BF16)
    row = lambda a: a.reshape(1, -1).astype(F32)

    cos_p, sin_p = _rope_tables(jnp.arange(tp, dtype=jnp.int32))
    cos_s, sin_s = _rope_tables(PAST_LEN + jnp.arange(SAMPLE_CHUNK, dtype=jnp.int32))

    xp = x_prompt.reshape(bp * tp, D_MODEL)
    xs = jnp.pad(x_sample, ((0, 0), (0, SAMPLE_CHUNK - ts), (0, 0))).reshape(bs * SAMPLE_CHUNK, D_MODEL)
    zero_states = (jnp.zeros((1, bp, 1, A_COLS), F32), jnp.zeros((1, bp, A_HEADS, A_HD, A_HD), F32),
                   jnp.zeros((1, bp, B_HEADS, B_HD, B_HD), F32), jnp.zeros((1, bp, C_HEADS, C_HD, C_HD), F32))
    sample_states = (state_rwkv_shift.reshape(depth, bs, 1, A_COLS), state_rwkv_wkv, state_ret, state_hgrn)

    outs_p = [[], [], [], []]
    outs_s = [[], [], [], []]
    for l in range(depth):
        p = {
            "norm_mix": row(norm_mix[l]), "w_in": w_in_r[l], "mu": row(rwkv_mu[l]),
            "w0": row(rwkv_w0[l]), "w2": rwkv_w2[l].astype(BF16), "a0": row(rwkv_a0[l]),
            "a2": rwkv_a2[l].astype(BF16), "g2": rwkv_g2[l].astype(BF16), "kk": row(rwkv_kk[l]),
            "ka": row(rwkv_ka[l]), "rk": row(rwkv_rk[l]), "lnw": row(rwkv_ln_w[l]),
            "lnb": row(rwkv_ln_b[l]), "hsum": hsum, "ret_gn": row(ret_gn[l]),
            "lower": lower[l:l + 1], "hgrn_gn": row(hgrn_gn[l]), "w_branch": w_branch_b[l],
            "w_out": w_out_b[l], "norm_ffn": row(norm_ffn[l]), "w_ffn_in": w_ffn_in_b[l],
            "w_ffn_out": w_ffn_out_b[l],
        }
        xp, a1, a2, a3, a4 = _layer(xp, zero_states, 0, cos_p, sin_p, p, bp, tp,
                                    PROMPT_CHUNK, PROMPT_CHUNK, PROMPT_SEQS)
        xs, b1, b2, b3, b4 = _layer(xs, sample_states, l, cos_s, sin_s, p, bs, SAMPLE_CHUNK,
                                    SAMPLE_CHUNK, ts, SAMPLE_SEQS)
        for acc, val in zip(outs_p, (a1.reshape(bp, A_COLS), a2, a3, a4)):
            acc.append(val.astype(dt))
        for acc, val in zip(outs_s, (b1.reshape(bs, A_COLS), b2, b3, b4)):
            acc.append(val.astype(dt))

    g_fin = row(norm_final)
    y_prompt = _final_norm(xp, g_fin, _row_tile(xp.shape[0])).reshape(bp, tp, D_MODEL)
    y_sample = _final_norm(xs, g_fin, _row_tile(xs.shape[0])).reshape(bs, SAMPLE_CHUNK, D_MODEL)[:, :ts]
    return (y_prompt, y_sample, *[jnp.stack(o) for o in outs_p], *[jnp.stack(o) for o in outs_s])
```

```python
import functools
import math

import jax
import jax.numpy as jnp
from jax import lax
from jax.experimental import pallas as pl
from jax.experimental.pallas import tpu as pltpu

F32 = jnp.float32
BF16 = jnp.bfloat16

D_MODEL = 1024
BR_W = 512
A_HD = 64
A_HEADS = BR_W // A_HD
LORA_W = 64
LORA_A = 64
LORA_G = 128
A_COLS = 3 * BR_W + LORA_W + LORA_A + LORA_G
B_HD = 128
B_HEADS = BR_W // B_HD
C_HD = 128
C_HEADS = BR_W // C_HD
B_COLS = 4 * BR_W
C_COLS = 4 * BR_W
GATE_COLS = 3 * D_MODEL
IN_COLS = A_COLS + B_COLS + C_COLS + GATE_COLS
D_FF = 2816
PAST_LEN = 16384
RMS_EPS = 1e-6
RWKV_GN_EPS = 64e-5
GN_EPS = 1e-5
F_FLOOR = 1e-30
ROPE_BASE = 10000.0

U_B_OFF = 0
U_C_OFF = B_COLS
U_G_OFF = B_COLS + C_COLS
U_A_OFF = B_COLS + C_COLS + GATE_COLS

SAMPLE_CHUNK = 8
PROMPT_CFG = {"rwkv": (64, 2), "ret": (128, 2), "hgrn": (64, 2)}
SAMPLE_CFG = {"rwkv": (SAMPLE_CHUNK, 4), "ret": (SAMPLE_CHUNK, 8), "hgrn": (SAMPLE_CHUNK, 8)}
SUB = 8
IN_ROW_TILE = 1024
IN_COL_TILE = 1280
FF_TILE = 1408

VMEM_LIMIT = 56 * 1024 * 1024

NN = (((1,), (0,)), ((), ()))
NT = (((1,), (1,)), ((), ()))
TN = (((0,), (0,)), ((), ()))


def _bdot(a, b, dims=NN):
    return lax.dot_general(a.astype(BF16), b.astype(BF16), dims, preferred_element_type=F32)


def _sel_dot(sel, x, passes, sel_first=True):
    sel = sel.astype(BF16)
    acc = None
    for _ in range(passes):
        piece = x.astype(BF16)
        ops = (sel, piece) if sel_first else (piece, sel)
        part = lax.dot_general(*ops, NN, preferred_element_type=F32)
        acc = part if acc is None else acc + part
        x = x - piece.astype(F32)
    return acc


def _sigmoid(x):
    return 1.0 / (1.0 + jnp.exp(-x))


def _silu(x):
    return x * _sigmoid(x)


def _cparams(sem):
    return pltpu.CompilerParams(dimension_semantics=sem, vmem_limit_bytes=VMEM_LIMIT)


def _chunk_cumsum(x, chunk):
    rows = x.shape[0]
    tr = lax.broadcasted_iota(jnp.int32, (rows, rows), 0)
    tc = lax.broadcasted_iota(jnp.int32, (rows, rows), 1)
    sh = chunk.bit_length() - 1
    tri = (tr >= tc) & ((tr >> sh) == (tc >> sh))
    return _sel_dot(jnp.where(tri, 1.0, 0.0), x, 3)


def _last_rows(x, chunk, bb):
    return jnp.concatenate(
        [jnp.broadcast_to(x[(i + 1) * chunk - 1:(i + 1) * chunk, :], (chunk, x.shape[1])) for i in range(bb)],
        axis=0)


def _lower_kernel(lb_ref, o_ref):
    x = lb_ref[...]
    depth = x.shape[0]
    m = x[0:1]
    for l in range(1, depth):
        m = jnp.maximum(m, x[l:l + 1])
    e = jnp.exp(x - m)
    tot = e[0:1]
    for l in range(1, depth):
        tot = tot + e[l:l + 1]
    sm = e / tot
    acc = jnp.zeros_like(m)
    for l in range(depth):
        acc = acc + sm[l:l + 1]
        o_ref[l:l + 1, :] = acc - sm[0:1]


def _lower_bounds(hgrn_lb):
    return pl.pallas_call(
        _lower_kernel, out_shape=jax.ShapeDtypeStruct(hgrn_lb.shape, F32), name="hgrn_lower",
    )(hgrn_lb.astype(F32))


def _in_proj_kernel(x_ref, g_ref, w_ref, o_ref, h_ref):
    @pl.when(pl.program_id(1) == 0)
    def _():
        x = x_ref[...]
        ms = jnp.mean(x * x, axis=-1, keepdims=True)
        h_ref[...] = (x * lax.rsqrt(ms + RMS_EPS) * g_ref[...]).astype(BF16)

    o_ref[...] = jnp.dot(h_ref[...], w_ref[...], preferred_element_type=F32)


def _in_proj(x, g, w, tm, tn):
    n, d = x.shape
    cols = w.shape[1]
    return pl.pallas_call(
        _in_proj_kernel,
        grid=(n // tm, cols // tn),
        in_specs=[
            pl.BlockSpec((tm, d), lambda i, j: (i, 0)),
            pl.BlockSpec((1, d), lambda i, j: (0, 0)),
            pl.BlockSpec((d, tn), lambda i, j: (0, j)),
        ],
        out_specs=pl.BlockSpec((tm, tn), lambda i, j: (i, j)),
        out_shape=jax.ShapeDtypeStruct((n, cols), F32),
        scratch_shapes=[pltpu.VMEM((tm, d), BF16)],
        compiler_params=_cparams(("parallel", "arbitrary")),
        name="in_proj",
    )(x, g, w)


def _rwkv_kernel(u_ref, shift_ref, s0_ref, mu_ref, w0_ref, w2_ref, a0_ref, a2_ref, g2_ref,
                 kk_ref, ka_ref, rk_ref, lnw_ref, lnb_ref, hsum_ref,
                 o_ref, shift_out_ref, s_out_ref, st_scr, prev_scr, o_scr, *, chunk, valid, n_chunks, bb):
    n = pl.program_id(1)
    rows = bb * chunk
    units = [(i, h) for i in range(bb) for h in range(A_HEADS)]

    @pl.when(n == 0)
    def _():
        for i, h in units:
            st_scr[i, h] = s0_ref[i, h].T
        prev_scr[...] = shift_ref[...]

    u = u_ref[...].reshape(rows, A_COLS)
    row = lax.broadcasted_iota(jnp.int32, (rows, 1), 0)
    pos = row & (chunk - 1)
    prev = jnp.concatenate([jnp.broadcast_to(prev_scr[i], (chunk, A_COLS)) for i in range(bb)], axis=0)
    shifted = jnp.where(pos == 0, prev, pltpu.roll(u, 1, axis=0))
    xm = u + mu_ref[...] * (shifted - u)
    for i in range(bb):
        prev_scr[i] = u[(i + 1) * chunk - 1:(i + 1) * chunk, :]

    r = xm[:, 0:BR_W]
    k = xm[:, BR_W:2 * BR_W]
    v = xm[:, 2 * BR_W:3 * BR_W]
    wl = xm[:, 3 * BR_W:3 * BR_W + LORA_W]
    al = xm[:, 3 * BR_W + LORA_W:3 * BR_W + LORA_W + LORA_A]
    gl = xm[:, 3 * BR_W + LORA_W + LORA_A:A_COLS]

    z = w0_ref[...] + _bdot(jnp.tanh(wl), w2_ref[...])
    ld = -math.exp(-0.5) * _sigmoid(z)
    a = _sigmoid(a0_ref[...] + _bdot(al, a2_ref[...]))
    g = _bdot(_sigmoid(gl), g2_ref[...])
    kk = k * kk_ref[...]
    ss = _sel_dot(hsum_ref[...], kk * kk, 2, sel_first=False)
    kk = kk / jnp.maximum(jnp.sqrt(ss), 1e-12)
    k = k * (1.0 + (a - 1.0) * ka_ref[...])
    if valid < chunk:
        ok = pos < valid
        ld = jnp.where(ok, ld, 0.0)
        kk = jnp.where(ok, kk, 0.0)
        k = jnp.where(ok, k, 0.0)
    bv = kk * a

    cl = _chunk_cumsum(ld, chunk)
    cl_last = _last_rows(cl, chunk, bb)
    e_ncl = jnp.exp(-cl)
    e_rem = jnp.exp(cl_last - cl)
    alpha_t = -kk * jnp.exp(cl - ld)
    r_t = r * jnp.exp(cl)
    b_bar = bv * e_ncl
    k_bar = k * e_ncl
    b_hat = bv * e_rem
    k_hat = k * e_rem
    e_last = jnp.exp(cl_last)
    rkk = r * k * rk_ref[...]

    qr = lax.broadcasted_iota(jnp.int32, (2 * chunk, 2 * chunk), 0)
    qc = lax.broadcasted_iota(jnp.int32, (2 * chunk, 2 * chunk), 1)
    qt, qs = qr & (chunk - 1), qc & (chunk - 1)
    keep = (qt > qs) | ((qr >= chunk) & (qt == qs))

    def blk(t, i, h):
        return t[i * chunk:(i + 1) * chunk, h * A_HD:(h + 1) * A_HD]

    aa, st0, vv, x, p = {}, {}, {}, {}, {}
    for i, h in units:
        lhs = jnp.concatenate([blk(alpha_t, i, h), blk(r_t, i, h)], axis=0)
        rhs = jnp.concatenate([blk(b_bar, i, h), blk(k_bar, i, h)], axis=0)
        aa[i, h] = jnp.where(keep, _bdot(lhs, rhs, NT), 0.0)
        st0[i, h] = st_scr[i, h]
        vv[i, h] = blk(v, i, h)
    for i, h in units:
        lhs = jnp.concatenate([blk(alpha_t, i, h), aa[i, h][0:chunk, chunk:2 * chunk]], axis=1)
        x[i, h] = _bdot(lhs, jnp.concatenate([st0[i, h], vv[i, h]], axis=0))
        p[i, h] = aa[i, h][0:chunk, 0:chunk]
    for i, h in units:
        x[i, h] = x[i, h] + _bdot(p[i, h], x[i, h])
    span = 2
    while span < chunk:
        for i, h in units:
            p[i, h] = _bdot(p[i, h], p[i, h])
        for i, h in units:
            x[i, h] = x[i, h] + _bdot(p[i, h], x[i, h])
        span *= 2
    for i, h in units:
        sl = slice(h * A_HD, (h + 1) * A_HD)
        lhs = jnp.concatenate([blk(r_t, i, h), aa[i, h][chunk:2 * chunk, :]], axis=1)
        o = _bdot(lhs, jnp.concatenate([st0[i, h], x[i, h], vv[i, h]], axis=0))
        upd = _bdot(jnp.concatenate([blk(b_hat, i, h), blk(k_hat, i, h)], axis=0),
                    jnp.concatenate([x[i, h], vv[i, h]], axis=0), TN)
        dec = jnp.broadcast_to(e_last[i * chunk:i * chunk + 1, sl], (A_HD, A_HD)).T
        st_scr[i, h] = st0[i, h] * dec + upd
        o_scr[i * chunk:(i + 1) * chunk, sl] = o

    hsum = hsum_ref[...]
    o = o_scr[...]
    d = o - _sel_dot(hsum, o, 2, sel_first=False) * (1.0 / A_HD)
    var = _sel_dot(hsum, d * d, 2, sel_first=False) * (1.0 / A_HD)
    o = d * lax.rsqrt(var + RWKV_GN_EPS) * lnw_ref[...] + lnb_ref[...]
    o = o + _sel_dot(hsum, rkk, 2, sel_first=False) * v
    o_ref[...] = (o * g).reshape(bb, chunk, BR_W)

    @pl.when(n == n_chunks - 1)
    def _():
        last = (valid - 1) if n_chunks == 1 else (chunk - 1)
        for i in range(bb):
            shift_out_ref[i] = u[i * chunk + last:i * chunk + last + 1, :]
        for i, h in units:
            s_out_ref[i, h] = st_scr[i, h].T


def _rwkv(u, shift, wkv, layer, p, chunk, valid, bb):
    batch, seq, _ = u.shape
    n_chunks = seq // chunk
    assert valid == chunk or n_chunks == 1
    a_blk = U_A_OFF // A_COLS
    vec = lambda w: pl.BlockSpec((1, w), lambda b, n: (0, 0))
    full = lambda s: pl.BlockSpec(s, lambda b, n: (0,) * len(s))
    kern = functools.partial(_rwkv_kernel, chunk=chunk, valid=valid, n_chunks=n_chunks, bb=bb)
    return pl.pallas_call(
        kern,
        grid=(batch // bb, n_chunks),
        in_specs=[
            pl.BlockSpec((bb, chunk, A_COLS), lambda b, n: (b, n, a_blk)),
            pl.BlockSpec((None, bb, 1, A_COLS), lambda b, n: (layer, b, 0, 0)),
            pl.BlockSpec((None, bb, A_HEADS, A_HD, A_HD), lambda b, n: (layer, b, 0, 0, 0)),
            vec(A_COLS), vec(BR_W), full((LORA_W, BR_W)), vec(BR_W), full((LORA_A, BR_W)),
            full((LORA_G, BR_W)), vec(BR_W), vec(BR_W), vec(BR_W), vec(BR_W), vec(BR_W),
            full((BR_W, BR_W)),
        ],
        out_specs=[
            pl.BlockSpec((bb, chunk, BR_W), lambda b, n: (b, n, 0)),
            pl.BlockSpec((bb, 1, A_COLS), lambda b, n: (b, 0, 0)),
            pl.BlockSpec((bb, A_HEADS, A_HD, A_HD), lambda b, n: (b, 0, 0, 0)),
        ],
        out_shape=[
            jax.ShapeDtypeStruct((batch, seq, BR_W), F32),
            jax.ShapeDtypeStruct((batch, 1, A_COLS), F32),
            jax.ShapeDtypeStruct((batch, A_HEADS, A_HD, A_HD), F32),
        ],
        scratch_shapes=[pltpu.VMEM((bb, A_HEADS, A_HD, A_HD), F32), pltpu.VMEM((bb, 1, A_COLS), F32),
                        pltpu.VMEM((bb * chunk, BR_W), F32)],
        compiler_params=_cparams(("parallel", "arbitrary")),
        name="rwkv7",
    )(u, shift, wkv, p["mu"], p["w0"], p["w2"], p["a0"], p["a2"], p["g2"], p["kk"], p["ka"],
      p["rk"], p["lnw"], p["lnb"], p["hsum"])


def _ret_kernel(u_ref, cos_ref, sin_ref, s0_ref, gn_ref, hsum_ref, o_ref, s_out_ref, s_scr, o_scr,
                *, chunk, valid, n_chunks, bb):
    n = pl.program_id(1)
    units = [(i, h) for i in range(bb) for h in range(B_HEADS)]

    @pl.when(n == 0)
    def _():
        s_scr[...] = s0_ref[...]

    cos = cos_ref[...]
    sin = sin_ref[...]
    tr = lax.broadcasted_iota(jnp.int32, (chunk, chunk), 0)
    tc = lax.broadcasted_iota(jnp.int32, (chunk, chunk), 1)
    diff = (tr - tc).astype(F32)
    causal = tr >= tc
    pos = lax.broadcasted_iota(jnp.int32, (chunk, 1), 0)
    posf = pos.astype(F32)
    lgs = [math.log(1.0 - 2.0 ** (-5.0 - h)) for h in range(B_HEADS)]
    intra = [jnp.where(causal, jnp.exp(lg * jnp.maximum(diff, 0.0)), 0.0) for lg in lgs]
    q_dec = [jnp.exp(lg * (posf + 1.0)) for lg in lgs]
    k_dec = [jnp.where(pos < valid, jnp.exp(lg * (valid - 1.0 - posf)), 0.0) for lg in lgs]
    s_dec = [math.exp(lg * valid) for lg in lgs]

    q, k, v, s0, sc = {}, {}, {}, {}, {}
    for i, h in units:
        ui = u_ref[i]
        qh = ui[:, h * B_HD:(h + 1) * B_HD]
        kh = ui[:, BR_W + h * B_HD:BR_W + (h + 1) * B_HD]
        q[i, h] = qh * cos + pltpu.roll(qh, B_HD // 2, axis=1) * sin
        k[i, h] = (kh * cos + pltpu.roll(kh, B_HD // 2, axis=1) * sin) * (B_HD ** -0.5)
        v[i, h] = ui[:, 2 * BR_W + h * B_HD:2 * BR_W + (h + 1) * B_HD]
        s0[i, h] = s_scr[i, h]
    for i, h in units:
        sc[i, h] = _bdot(q[i, h], k[i, h], NT) * intra[h]
    for i, h in units:
        sl = slice(h * B_HD, (h + 1) * B_HD)
        o = _bdot(sc[i, h], v[i, h]) + _bdot(q[i, h] * q_dec[h], s0[i, h])
        s_scr[i, h] = s0[i, h] * s_dec[h] + _bdot(k[i, h] * k_dec[h], v[i, h], TN)
        o_scr[i * chunk:(i + 1) * chunk, sl] = o

    hsum = hsum_ref[...]
    o = o_scr[...]
    d = o - _sel_dot(hsum, o, 2, sel_first=False) * (1.0 / B_HD)
    var = _sel_dot(hsum, d * d, 2, sel_first=False) * (1.0 / B_HD)
    gt = u_ref[:, :, 3 * BR_W:4 * BR_W].reshape(bb * chunk, BR_W)
    o_ref[...] = (d * lax.rsqrt(var + GN_EPS) * gn_ref[...] * _silu(gt)).reshape(bb, chunk, BR_W)

    @pl.when(n == n_chunks - 1)
    def _():
        s_out_ref[...] = s_scr[...]


def _retention(u, cos, sin, state, layer, gn, hsum, chunk, valid, bb):
    batch, seq, _ = u.shape
    n_chunks = seq // chunk
    assert valid == chunk or n_chunks == 1
    kern = functools.partial(_ret_kernel, chunk=chunk, valid=valid, n_chunks=n_chunks, bb=bb)
    return pl.pallas_call(
        kern,
        grid=(batch // bb, n_chunks),
        in_specs=[
            pl.BlockSpec((bb, chunk, B_COLS), lambda b, n: (b, n, U_B_OFF // B_COLS)),
            pl.BlockSpec((chunk, B_HD), lambda b, n: (n, 0)),
            pl.BlockSpec((chunk, B_HD), lambda b, n: (n, 0)),
            pl.BlockSpec((None, bb, B_HEADS, B_HD, B_HD), lambda b, n: (layer, b, 0, 0, 0)),
            pl.BlockSpec((1, BR_W), lambda b, n: (0, 0)),
            pl.BlockSpec((BR_W, BR_W), lambda b, n: (0, 0)),
        ],
        out_specs=[
            pl.BlockSpec((bb, chunk, BR_W), lambda b, n: (b, n, 0)),
            pl.BlockSpec((bb, B_HEADS, B_HD, B_HD), lambda b, n: (b, 0, 0, 0)),
        ],
        out_shape=[
            jax.ShapeDtypeStruct((batch, seq, BR_W), F32),
            jax.ShapeDtypeStruct((batch, B_HEADS, B_HD, B_HD), F32),
        ],
        scratch_shapes=[pltpu.VMEM((bb, B_HEADS, B_HD, B_HD), F32), pltpu.VMEM((bb * chunk, BR_W), F32)],
        compiler_params=_cparams(("parallel", "arbitrary")),
        name="retention",
    )(u, cos, sin, state, gn, hsum)


def _hgrn_kernel(u_ref, lb_ref, s0_ref, gn_ref, hsum_ref, o_ref, s_out_ref, st_scr, o_scr,
                 *, chunk, valid, n_chunks, bb):
    n = pl.program_id(1)
    rows = bb * chunk
    units = [(i, h) for i in range(bb) for h in range(C_HEADS)]

    @pl.when(n == 0)
    def _():
        for i, h in units:
            st_scr[i, h] = s0_ref[i, h].T

    lb = lb_ref[...]
    row = lax.broadcasted_iota(jnp.int32, (rows, 1), 0)
    sig_f = _sigmoid(u_ref[:, :, BR_W:2 * BR_W].reshape(rows, BR_W))
    lf = jnp.log(jnp.maximum(lb + (1.0 - lb) * sig_f, F_FLOOR))
    kc = (1.0 - lb) * (1.0 - sig_f)
    if valid < chunk:
        ok = (row & (chunk - 1)) < valid
        lf = jnp.where(ok, lf, 0.0)
        kc = jnp.where(ok, kc, 0.0)
    bc = _chunk_cumsum(lf, chunk)
    b_last = _last_rows(bc, chunk, bb)
    e_b = jnp.exp(bc)
    e_last = jnp.exp(b_last)
    k_hat = kc * jnp.exp(b_last - bc)
    n_sub = chunk // SUB
    tr = lax.broadcasted_iota(jnp.int32, (chunk, chunk), 0)
    tc = lax.broadcasted_iota(jnp.int32, (chunk, chunk), 1)
    lane = lax.broadcasted_iota(jnp.int32, (SUB, chunk), 1)
    t_sub = lax.broadcasted_iota(jnp.int32, (SUB, 1), 0)

    def blk(t, i, h):
        return t[i * chunk:(i + 1) * chunk, h * C_HD:(h + 1) * C_HD]

    q, k, v, b, st, scores = {}, {}, {}, {}, {}, {}
    for i, h in units:
        q[i, h] = u_ref[i, :, h * C_HD:(h + 1) * C_HD]
        v[i, h] = u_ref[i, :, 2 * BR_W + h * C_HD:2 * BR_W + (h + 1) * C_HD]
        k[i, h] = blk(kc, i, h)
        b[i, h] = blk(bc, i, h)
        st[i, h] = st_scr[i, h]
        scores[i, h] = jnp.zeros((chunk, chunk), F32)

    size = SUB
    while size < chunk:
        sh = (2 * size).bit_length() - 1
        same_pair = (tr >> sh) == (tc >> sh)
        for i, h in units:
            qs, ks = [], []
            for j in range(chunk // size):
                sel = slice(j * size, (j + 1) * size)
                if j % 2 == 1:
                    ref = b[i, h][j * size - 1:j * size, :]
                    qs.append(q[i, h][sel] * jnp.exp(b[i, h][sel] - ref))
                    ks.append(jnp.zeros((size, C_HD), F32))
                else:
                    ref = b[i, h][(j + 1) * size - 1:(j + 1) * size, :]
                    qs.append(jnp.zeros((size, C_HD), F32))
                    ks.append(k[i, h][sel] * jnp.exp(ref - b[i, h][sel]))
            lvl = _bdot(jnp.concatenate(qs, axis=0), jnp.concatenate(ks, axis=0), NT)
            scores[i, h] = scores[i, h] + jnp.where(same_pair, lvl, 0.0)
        size *= 2

    for i, h in units:
        diag_rows = []
        for j in range(n_sub):
            sel = slice(j * SUB, (j + 1) * SUB)
            qb, kb, bb_ = q[i, h][sel], k[i, h][sel], b[i, h][sel]
            d = jnp.zeros((SUB, chunk), F32)
            for s in range(SUB):
                w = jnp.exp(jnp.minimum(bb_ - bb_[s:s + 1, :], 0.0))
                col = jnp.sum(qb * kb[s:s + 1, :] * w, axis=-1, keepdims=True)
                d = d + jnp.where((lane == j * SUB + s) & (t_sub >= s), col, 0.0)
            diag_rows.append(d)
        scores[i, h] = scores[i, h] + jnp.concatenate(diag_rows, axis=0)

    for i, h in units:
        sl = slice(h * C_HD, (h + 1) * C_HD)
        o = _bdot(scores[i, h], v[i, h]) + _bdot(q[i, h] * blk(e_b, i, h), st[i, h], NT)
        st_scr[i, h] = st[i, h] * e_last[i * chunk:i * chunk + 1, sl] + _bdot(v[i, h], blk(k_hat, i, h), TN)
        o_scr[i * chunk:(i + 1) * chunk, sl] = o

    o = o_scr[...]
    ms = _sel_dot(hsum_ref[...], o * o, 2, sel_first=False) * (1.0 / C_HD)
    gt = u_ref[:, :, 3 * BR_W:4 * BR_W].reshape(rows, BR_W)
    o_ref[...] = (o * lax.rsqrt(ms + GN_EPS) * gn_ref[...] * _silu(gt)).reshape(bb, chunk, BR_W)

    @pl.when(n == n_chunks - 1)
    def _():
        for i, h in units:
            s_out_ref[i, h] = st_scr[i, h].T


def _hgrn(u, lb, state, layer, gn, hsum, chunk, valid, bb):
    batch, seq, _ = u.shape
    n_chunks = seq // chunk
    assert valid == chunk or n_chunks == 1
    kern = functools.partial(_hgrn_kernel, chunk=chunk, valid=valid, n_chunks=n_chunks, bb=bb)
    return pl.pallas_call(
        kern,
        grid=(batch // bb, n_chunks),
        in_specs=[
            pl.BlockSpec((bb, chunk, C_COLS), lambda b, n: (b, n, U_C_OFF // C_COLS)),
            pl.BlockSpec((1, BR_W), lambda b, n: (0, 0)),
            pl.BlockSpec((None, bb, C_HEADS, C_HD, C_HD), lambda b, n: (layer, b, 0, 0, 0)),
            pl.BlockSpec((1, BR_W), lambda b, n: (0, 0)),
            pl.BlockSpec((BR_W, BR_W), lambda b, n: (0, 0)),
        ],
        out_specs=[
            pl.BlockSpec((bb, chunk, BR_W), lambda b, n: (b, n, 0)),
            pl.BlockSpec((bb, C_HEADS, C_HD, C_HD), lambda b, n: (b, 0, 0, 0)),
        ],
        out_shape=[
            jax.ShapeDtypeStruct((batch, seq, BR_W), F32),
            jax.ShapeDtypeStruct((batch, C_HEADS, C_HD, C_HD), F32),
        ],
        scratch_shapes=[pltpu.VMEM((bb, C_HEADS, C_HD, C_HD), F32), pltpu.VMEM((bb * chunk, BR_W), F32)],
        compiler_params=_cparams(("parallel", "arbitrary")),
        name="hgrn2",
    )(u, lb, state, gn, hsum)


def _merge_kernel(oa_ref, ob_ref, oc_ref, g0_ref, g1_ref, g2_ref, x_ref, wb_ref, wo_ref, o_ref):
    m = _sigmoid(g0_ref[...]) * _bdot(oa_ref[...], wb_ref[0])
    m = m + _sigmoid(g1_ref[...]) * _bdot(ob_ref[...], wb_ref[1])
    m = m + _sigmoid(g2_ref[...]) * _bdot(oc_ref[...], wb_ref[2])
    o_ref[...] = x_ref[...] + _bdot(m, wo_ref[...])


def _merge(oa, ob, oc, u, x, wb, wo, tm):
    n = x.shape[0]
    g_blk = U_G_OFF // D_MODEL
    br = pl.BlockSpec((tm, BR_W), lambda i: (i, 0))
    gate = lambda c: pl.BlockSpec((tm, D_MODEL), lambda i: (i, g_blk + c))
    return pl.pallas_call(
        _merge_kernel,
        grid=(n // tm,),
        in_specs=[br, br, br, gate(0), gate(1), gate(2),
                  pl.BlockSpec((tm, D_MODEL), lambda i: (i, 0)),
                  pl.BlockSpec((3, BR_W, D_MODEL), lambda i: (0, 0, 0)),
                  pl.BlockSpec((D_MODEL, D_MODEL), lambda i: (0, 0))],
        out_specs=pl.BlockSpec((tm, D_MODEL), lambda i: (i, 0)),
        out_shape=jax.ShapeDtypeStruct((n, D_MODEL), F32),
        compiler_params=_cparams(("parallel",)),
        name="merge",
    )(oa, ob, oc, u, u, u, x, wb, wo)


def _ffn_kernel(x_ref, g_ref, wg_ref, wu_ref, wo_ref, o_ref, h_ref):
    @pl.when(pl.program_id(1) == 0)
    def _():
        x = x_ref[...]
        ms = jnp.mean(x * x, axis=-1, keepdims=True)
        h_ref[...] = (x * lax.rsqrt(ms + RMS_EPS) * g_ref[...]).astype(BF16)
        o_ref[...] = x

    h = h_ref[...]
    gt = jnp.dot(h, wg_ref[...], preferred_element_type=F32)
    up = jnp.dot(h, wu_ref[...], preferred_element_type=F32)
    o_ref[...] += _bdot(_silu(gt) * up, wo_ref[...])


def _ffn(x, g, w_in, w_out, tm, tf):
    n = x.shape[0]
    n_ff = D_FF // tf
    return pl.pallas_call(
        _ffn_kernel,
        grid=(n // tm, n_ff),
        in_specs=[
            pl.BlockSpec((tm, D_MODEL), lambda i, j: (i, 0)),
            pl.BlockSpec((1, D_MODEL), lambda i, j: (0, 0)),
            pl.BlockSpec((D_MODEL, tf), lambda i, j: (0, j)),
            pl.BlockSpec((D_MODEL, tf), lambda i, j: (0, n_ff + j)),
            pl.BlockSpec((tf, D_MODEL), lambda i, j: (j, 0)),
        ],
        out_specs=pl.BlockSpec((tm, D_MODEL), lambda i, j: (i, 0)),
        out_shape=jax.ShapeDtypeStruct((n, D_MODEL), F32),
        scratch_shapes=[pltpu.VMEM((tm, D_MODEL), BF16)],
        compiler_params=_cparams(("parallel", "arbitrary")),
        name="ffn",
    )(x, g, w_in, w_in, w_out)


def _final_norm_kernel(x_ref, g_ref, o_ref):
    x = x_ref[...]
    ms = jnp.mean(x * x, axis=-1, keepdims=True)
    o_ref[...] = x * lax.rsqrt(ms + RMS_EPS) * g_ref[...]


def _final_norm(x, g, tm):
    n = x.shape[0]
    return pl.pallas_call(
        _final_norm_kernel,
        grid=(n // tm,),
        in_specs=[pl.BlockSpec((tm, D_MODEL), lambda i: (i, 0)),
                  pl.BlockSpec((1, D_MODEL), lambda i: (0, 0))],
        out_specs=pl.BlockSpec((tm, D_MODEL), lambda i: (i, 0)),
        out_shape=jax.ShapeDtypeStruct((n, D_MODEL), F32),
        compiler_params=_cparams(("parallel",)),
        name="final_norm",
    )(x, g)


def _rope_tables(pos):
    half = B_HD // 2
    inv = ROPE_BASE ** (-jnp.arange(half, dtype=F32) / half)
    ang = pos.astype(F32)[:, None] * inv[None, :]
    cos, sin = jnp.cos(ang), jnp.sin(ang)
    return jnp.concatenate([cos, cos], axis=-1), jnp.concatenate([-sin, sin], axis=-1)


def _row_tile(n, largest=512):
    t = largest
    while t >= 8:
        if n % t == 0:
            return t
        t //= 2
    raise ValueError(f"row count {n} is not a multiple of 8")


def _layer(x, states, layer, cos, sin, p, batch, seq, valid, cfg):
    shift, wkv, ret, hg = states
    tm = _row_tile(x.shape[0])
    u2 = _in_proj(x, p["norm_mix"], p["w_in"], _row_tile(x.shape[0], IN_ROW_TILE), IN_COL_TILE)
    u = u2.reshape(batch, seq, IN_COLS)
    full = lambda c: valid if c >= seq else c
    (ca, sa), (cb, sb), (cc, sc) = cfg["rwkv"], cfg["ret"], cfg["hgrn"]
    oa, shift_new, wkv_new = _rwkv(u, shift, wkv, layer, p, ca, full(ca), sa)
    ob, ret_new = _retention(u, cos, sin, ret, layer, p["ret_gn"], p["hsum128"], cb, full(cb), sb)
    oc, hg_new = _hgrn(u, p["lower"], hg, layer, p["hgrn_gn"], p["hsum128"], cc, full(cc), sc)
    flat = lambda o: o.reshape(batch * seq, BR_W)
    x = _merge(flat(oa), flat(ob), flat(oc), u2, x, p["w_branch"], p["w_out"], tm)
    x = _ffn(x, p["norm_ffn"], p["w_ffn_in"], p["w_ffn_out"], tm, FF_TILE)
    return x, shift_new, wkv_new, ret_new, hg_new


def kernel(x_prompt, x_sample, state_rwkv_shift, state_rwkv_wkv, state_ret, state_hgrn,
           norm_mix, w_in, rwkv_mu, rwkv_w0, rwkv_w2, rwkv_a0, rwkv_a2, rwkv_g2, rwkv_kk,
           rwkv_ka, rwkv_rk, rwkv_ln_w, rwkv_ln_b, ret_gn, hgrn_lb, hgrn_gn, w_branch, w_out,
           norm_ffn, w_ffn_in, w_ffn_out, norm_final):
    depth = w_in.shape[0]
    bp, tp, _ = x_prompt.shape
    bs, ts, _ = x_sample.shape
    assert ts <= SAMPLE_CHUNK
    assert all(tp % c == 0 and bp % s == 0 for c, s in PROMPT_CFG.values())
    assert all(bs % s == 0 for _, s in SAMPLE_CFG.values())
    dt = x_prompt.dtype

    lower = _lower_bounds(hgrn_lb)
    w_in_r = jnp.concatenate(
        [w_in[:, :, A_COLS:A_COLS + B_COLS + C_COLS + GATE_COLS], w_in[:, :, :A_COLS]], axis=-1).astype(BF16)
    w_branch_b = w_branch.astype(BF16)
    w_out_b = w_out.astype(BF16)
    w_ffn_in_b = w_ffn_in.astype(BF16)
    w_ffn_out_b = w_ffn_out.astype(BF16)
    lanes = jnp.arange(BR_W)
    hsum = (lanes[:, None] // A_HD == lanes[None, :] // A_HD).astype(BF16)
    hsum128 = (lanes[:, None] // B_HD == lanes[None, :] // B_HD).astype(BF16)
    row = lambda a: a.reshape(1, -1).astype(F32)

    cos_p, sin_p = _rope_tables(jnp.arange(tp, dtype=jnp.int32))
    cos_s, sin_s = _rope_tables(PAST_LEN + jnp.arange(SAMPLE_CHUNK, dtype=jnp.int32))

    xp = x_prompt.reshape(bp * tp, D_MODEL)
    xs = jnp.pad(x_sample, ((0, 0), (0, SAMPLE_CHUNK - ts), (0, 0))).reshape(bs * SAMPLE_CHUNK, D_MODEL)
    zero_states = (jnp.zeros((1, bp, 1, A_COLS), F32), jnp.zeros((1, bp, A_HEADS, A_HD, A_HD), F32),
                   jnp.zeros((1, bp, B_HEADS, B_HD, B_HD), F32), jnp.zeros((1, bp, C_HEADS, C_HD, C_HD), F32))
    sample_states = (state_rwkv_shift.reshape(depth, bs, 1, A_COLS), state_rwkv_wkv, state_ret, state_hgrn)

    outs_p = [[], [], [], []]
    outs_s = [[], [], [], []]
    for l in range(depth):
        p = {
            "norm_mix": row(norm_mix[l]), "w_in": w_in_r[l], "mu": row(rwkv_mu[l]),
            "w0": row(rwkv_w0[l]), "w2": rwkv_w2[l].astype(BF16), "a0": row(rwkv_a0[l]),
            "a2": rwkv_a2[l].astype(BF16), "g2": rwkv_g2[l].astype(BF16), "kk": row(rwkv_kk[l]),
            "ka": row(rwkv_ka[l]), "rk": row(rwkv_rk[l]), "lnw": row(rwkv_ln_w[l]),
            "lnb": row(rwkv_ln_b[l]), "hsum": hsum, "hsum128": hsum128, "ret_gn": row(ret_gn[l]),
            "lower": lower[l:l + 1], "hgrn_gn": row(hgrn_gn[l]), "w_branch": w_branch_b[l],
            "w_out": w_out_b[l], "norm_ffn": row(norm_ffn[l]), "w_ffn_in": w_ffn_in_b[l],
            "w_ffn_out": w_ffn_out_b[l],
        }
        xp, a1, a2, a3, a4 = _layer(xp, zero_states, 0, cos_p, sin_p, p, bp, tp, tp, PROMPT_CFG)
        xs, b1, b2, b3, b4 = _layer(xs, sample_states, l, cos_s, sin_s, p, bs, SAMPLE_CHUNK, ts, SAMPLE_CFG)
        for acc, val in zip(outs_p, (a1.reshape(bp, A_COLS), a2, a3, a4)):
            acc.append(val.astype(dt))
        for acc, val in zip(outs_s, (b1.reshape(bs, A_COLS), b2, b3, b4)):
            acc.append(val.astype(dt))

    g_fin = row(norm_final)
    y_prompt = _final_norm(xp, g_fin, _row_tile(xp.shape[0])).reshape(bp, tp, D_MODEL)
    y_sample = _final_norm(xs, g_fin, _row_tile(xs.shape[0])).reshape(bs, SAMPLE_CHUNK, D_MODEL)[:, :ts]
    return (y_prompt, y_sample, *[jnp.stack(o) for o in outs_p], *[jnp.stack(o) for o in outs_s])
```

```python
import functools
import math

import jax
import jax.numpy as jnp
from jax import lax
from jax.experimental import pallas as pl
from jax.experimental.pallas import tpu as pltpu

F32 = jnp.float32
BF16 = jnp.bfloat16

D_MODEL = 1024
BR_W = 512
A_HD = 64
A_HEADS = BR_W // A_HD
LORA_W = 64
LORA_A = 64
LORA_G = 128
A_COLS = 3 * BR_W + LORA_W + LORA_A + LORA_G
B_HD = 128
B_HEADS = BR_W // B_HD
C_HD = 128
C_HEADS = BR_W // C_HD
B_COLS = 4 * BR_W
C_COLS = 4 * BR_W
GATE_COLS = 3 * D_MODEL
IN_COLS = A_COLS + B_COLS + C_COLS + GATE_COLS
D_FF = 2816
PAST_LEN = 16384
RMS_EPS = 1e-6
RWKV_GN_EPS = 64e-5
GN_EPS = 1e-5
F_FLOOR = 1e-30
ROPE_BASE = 10000.0

U_B_OFF = 0
U_C_OFF = B_COLS
U_G_OFF = B_COLS + C_COLS
U_A_OFF = B_COLS + C_COLS + GATE_COLS

SAMPLE_CHUNK = 8
PROMPT_CFG = {"rwkv": (64, 4), "ret": (128, 8), "hgrn": (128, 4)}
SAMPLE_CFG = {"rwkv": (SAMPLE_CHUNK, 4), "ret": (SAMPLE_CHUNK, 8), "hgrn": (SAMPLE_CHUNK, 8)}
IN_ROW_TILE = 1024
IN_COL_TILE = 1280
FF_TILE = 1408

VMEM_LIMIT = 56 * 1024 * 1024

NN = (((1,), (0,)), ((), ()))
NT = (((1,), (1,)), ((), ()))
TN = (((0,), (0,)), ((), ()))


def _bdot(a, b, dims=NN):
    return lax.dot_general(a.astype(BF16), b.astype(BF16), dims, preferred_element_type=F32)


def _sel_dot(sel, x, passes, sel_first=True):
    sel = sel.astype(BF16)
    acc = None
    for _ in range(passes):
        piece = x.astype(BF16)
        ops = (sel, piece) if sel_first else (piece, sel)
        part = lax.dot_general(*ops, NN, preferred_element_type=F32)
        acc = part if acc is None else acc + part
        x = x - piece.astype(F32)
    return acc


def _sigmoid(x):
    return 0.5 * jnp.tanh(0.5 * x) + 0.5


def _silu(x):
    return x * _sigmoid(x)


def _cparams(sem):
    return pltpu.CompilerParams(dimension_semantics=sem, vmem_limit_bytes=VMEM_LIMIT)


def _chunk_cumsum(x, chunk):
    rows = x.shape[0]
    group = max(chunk, min(rows, 64))
    assert rows % group == 0
    tr = lax.broadcasted_iota(jnp.int32, (group, group), 0)
    tc = lax.broadcasted_iota(jnp.int32, (group, group), 1)
    sh = chunk.bit_length() - 1
    tri = jnp.where((tr >= tc) & ((tr >> sh) == (tc >> sh)), 1.0, 0.0)
    return jnp.concatenate([_sel_dot(tri, x[r:r + group], 3) for r in range(0, rows, group)], axis=0)


def _last_rows(x, chunk, bb):
    return jnp.concatenate(
        [jnp.broadcast_to(x[(i + 1) * chunk - 1:(i + 1) * chunk, :], (chunk, x.shape[1])) for i in range(bb)],
        axis=0)


def _lower_kernel(lb_ref, o_ref):
    x = lb_ref[...]
    depth = x.shape[0]
    m = x[0:1]
    for l in range(1, depth):
        m = jnp.maximum(m, x[l:l + 1])
    e = jnp.exp(x - m)
    tot = e[0:1]
    for l in range(1, depth):
        tot = tot + e[l:l + 1]
    sm = e / tot
    acc = jnp.zeros_like(m)
    for l in range(depth):
        acc = acc + sm[l:l + 1]
        o_ref[l:l + 1, :] = acc - sm[0:1]


def _lower_bounds(hgrn_lb):
    return pl.pallas_call(
        _lower_kernel, out_shape=jax.ShapeDtypeStruct(hgrn_lb.shape, F32), name="hgrn_lower",
    )(hgrn_lb.astype(F32))


def _in_proj_kernel(x_ref, g_ref, w_ref, o_ref, h_ref):
    @pl.when(pl.program_id(1) == 0)
    def _():
        x = x_ref[...]
        ms = jnp.mean(x * x, axis=-1, keepdims=True)
        h_ref[...] = (x * lax.rsqrt(ms + RMS_EPS) * g_ref[...]).astype(BF16)

    o_ref[...] = jnp.dot(h_ref[...], w_ref[...], preferred_element_type=F32)


def _in_proj(x, g, w, tm, tn):
    n, d = x.shape
    cols = w.shape[1]
    return pl.pallas_call(
        _in_proj_kernel,
        grid=(n // tm, cols // tn),
        in_specs=[
            pl.BlockSpec((tm, d), lambda i, j: (i, 0)),
            pl.BlockSpec((1, d), lambda i, j: (0, 0)),
            pl.BlockSpec((d, tn), lambda i, j: (0, j)),
        ],
        out_specs=pl.BlockSpec((tm, tn), lambda i, j: (i, j)),
        out_shape=jax.ShapeDtypeStruct((n, cols), F32),
        scratch_shapes=[pltpu.VMEM((tm, d), BF16)],
        compiler_params=_cparams(("parallel", "arbitrary")),
        name="in_proj",
    )(x, g, w)


def _rwkv_kernel(u_ref, shift_ref, s0_ref, mu_ref, w0_ref, w2_ref, a0_ref, a2_ref, g2_ref,
                 kk_ref, ka_ref, rk_ref, lnw_ref, lnb_ref, hsum_ref,
                 o_ref, shift_out_ref, s_out_ref, st_scr, prev_scr, o_scr, *, chunk, valid, n_chunks, bb):
    n = pl.program_id(1)
    rows = bb * chunk
    units = [(i, h) for i in range(bb) for h in range(A_HEADS)]

    @pl.when(n == 0)
    def _():
        for i, h in units:
            st_scr[i, h] = s0_ref[i, h].T
        prev_scr[...] = shift_ref[...]

    u = u_ref[...].reshape(rows, A_COLS)
    row = lax.broadcasted_iota(jnp.int32, (rows, 1), 0)
    pos = row & (chunk - 1)
    prev = jnp.concatenate([jnp.broadcast_to(prev_scr[i], (chunk, A_COLS)) for i in range(bb)], axis=0)
    shifted = jnp.where(pos == 0, prev, pltpu.roll(u, 1, axis=0))
    xm = u + mu_ref[...] * (shifted - u)
    for i in range(bb):
        prev_scr[i] = u[(i + 1) * chunk - 1:(i + 1) * chunk, :]

    r = xm[:, 0:BR_W]
    k = xm[:, BR_W:2 * BR_W]
    v = xm[:, 2 * BR_W:3 * BR_W]
    wl = xm[:, 3 * BR_W:3 * BR_W + LORA_W]
    al = xm[:, 3 * BR_W + LORA_W:3 * BR_W + LORA_W + LORA_A]
    gl = xm[:, 3 * BR_W + LORA_W + LORA_A:A_COLS]

    z = w0_ref[...] + _bdot(jnp.tanh(wl), w2_ref[...])
    ld = -math.exp(-0.5) * _sigmoid(z)
    a = _sigmoid(a0_ref[...] + _bdot(al, a2_ref[...]))
    g = _bdot(_sigmoid(gl), g2_ref[...])
    kk = k * kk_ref[...]
    ss = _sel_dot(hsum_ref[...], kk * kk, 2, sel_first=False)
    kk = kk / jnp.maximum(jnp.sqrt(ss), 1e-12)
    k = k * (1.0 + (a - 1.0) * ka_ref[...])
    if valid < chunk:
        ok = pos < valid
        ld = jnp.where(ok, ld, 0.0)
        kk = jnp.where(ok, kk, 0.0)
        k = jnp.where(ok, k, 0.0)
    bv = kk * a

    cl = _chunk_cumsum(ld, chunk)
    cl_last = _last_rows(cl, chunk, bb)
    e_ncl = jnp.exp(-cl)
    e_rem = jnp.exp(cl_last - cl)
    alpha_t = -kk * jnp.exp(cl - ld)
    r_t = r * jnp.exp(cl)
    b_bar = bv * e_ncl
    k_bar = k * e_ncl
    b_hat = bv * e_rem
    k_hat = k * e_rem
    e_last = jnp.exp(cl_last)
    rkk = r * k * rk_ref[...]

    qr = lax.broadcasted_iota(jnp.int32, (2 * chunk, 2 * chunk), 0)
    qc = lax.broadcasted_iota(jnp.int32, (2 * chunk, 2 * chunk), 1)
    qt, qs = qr & (chunk - 1), qc & (chunk - 1)
    keep = (qt > qs) | ((qr >= chunk) & (qt == qs))

    def blk(t, i, h):
        return t[i * chunk:(i + 1) * chunk, h * A_HD:(h + 1) * A_HD]

    aa, st0, vv, x, p = {}, {}, {}, {}, {}
    for i, h in units:
        lhs = jnp.concatenate([blk(alpha_t, i, h), blk(r_t, i, h)], axis=0)
        rhs = jnp.concatenate([blk(b_bar, i, h), blk(k_bar, i, h)], axis=0)
        aa[i, h] = jnp.where(keep, _bdot(lhs, rhs, NT), 0.0)
        st0[i, h] = st_scr[i, h]
        vv[i, h] = blk(v, i, h)
    for i, h in units:
        lhs = jnp.concatenate([blk(alpha_t, i, h), aa[i, h][0:chunk, chunk:2 * chunk]], axis=1)
        x[i, h] = _bdot(lhs, jnp.concatenate([st0[i, h], vv[i, h]], axis=0))
        p[i, h] = aa[i, h][0:chunk, 0:chunk]
    for i, h in units:
        x[i, h] = x[i, h] + _bdot(p[i, h], x[i, h])
    span = 2
    while span < chunk:
        for i, h in units:
            p[i, h] = _bdot(p[i, h], p[i, h])
        for i, h in units:
            x[i, h] = x[i, h] + _bdot(p[i, h], x[i, h])
        span *= 2
    for i, h in units:
        sl = slice(h * A_HD, (h + 1) * A_HD)
        lhs = jnp.concatenate([blk(r_t, i, h), aa[i, h][chunk:2 * chunk, :]], axis=1)
        o = _bdot(lhs, jnp.concatenate([st0[i, h], x[i, h], vv[i, h]], axis=0))
        upd = _bdot(jnp.concatenate([blk(b_hat, i, h), blk(k_hat, i, h)], axis=0),
                    jnp.concatenate([x[i, h], vv[i, h]], axis=0), TN)
        dec = jnp.broadcast_to(e_last[i * chunk:i * chunk + 1, sl], (A_HD, A_HD)).T
        st_scr[i, h] = st0[i, h] * dec + upd
        o_scr[i * chunk:(i + 1) * chunk, sl] = o

    hsum = hsum_ref[...]
    o = o_scr[...]
    d = o - _sel_dot(hsum, o, 2, sel_first=False) * (1.0 / A_HD)
    var = _sel_dot(hsum, d * d, 2, sel_first=False) * (1.0 / A_HD)
    o = d * lax.rsqrt(var + RWKV_GN_EPS) * lnw_ref[...] + lnb_ref[...]
    o = o + _sel_dot(hsum, rkk, 2, sel_first=False) * v
    o_ref[...] = (o * g).reshape(bb, chunk, BR_W)

    @pl.when(n == n_chunks - 1)
    def _():
        last = (valid - 1) if n_chunks == 1 else (chunk - 1)
        for i in range(bb):
            shift_out_ref[i] = u[i * chunk + last:i * chunk + last + 1, :]
        for i, h in units:
            s_out_ref[i, h] = st_scr[i, h].T


def _rwkv(u, shift, wkv, layer, p, chunk, valid, bb):
    batch, seq, _ = u.shape
    n_chunks = seq // chunk
    assert valid == chunk or n_chunks == 1
    a_blk = U_A_OFF // A_COLS
    vec = lambda w: pl.BlockSpec((1, w), lambda b, n: (0, 0))
    full = lambda s: pl.BlockSpec(s, lambda b, n: (0,) * len(s))
    kern = functools.partial(_rwkv_kernel, chunk=chunk, valid=valid, n_chunks=n_chunks, bb=bb)
    return pl.pallas_call(
        kern,
        grid=(batch // bb, n_chunks),
        in_specs=[
            pl.BlockSpec((bb, chunk, A_COLS), lambda b, n: (b, n, a_blk)),
            pl.BlockSpec((None, bb, 1, A_COLS), lambda b, n: (layer, b, 0, 0)),
            pl.BlockSpec((None, bb, A_HEADS, A_HD, A_HD), lambda b, n: (layer, b, 0, 0, 0)),
            vec(A_COLS), vec(BR_W), full((LORA_W, BR_W)), vec(BR_W), full((LORA_A, BR_W)),
            full((LORA_G, BR_W)), vec(BR_W), vec(BR_W), vec(BR_W), vec(BR_W), vec(BR_W),
            full((BR_W, BR_W)),
        ],
        out_specs=[
            pl.BlockSpec((bb, chunk, BR_W), lambda b, n: (b, n, 0)),
            pl.BlockSpec((bb, 1, A_COLS), lambda b, n: (b, 0, 0)),
            pl.BlockSpec((bb, A_HEADS, A_HD, A_HD), lambda b, n: (b, 0, 0, 0)),
        ],
        out_shape=[
            jax.ShapeDtypeStruct((batch, seq, BR_W), F32),
            jax.ShapeDtypeStruct((batch, 1, A_COLS), F32),
            jax.ShapeDtypeStruct((batch, A_HEADS, A_HD, A_HD), F32),
        ],
        scratch_shapes=[pltpu.VMEM((bb, A_HEADS, A_HD, A_HD), F32), pltpu.VMEM((bb, 1, A_COLS), F32),
                        pltpu.VMEM((bb * chunk, BR_W), F32)],
        compiler_params=_cparams(("parallel", "arbitrary")),
        name="rwkv7",
    )(u, shift, wkv, p["mu"], p["w0"], p["w2"], p["a0"], p["a2"], p["g2"], p["kk"], p["ka"],
      p["rk"], p["lnw"], p["lnb"], p["hsum"])


def _ret_kernel(u_ref, cos_ref, sin_ref, s0_ref, gn_ref, hsum_ref, o_ref, s_out_ref, s_scr, o_scr,
                *, chunk, valid, n_chunks, bb):
    n = pl.program_id(1)
    units = [(i, h) for i in range(bb) for h in range(B_HEADS)]

    @pl.when(n == 0)
    def _():
        s_scr[...] = s0_ref[...]

    cos = cos_ref[...]
    sin = sin_ref[...]
    tr = lax.broadcasted_iota(jnp.int32, (chunk, chunk), 0)
    tc = lax.broadcasted_iota(jnp.int32, (chunk, chunk), 1)
    diff = (tr - tc).astype(F32)
    causal = tr >= tc
    pos = lax.broadcasted_iota(jnp.int32, (chunk, 1), 0)
    posf = pos.astype(F32)
    lgs = [math.log(1.0 - 2.0 ** (-5.0 - h)) for h in range(B_HEADS)]
    intra = [jnp.where(causal, jnp.exp(lg * jnp.maximum(diff, 0.0)), 0.0) for lg in lgs]
    q_dec = [jnp.exp(lg * (posf + 1.0)) for lg in lgs]
    k_dec = [jnp.where(pos < valid, jnp.exp(lg * (valid - 1.0 - posf)), 0.0) for lg in lgs]
    s_dec = [math.exp(lg * valid) for lg in lgs]

    q, k, v, s0, sc = {}, {}, {}, {}, {}
    for i, h in units:
        ui = u_ref[i]
        qh = ui[:, h * B_HD:(h + 1) * B_HD]
        kh = ui[:, BR_W + h * B_HD:BR_W + (h + 1) * B_HD]
        q[i, h] = qh * cos + pltpu.roll(qh, B_HD // 2, axis=1) * sin
        k[i, h] = (kh * cos + pltpu.roll(kh, B_HD // 2, axis=1) * sin) * (B_HD ** -0.5)
        v[i, h] = ui[:, 2 * BR_W + h * B_HD:2 * BR_W + (h + 1) * B_HD]
        s0[i, h] = s_scr[i, h]
    for i, h in units:
        sc[i, h] = _bdot(q[i, h], k[i, h], NT) * intra[h]
    for i, h in units:
        sl = slice(h * B_HD, (h + 1) * B_HD)
        o = _bdot(sc[i, h], v[i, h]) + _bdot(q[i, h] * q_dec[h], s0[i, h])
        s_scr[i, h] = s0[i, h] * s_dec[h] + _bdot(k[i, h] * k_dec[h], v[i, h], TN)
        o_scr[i * chunk:(i + 1) * chunk, sl] = o

    hsum = hsum_ref[...]
    o = o_scr[...]
    d = o - _sel_dot(hsum, o, 2, sel_first=False) * (1.0 / B_HD)
    var = _sel_dot(hsum, d * d, 2, sel_first=False) * (1.0 / B_HD)
    gt = u_ref[:, :, 3 * BR_W:4 * BR_W].reshape(bb * chunk, BR_W)
    o_ref[...] = (d * lax.rsqrt(var + GN_EPS) * gn_ref[...] * _silu(gt)).reshape(bb, chunk, BR_W)

    @pl.when(n == n_chunks - 1)
    def _():
        s_out_ref[...] = s_scr[...]


def _retention(u, cos, sin, state, layer, gn, hsum, chunk, valid, bb):
    batch, seq, _ = u.shape
    n_chunks = seq // chunk
    assert valid == chunk or n_chunks == 1
    kern = functools.partial(_ret_kernel, chunk=chunk, valid=valid, n_chunks=n_chunks, bb=bb)
    return pl.pallas_call(
        kern,
        grid=(batch // bb, n_chunks),
        in_specs=[
            pl.BlockSpec((bb, chunk, B_COLS), lambda b, n: (b, n, U_B_OFF // B_COLS)),
            pl.BlockSpec((chunk, B_HD), lambda b, n: (n, 0)),
            pl.BlockSpec((chunk, B_HD), lambda b, n: (n, 0)),
            pl.BlockSpec((None, bb, B_HEADS, B_HD, B_HD), lambda b, n: (layer, b, 0, 0, 0)),
            pl.BlockSpec((1, BR_W), lambda b, n: (0, 0)),
            pl.BlockSpec((BR_W, BR_W), lambda b, n: (0, 0)),
        ],
        out_specs=[
            pl.BlockSpec((bb, chunk, BR_W), lambda b, n: (b, n, 0)),
            pl.BlockSpec((bb, B_HEADS, B_HD, B_HD), lambda b, n: (b, 0, 0, 0)),
        ],
        out_shape=[
            jax.ShapeDtypeStruct((batch, seq, BR_W), F32),
            jax.ShapeDtypeStruct((batch, B_HEADS, B_HD, B_HD), F32),
        ],
        scratch_shapes=[pltpu.VMEM((bb, B_HEADS, B_HD, B_HD), F32), pltpu.VMEM((bb * chunk, BR_W), F32)],
        compiler_params=_cparams(("parallel", "arbitrary")),
        name="retention",
    )(u, cos, sin, state, gn, hsum)


def _hgrn_kernel(u_ref, lb_ref, s0_ref, gn_ref, hsum_ref, o_ref, s_out_ref, st_scr, o_scr,
                 *, chunk, valid, n_chunks, bb):
    n = pl.program_id(1)
    rows = bb * chunk
    units = [(i, h) for i in range(bb) for h in range(C_HEADS)]

    @pl.when(n == 0)
    def _():
        for i, h in units:
            st_scr[i, h] = s0_ref[i, h].T

    lb = lb_ref[...]
    row = lax.broadcasted_iota(jnp.int32, (rows, 1), 0)
    sig_f = _sigmoid(u_ref[:, :, BR_W:2 * BR_W].reshape(rows, BR_W))
    lf = jnp.log(jnp.maximum(lb + (1.0 - lb) * sig_f, F_FLOOR))
    kc = (1.0 - lb) * (1.0 - sig_f)
    if valid < chunk:
        ok = (row & (chunk - 1)) < valid
        lf = jnp.where(ok, lf, 0.0)
        kc = jnp.where(ok, kc, 0.0)
    bc = _chunk_cumsum(lf, chunk)
    b_last = _last_rows(bc, chunk, bb)
    e_b = jnp.exp(bc)
    e_last = jnp.exp(b_last)
    k_hat = kc * jnp.exp(b_last - bc)
    q_all = u_ref[:, :, 0:BR_W].reshape(rows, BR_W)
    pos = row & (chunk - 1)
    tr = lax.broadcasted_iota(jnp.int32, (chunk, chunk), 0)
    tc = lax.broadcasted_iota(jnp.int32, (chunk, chunk), 1)
    differ = tr ^ tc
    sub8 = lax.broadcasted_iota(jnp.int32, (8, 1), 0)
    width = min(chunk, C_HD)

    def blk(t, i, h):
        return t[i * chunk:(i + 1) * chunk, h * C_HD:(h + 1) * C_HD]

    def boundary(c):
        pieces = []
        for base in range(0, rows, 8):
            if 2 * c <= 8:
                parts = [jnp.broadcast_to(bc[base + m + c - 1:base + m + c, :], (8, BR_W))
                         for m in range(0, 8, 2 * c)]
                piece = parts[-1]
                for idx in range(len(parts) - 2, -1, -1):
                    piece = jnp.where(sub8 < (idx + 1) * 2 * c, parts[idx], piece)
            else:
                at = (base // (2 * c)) * (2 * c) + c - 1
                piece = jnp.broadcast_to(bc[at:at + 1, :], (8, BR_W))
            pieces.append(piece)
        return jnp.concatenate(pieces, axis=0)

    q, v, st, scores = {}, {}, {}, {}
    diag = _sel_dot(hsum_ref[...], q_all * kc, 2, sel_first=False)
    for i, h in units:
        v[i, h] = u_ref[i, :, 2 * BR_W + h * C_HD:2 * BR_W + (h + 1) * C_HD]
        q[i, h] = blk(q_all, i, h)
        st[i, h] = st_scr[i, h]
        scores[i, h] = jnp.where(tr == tc, diag[i * chunk:(i + 1) * chunk, h * C_HD:h * C_HD + width], 0.0)
    c = 1
    while c < chunk:
        w = jnp.exp(-jnp.abs(bc - boundary(c)))
        qz = q_all * w
        kz = kc * w
        here = (tr > tc) & (differ >= c) & (differ < 2 * c)
        for i, h in units:
            scores[i, h] = jnp.where(here, _bdot(blk(qz, i, h), blk(kz, i, h), NT), scores[i, h])
        c *= 2

    for i, h in units:
        sl = slice(h * C_HD, (h + 1) * C_HD)
        o = _bdot(scores[i, h], v[i, h]) + _bdot(q[i, h] * blk(e_b, i, h), st[i, h], NT)
        st_scr[i, h] = st[i, h] * e_last[i * chunk:i * chunk + 1, sl] + _bdot(v[i, h], blk(k_hat, i, h), TN)
        o_scr[i * chunk:(i + 1) * chunk, sl] = o

    o = o_scr[...]
    ms = _sel_dot(hsum_ref[...], o * o, 2, sel_first=False) * (1.0 / C_HD)
    gt = u_ref[:, :, 3 * BR_W:4 * BR_W].reshape(rows, BR_W)
    o_ref[...] = (o * lax.rsqrt(ms + GN_EPS) * gn_ref[...] * _silu(gt)).reshape(bb, chunk, BR_W)

    @pl.when(n == n_chunks - 1)
    def _():
        for i, h in units:
            s_out_ref[i, h] = st_scr[i, h].T


def _hgrn(u, lb, state, layer, gn, hsum, chunk, valid, bb):
    batch, seq, _ = u.shape
    n_chunks = seq // chunk
    assert valid == chunk or n_chunks == 1
    kern = functools.partial(_hgrn_kernel, chunk=chunk, valid=valid, n_chunks=n_chunks, bb=bb)
    return pl.pallas_call(
        kern,
        grid=(batch // bb, n_chunks),
        in_specs=[
            pl.BlockSpec((bb, chunk, C_COLS), lambda b, n: (b, n, U_C_OFF // C_COLS)),
            pl.BlockSpec((1, BR_W), lambda b, n: (0, 0)),
            pl.BlockSpec((None, bb, C_HEADS, C_HD, C_HD), lambda b, n: (layer, b, 0, 0, 0)),
            pl.BlockSpec((1, BR_W), lambda b, n: (0, 0)),
            pl.BlockSpec((BR_W, BR_W), lambda b, n: (0, 0)),
        ],
        out_specs=[
            pl.BlockSpec((bb, chunk, BR_W), lambda b, n: (b, n, 0)),
            pl.BlockSpec((bb, C_HEADS, C_HD, C_HD), lambda b, n: (b, 0, 0, 0)),
        ],
        out_shape=[
            jax.ShapeDtypeStruct((batch, seq, BR_W), F32),
            jax.ShapeDtypeStruct((batch, C_HEADS, C_HD, C_HD), F32),
        ],
        scratch_shapes=[pltpu.VMEM((bb, C_HEADS, C_HD, C_HD), F32), pltpu.VMEM((bb * chunk, BR_W), F32)],
        compiler_params=_cparams(("parallel", "arbitrary")),
        name="hgrn2",
    )(u, lb, state, gn, hsum)


def _merge_kernel(oa_ref, ob_ref, oc_ref, g0_ref, g1_ref, g2_ref, x_ref, wb_ref, wo_ref, o_ref):
    m = _sigmoid(g0_ref[...]) * _bdot(oa_ref[...], wb_ref[0])
    m = m + _sigmoid(g1_ref[...]) * _bdot(ob_ref[...], wb_ref[1])
    m = m + _sigmoid(g2_ref[...]) * _bdot(oc_ref[...], wb_ref[2])
    o_ref[...] = x_ref[...] + _bdot(m, wo_ref[...])


def _merge(oa, ob, oc, u, x, wb, wo, tm):
    n = x.shape[0]
    g_blk = U_G_OFF // D_MODEL
    br = pl.BlockSpec((tm, BR_W), lambda i: (i, 0))
    gate = lambda c: pl.BlockSpec((tm, D_MODEL), lambda i: (i, g_blk + c))
    return pl.pallas_call(
        _merge_kernel,
        grid=(n // tm,),
        in_specs=[br, br, br, gate(0), gate(1), gate(2),
                  pl.BlockSpec((tm, D_MODEL), lambda i: (i, 0)),
                  pl.BlockSpec((3, BR_W, D_MODEL), lambda i: (0, 0, 0)),
                  pl.BlockSpec((D_MODEL, D_MODEL), lambda i: (0, 0))],
        out_specs=pl.BlockSpec((tm, D_MODEL), lambda i: (i, 0)),
        out_shape=jax.ShapeDtypeStruct((n, D_MODEL), F32),
        compiler_params=_cparams(("parallel",)),
        name="merge",
    )(oa, ob, oc, u, u, u, x, wb, wo)


def _ffn_kernel(x_ref, g_ref, wg_ref, wu_ref, wo_ref, o_ref, h_ref):
    @pl.when(pl.program_id(1) == 0)
    def _():
        x = x_ref[...]
        ms = jnp.mean(x * x, axis=-1, keepdims=True)
        h_ref[...] = (x * lax.rsqrt(ms + RMS_EPS) * g_ref[...]).astype(BF16)
        o_ref[...] = x

    h = h_ref[...]
    gt = jnp.dot(h, wg_ref[...], preferred_element_type=F32)
    up = jnp.dot(h, wu_ref[...], preferred_element_type=F32)
    o_ref[...] += _bdot(_silu(gt) * up, wo_ref[...])


def _ffn(x, g, w_in, w_out, tm, tf):
    n = x.shape[0]
    n_ff = D_FF // tf
    return pl.pallas_call(
        _ffn_kernel,
        grid=(n // tm, n_ff),
        in_specs=[
            pl.BlockSpec((tm, D_MODEL), lambda i, j: (i, 0)),
            pl.BlockSpec((1, D_MODEL), lambda i, j: (0, 0)),
            pl.BlockSpec((D_MODEL, tf), lambda i, j: (0, j)),
            pl.BlockSpec((D_MODEL, tf), lambda i, j: (0, n_ff + j)),
            pl.BlockSpec((tf, D_MODEL), lambda i, j: (j, 0)),
        ],
        out_specs=pl.BlockSpec((tm, D_MODEL), lambda i, j: (i, 0)),
        out_shape=jax.ShapeDtypeStruct((n, D_MODEL), F32),
        scratch_shapes=[pltpu.VMEM((tm, D_MODEL), BF16)],
        compiler_params=_cparams(("parallel", "arbitrary")),
        name="ffn",
    )(x, g, w_in, w_in, w_out)


def _final_norm_kernel(x_ref, g_ref, o_ref):
    x = x_ref[...]
    ms = jnp.mean(x * x, axis=-1, keepdims=True)
    o_ref[...] = x * lax.rsqrt(ms + RMS_EPS) * g_ref[...]


def _final_norm(x, g, tm):
    n = x.shape[0]
    return pl.pallas_call(
        _final_norm_kernel,
        grid=(n // tm,),
        in_specs=[pl.BlockSpec((tm, D_MODEL), lambda i: (i, 0)),
                  pl.BlockSpec((1, D_MODEL), lambda i: (0, 0))],
        out_specs=pl.BlockSpec((tm, D_MODEL), lambda i: (i, 0)),
        out_shape=jax.ShapeDtypeStruct((n, D_MODEL), F32),
        compiler_params=_cparams(("parallel",)),
        name="final_norm",
    )(x, g)


def _rope_tables(pos):
    half = B_HD // 2
    inv = ROPE_BASE ** (-jnp.arange(half, dtype=F32) / half)
    ang = pos.astype(F32)[:, None] * inv[None, :]
    cos, sin = jnp.cos(ang), jnp.sin(ang)
    return jnp.concatenate([cos, cos], axis=-1), jnp.concatenate([-sin, sin], axis=-1)


def _row_tile(n, largest=512):
    t = largest
    while t >= 8:
        if n % t == 0:
            return t
        t //= 2
    raise ValueError(f"row count {n} is not a multiple of 8")


def _layer(x, states, layer, cos, sin, p, batch, seq, valid, cfg):
    shift, wkv, ret, hg = states
    tm = _row_tile(x.shape[0])
    u2 = _in_proj(x, p["norm_mix"], p["w_in"], _row_tile(x.shape[0], IN_ROW_TILE), IN_COL_TILE)
    u = u2.reshape(batch, seq, IN_COLS)
    full = lambda c: valid if c >= seq else c
    (ca, sa), (cb, sb), (cc, sc) = cfg["rwkv"], cfg["ret"], cfg["hgrn"]
    oa, shift_new, wkv_new = _rwkv(u, shift, wkv, layer, p, ca, full(ca), sa)
    ob, ret_new = _retention(u, cos, sin, ret, layer, p["ret_gn"], p["hsum128"], cb, full(cb), sb)
    oc, hg_new = _hgrn(u, p["lower"], hg, layer, p["hgrn_gn"], p["hsum128"], cc, full(cc), sc)
    flat = lambda o: o.reshape(batch * seq, BR_W)
    x = _merge(flat(oa), flat(ob), flat(oc), u2, x, p["w_branch"], p["w_out"], tm)
    x = _ffn(x, p["norm_ffn"], p["w_ffn_in"], p["w_ffn_out"], tm, FF_TILE)
    return x, shift_new, wkv_new, ret_new, hg_new


def kernel(x_prompt, x_sample, state_rwkv_shift, state_rwkv_wkv, state_ret, state_hgrn,
           norm_mix, w_in, rwkv_mu, rwkv_w0, rwkv_w2, rwkv_a0, rwkv_a2, rwkv_g2, rwkv_kk,
           rwkv_ka, rwkv_rk, rwkv_ln_w, rwkv_ln_b, ret_gn, hgrn_lb, hgrn_gn, w_branch, w_out,
           norm_ffn, w_ffn_in, w_ffn_out, norm_final):
    depth = w_in.shape[0]
    bp, tp, _ = x_prompt.shape
    bs, ts, _ = x_sample.shape
    assert ts <= SAMPLE_CHUNK
    assert all(tp % c == 0 and bp % s == 0 for c, s in PROMPT_CFG.values())
    assert all(bs % s == 0 for _, s in SAMPLE_CFG.values())
    dt = x_prompt.dtype

    lower = _lower_bounds(hgrn_lb)
    w_in_r = jnp.concatenate(
        [w_in[:, :, A_COLS:A_COLS + B_COLS + C_COLS + GATE_COLS], w_in[:, :, :A_COLS]], axis=-1).astype(BF16)
    w_branch_b = w_branch.astype(BF16)
    w_out_b = w_out.astype(BF16)
    w_ffn_in_b = w_ffn_in.astype(BF16)
    w_ffn_out_b = w_ffn_out.astype(BF16)
    lanes = jnp.arange(BR_W)
    hsum = (lanes[:, None] // A_HD == lanes[None, :] // A_HD).astype(BF16)
    hsum128 = (lanes[:, None] // B_HD == lanes[None, :] // B_HD).astype(BF16)
    row = lambda a: a.reshape(1, -1).astype(F32)

    cos_p, sin_p = _rope_tables(jnp.arange(tp, dtype=jnp.int32))
    cos_s, sin_s = _rope_tables(PAST_LEN + jnp.arange(SAMPLE_CHUNK, dtype=jnp.int32))

    xp = x_prompt.reshape(bp * tp, D_MODEL)
    xs = jnp.pad(x_sample, ((0, 0), (0, SAMPLE_CHUNK - ts), (0, 0))).reshape(bs * SAMPLE_CHUNK, D_MODEL)
    zero_states = (jnp.zeros((1, bp, 1, A_COLS), F32), jnp.zeros((1, bp, A_HEADS, A_HD, A_HD), F32),
                   jnp.zeros((1, bp, B_HEADS, B_HD, B_HD), F32), jnp.zeros((1, bp, C_HEADS, C_HD, C_HD), F32))
    sample_states = (state_rwkv_shift.reshape(depth, bs, 1, A_COLS), state_rwkv_wkv, state_ret, state_hgrn)

    outs_p = [[], [], [], []]
    outs_s = [[], [], [], []]
    for l in range(depth):
        p = {
            "norm_mix": row(norm_mix[l]), "w_in": w_in_r[l], "mu": row(rwkv_mu[l]),
            "w0": row(rwkv_w0[l]), "w2": rwkv_w2[l].astype(BF16), "a0": row(rwkv_a0[l]),
            "a2": rwkv_a2[l].astype(BF16), "g2": rwkv_g2[l].astype(BF16), "kk": row(rwkv_kk[l]),
            "ka": row(rwkv_ka[l]), "rk": row(rwkv_rk[l]), "lnw": row(rwkv_ln_w[l]),
            "lnb": row(rwkv_ln_b[l]), "hsum": hsum, "hsum128": hsum128, "ret_gn": row(ret_gn[l]),
            "lower": lower[l:l + 1], "hgrn_gn": row(hgrn_gn[l]), "w_branch": w_branch_b[l],
            "w_out": w_out_b[l], "norm_ffn": row(norm_ffn[l]), "w_ffn_in": w_ffn_in_b[l],
            "w_ffn_out": w_ffn_out_b[l],
        }
        xp, a1, a2, a3, a4 = _layer(xp, zero_states, 0, cos_p, sin_p, p, bp, tp, tp, PROMPT_CFG)
        xs, b1, b2, b3, b4 = _layer(xs, sample_states, l, cos_s, sin_s, p, bs, SAMPLE_CHUNK, ts, SAMPLE_CFG)
        for acc, val in zip(outs_p, (a1.reshape(bp, A_COLS), a2, a3, a4)):
            acc.append(val.astype(dt))
        for acc, val in zip(outs_s, (b1.reshape(bs, A_COLS), b2, b3, b4)):
            acc.append(val.astype(dt))

    g_fin = row(norm_final)
    y_prompt = _final_norm(xp, g_fin, _row_tile(xp.shape[0])).reshape(bp, tp, D_MODEL)
    y_sample = _final_norm(xs, g_fin, _row_tile(xs.shape[0])).reshape(bs, SAMPLE_CHUNK, D_MODEL)[:, :ts]
    return (y_prompt, y_sample, *[jnp.stack(o) for o in outs_p], *[jnp.stack(o) for o in outs_s])
```

```python
import functools
import math

import jax
import jax.numpy as jnp
from jax import lax
from jax.experimental import pallas as pl
from jax.experimental.pallas import tpu as pltpu

F32 = jnp.float32
BF16 = jnp.bfloat16

D_MODEL = 1024
BR_W = 512
A_HD = 64
A_HEADS = BR_W // A_HD
LORA_W = 64
LORA_A = 64
LORA_G = 128
A_COLS = 3 * BR_W + LORA_W + LORA_A + LORA_G
B_HD = 128
B_HEADS = BR_W // B_HD
C_HD = 128
C_HEADS = BR_W // C_HD
B_COLS = 4 * BR_W
C_COLS = 4 * BR_W
GATE_COLS = 3 * D_MODEL
IN_COLS = A_COLS + B_COLS + C_COLS + GATE_COLS
D_FF = 2816
PAST_LEN = 16384
RMS_EPS = 1e-6
RWKV_GN_EPS = 64e-5
GN_EPS = 1e-5
F_FLOOR = 1e-30
ROPE_BASE = 10000.0

U_B_OFF = 0
U_C_OFF = B_COLS
U_G_OFF = B_COLS + C_COLS
U_A_OFF = B_COLS + C_COLS + GATE_COLS

SAMPLE_CHUNK = 8
PROMPT_CFG = {"rwkv": (64, 4), "ret": (128, 8), "hgrn": (128, 4)}
SAMPLE_CFG = {"rwkv": (SAMPLE_CHUNK, 4), "ret": (SAMPLE_CHUNK, 8), "hgrn": (SAMPLE_CHUNK, 8)}
IN_ROW_TILE = 1024
IN_COL_TILE = 1792
FF_TILE = 1408

VMEM_LIMIT = 56 * 1024 * 1024
LANES = 128

NN = (((1,), (0,)), ((), ()))
NT = (((1,), (1,)), ((), ()))
TN = (((0,), (0,)), ((), ()))


def _bdot(a, b, dims=NN):
    return lax.dot_general(a.astype(BF16), b.astype(BF16), dims, preferred_element_type=F32)


def _sel_dot(sel, x, passes, sel_first=True):
    sel = sel.astype(BF16)
    acc = None
    for _ in range(passes):
        piece = x.astype(BF16)
        ops = (sel, piece) if sel_first else (piece, sel)
        part = lax.dot_general(*ops, NN, preferred_element_type=F32)
        acc = part if acc is None else acc + part
        x = x - piece.astype(F32)
    return acc


def _head_sum(x, same_head, passes):
    return jnp.concatenate([_sel_dot(same_head, x[:, g:g + LANES], passes, sel_first=False)
                            for g in range(0, x.shape[1], LANES)], axis=1)


def _sigmoid(x):
    return 0.5 * jnp.tanh(0.5 * x) + 0.5


def _silu(x):
    return x * _sigmoid(x)


def _cparams(sem):
    return pltpu.CompilerParams(dimension_semantics=sem, vmem_limit_bytes=VMEM_LIMIT)


def _chunk_cumsum(x, chunk):
    rows = x.shape[0]
    group = max(chunk, min(rows, 64))
    assert rows % group == 0
    tr = lax.broadcasted_iota(jnp.int32, (group, group), 0)
    tc = lax.broadcasted_iota(jnp.int32, (group, group), 1)
    sh = chunk.bit_length() - 1
    tri = jnp.where((tr >= tc) & ((tr >> sh) == (tc >> sh)), 1.0, 0.0)
    return jnp.concatenate([_sel_dot(tri, x[r:r + group], 3) for r in range(0, rows, group)], axis=0)


def _last_rows(x, chunk, bb):
    return jnp.concatenate(
        [jnp.broadcast_to(x[(i + 1) * chunk - 1:(i + 1) * chunk, :], (chunk, x.shape[1])) for i in range(bb)],
        axis=0)


def _lower_kernel(lb_ref, o_ref):
    x = lb_ref[...]
    depth = x.shape[0]
    m = x[0:1]
    for l in range(1, depth):
        m = jnp.maximum(m, x[l:l + 1])
    e = jnp.exp(x - m)
    tot = e[0:1]
    for l in range(1, depth):
        tot = tot + e[l:l + 1]
    sm = e / tot
    acc = jnp.zeros_like(m)
    for l in range(depth):
        acc = acc + sm[l:l + 1]
        o_ref[l:l + 1, :] = acc - sm[0:1]


def _lower_bounds(hgrn_lb):
    return pl.pallas_call(
        _lower_kernel, out_shape=jax.ShapeDtypeStruct(hgrn_lb.shape, F32), name="hgrn_lower",
    )(hgrn_lb.astype(F32))


def _in_proj_kernel(x_ref, g_ref, w_ref, o_ref, h_ref):
    @pl.when(pl.program_id(1) == 0)
    def _():
        x = x_ref[...]
        ms = jnp.mean(x * x, axis=-1, keepdims=True)
        h_ref[...] = (x * lax.rsqrt(ms + RMS_EPS) * g_ref[...]).astype(BF16)

    o_ref[...] = jnp.dot(h_ref[...], w_ref[...], preferred_element_type=F32)


def _in_proj(x, g, w, tm, tn):
    n, d = x.shape
    cols = w.shape[1]
    return pl.pallas_call(
        _in_proj_kernel,
        grid=(n // tm, cols // tn),
        in_specs=[
            pl.BlockSpec((tm, d), lambda i, j: (i, 0)),
            pl.BlockSpec((1, d), lambda i, j: (0, 0)),
            pl.BlockSpec((d, tn), lambda i, j: (0, j)),
        ],
        out_specs=pl.BlockSpec((tm, tn), lambda i, j: (i, j)),
        out_shape=jax.ShapeDtypeStruct((n, cols), F32),
        scratch_shapes=[pltpu.VMEM((tm, d), BF16)],
        compiler_params=_cparams(("parallel", "arbitrary")),
        name="in_proj",
    )(x, g, w)


def _rwkv_kernel(u_ref, shift_ref, s0_ref, mu_ref, w0_ref, w2_ref, a0_ref, a2_ref, g2_ref,
                 kk_ref, ka_ref, rk_ref, lnw_ref, lnb_ref, hsum_ref,
                 o_ref, shift_out_ref, s_out_ref, st_scr, prev_scr, o_scr, *, chunk, valid, n_chunks, bb):
    n = pl.program_id(1)
    rows = bb * chunk
    units = [(i, h) for i in range(bb) for h in range(A_HEADS)]

    @pl.when(n == 0)
    def _():
        for i, h in units:
            st_scr[i, h] = s0_ref[i, h].T
        prev_scr[...] = shift_ref[...]

    u = u_ref[...].reshape(rows, A_COLS)
    row = lax.broadcasted_iota(jnp.int32, (rows, 1), 0)
    pos = row & (chunk - 1)
    prev = jnp.concatenate([jnp.broadcast_to(prev_scr[i], (chunk, A_COLS)) for i in range(bb)], axis=0)
    shifted = jnp.where(pos == 0, prev, pltpu.roll(u, 1, axis=0))
    xm = u + mu_ref[...] * (shifted - u)
    for i in range(bb):
        prev_scr[i] = u[(i + 1) * chunk - 1:(i + 1) * chunk, :]

    r = xm[:, 0:BR_W]
    k = xm[:, BR_W:2 * BR_W]
    v = xm[:, 2 * BR_W:3 * BR_W]
    wl = xm[:, 3 * BR_W:3 * BR_W + LORA_W]
    al = xm[:, 3 * BR_W + LORA_W:3 * BR_W + LORA_W + LORA_A]
    gl = xm[:, 3 * BR_W + LORA_W + LORA_A:A_COLS]

    z = w0_ref[...] + _bdot(jnp.tanh(wl), w2_ref[...])
    ld = -math.exp(-0.5) * _sigmoid(z)
    a = _sigmoid(a0_ref[...] + _bdot(al, a2_ref[...]))
    g = _bdot(_sigmoid(gl), g2_ref[...])
    kk = k * kk_ref[...]
    ss = _head_sum(kk * kk, hsum_ref[...], 2)
    kk = kk * lax.rsqrt(jnp.maximum(ss, 1e-24))
    k = k * (1.0 + (a - 1.0) * ka_ref[...])
    if valid < chunk:
        ok = pos < valid
        ld = jnp.where(ok, ld, 0.0)
        kk = jnp.where(ok, kk, 0.0)
        k = jnp.where(ok, k, 0.0)
    bv = kk * a

    cl = _chunk_cumsum(ld, chunk)
    cl_last = _last_rows(cl, chunk, bb)
    e_ncl = jnp.exp(-cl)
    e_rem = jnp.exp(cl_last - cl)
    alpha_t = -kk * jnp.exp(cl - ld)
    r_t = r * jnp.exp(cl)
    b_bar = bv * e_ncl
    k_bar = k * e_ncl
    b_hat = bv * e_rem
    k_hat = k * e_rem
    e_last = jnp.exp(cl_last)
    rkk = r * k * rk_ref[...]

    qr = lax.broadcasted_iota(jnp.int32, (2 * chunk, 2 * chunk), 0)
    qc = lax.broadcasted_iota(jnp.int32, (2 * chunk, 2 * chunk), 1)
    qt, qs = qr & (chunk - 1), qc & (chunk - 1)
    keep = (qt > qs) | ((qr >= chunk) & (qt == qs))

    def blk(t, i, h):
        return t[i * chunk:(i + 1) * chunk, h * A_HD:(h + 1) * A_HD]

    aa, st0, vv, x, p = {}, {}, {}, {}, {}
    for i, h in units:
        lhs = jnp.concatenate([blk(alpha_t, i, h), blk(r_t, i, h)], axis=0)
        rhs = jnp.concatenate([blk(b_bar, i, h), blk(k_bar, i, h)], axis=0)
        aa[i, h] = jnp.where(keep, _bdot(lhs, rhs, NT), 0.0)
        st0[i, h] = st_scr[i, h]
        vv[i, h] = blk(v, i, h)
    for i, h in units:
        lhs = jnp.concatenate([blk(alpha_t, i, h), aa[i, h][0:chunk, chunk:2 * chunk]], axis=1)
        x[i, h] = _bdot(lhs, jnp.concatenate([st0[i, h], vv[i, h]], axis=0))
        p[i, h] = aa[i, h][0:chunk, 0:chunk]
    for i, h in units:
        x[i, h] = x[i, h] + _bdot(p[i, h], x[i, h])
    span = 2
    while span < chunk:
        for i, h in units:
            p[i, h] = _bdot(p[i, h], p[i, h])
        for i, h in units:
            x[i, h] = x[i, h] + _bdot(p[i, h], x[i, h])
        span *= 2
    for i, h in units:
        sl = slice(h * A_HD, (h + 1) * A_HD)
        lhs = jnp.concatenate([blk(r_t, i, h), aa[i, h][chunk:2 * chunk, :]], axis=1)
        o = _bdot(lhs, jnp.concatenate([st0[i, h], x[i, h], vv[i, h]], axis=0))
        upd = _bdot(jnp.concatenate([blk(b_hat, i, h), blk(k_hat, i, h)], axis=0),
                    jnp.concatenate([x[i, h], vv[i, h]], axis=0), TN)
        dec = jnp.broadcast_to(e_last[i * chunk:i * chunk + 1, sl], (A_HD, A_HD)).T
        st_scr[i, h] = st0[i, h] * dec + upd
        o_scr[i * chunk:(i + 1) * chunk, sl] = o

    hsum = hsum_ref[...]
    o = o_scr[...]
    d = o - _head_sum(o, hsum, 2) * (1.0 / A_HD)
    var = _head_sum(d * d, hsum, 1) * (1.0 / A_HD)
    o = d * lax.rsqrt(var + RWKV_GN_EPS) * lnw_ref[...] + lnb_ref[...]
    o = o + _head_sum(rkk, hsum, 1) * v
    o_ref[...] = (o * g).reshape(bb, chunk, BR_W)

    @pl.when(n == n_chunks - 1)
    def _():
        last = (valid - 1) if n_chunks == 1 else (chunk - 1)
        for i in range(bb):
            shift_out_ref[i] = u[i * chunk + last:i * chunk + last + 1, :]
        for i, h in units:
            s_out_ref[i, h] = st_scr[i, h].T


def _rwkv(u, shift, wkv, layer, p, chunk, valid, bb):
    batch, seq, _ = u.shape
    n_chunks = seq // chunk
    assert valid == chunk or n_chunks == 1
    a_blk = U_A_OFF // A_COLS
    vec = lambda w: pl.BlockSpec((1, w), lambda b, n: (0, 0))
    full = lambda s: pl.BlockSpec(s, lambda b, n: (0,) * len(s))
    kern = functools.partial(_rwkv_kernel, chunk=chunk, valid=valid, n_chunks=n_chunks, bb=bb)
    return pl.pallas_call(
        kern,
        grid=(batch // bb, n_chunks),
        in_specs=[
            pl.BlockSpec((bb, chunk, A_COLS), lambda b, n: (b, n, a_blk)),
            pl.BlockSpec((None, bb, 1, A_COLS), lambda b, n: (layer, b, 0, 0)),
            pl.BlockSpec((None, bb, A_HEADS, A_HD, A_HD), lambda b, n: (layer, b, 0, 0, 0)),
            vec(A_COLS), vec(BR_W), full((LORA_W, BR_W)), vec(BR_W), full((LORA_A, BR_W)),
            full((LORA_G, BR_W)), vec(BR_W), vec(BR_W), vec(BR_W), vec(BR_W), vec(BR_W),
            pl.BlockSpec((LANES, LANES), lambda b, n: (0, 0)),
        ],
        out_specs=[
            pl.BlockSpec((bb, chunk, BR_W), lambda b, n: (b, n, 0)),
            pl.BlockSpec((bb, 1, A_COLS), lambda b, n: (b, 0, 0)),
            pl.BlockSpec((bb, A_HEADS, A_HD, A_HD), lambda b, n: (b, 0, 0, 0)),
        ],
        out_shape=[
            jax.ShapeDtypeStruct((batch, seq, BR_W), F32),
            jax.ShapeDtypeStruct((batch, 1, A_COLS), F32),
            jax.ShapeDtypeStruct((batch, A_HEADS, A_HD, A_HD), F32),
        ],
        scratch_shapes=[pltpu.VMEM((bb, A_HEADS, A_HD, A_HD), F32), pltpu.VMEM((bb, 1, A_COLS), F32),
                        pltpu.VMEM((bb * chunk, BR_W), F32)],
        compiler_params=_cparams(("parallel", "arbitrary")),
        name="rwkv7",
    )(u, shift, wkv, p["mu"], p["w0"], p["w2"], p["a0"], p["a2"], p["g2"], p["kk"], p["ka"],
      p["rk"], p["lnw"], p["lnb"], p["hsum"])


def _ret_kernel(u_ref, cos_ref, sin_ref, s0_ref, gn_ref, hsum_ref, o_ref, s_out_ref, s_scr, o_scr,
                *, chunk, valid, n_chunks, bb):
    n = pl.program_id(1)
    units = [(i, h) for i in range(bb) for h in range(B_HEADS)]

    @pl.when(n == 0)
    def _():
        s_scr[...] = s0_ref[...]

    cos = cos_ref[...]
    sin = sin_ref[...]
    tr = lax.broadcasted_iota(jnp.int32, (chunk, chunk), 0)
    tc = lax.broadcasted_iota(jnp.int32, (chunk, chunk), 1)
    diff = (tr - tc).astype(F32)
    causal = tr >= tc
    pos = lax.broadcasted_iota(jnp.int32, (chunk, 1), 0)
    posf = pos.astype(F32)
    lgs = [math.log(1.0 - 2.0 ** (-5.0 - h)) for h in range(B_HEADS)]
    intra = [jnp.where(causal, jnp.exp(lg * jnp.maximum(diff, 0.0)), 0.0) for lg in lgs]
    q_dec = [jnp.exp(lg * (posf + 1.0)) for lg in lgs]
    k_dec = [jnp.where(pos < valid, jnp.exp(lg * (valid - 1.0 - posf)), 0.0) for lg in lgs]
    s_dec = [math.exp(lg * valid) for lg in lgs]

    q, k, v, s0, sc = {}, {}, {}, {}, {}
    for i, h in units:
        ui = u_ref[i]
        qh = ui[:, h * B_HD:(h + 1) * B_HD]
        kh = ui[:, BR_W + h * B_HD:BR_W + (h + 1) * B_HD]
        q[i, h] = qh * cos + pltpu.roll(qh, B_HD // 2, axis=1) * sin
        k[i, h] = (kh * cos + pltpu.roll(kh, B_HD // 2, axis=1) * sin) * (B_HD ** -0.5)
        v[i, h] = ui[:, 2 * BR_W + h * B_HD:2 * BR_W + (h + 1) * B_HD]
        s0[i, h] = s_scr[i, h]
    for i, h in units:
        sc[i, h] = _bdot(q[i, h], k[i, h], NT) * intra[h]
    for i, h in units:
        sl = slice(h * B_HD, (h + 1) * B_HD)
        o = _bdot(sc[i, h], v[i, h]) + _bdot(q[i, h] * q_dec[h], s0[i, h])
        s_scr[i, h] = s0[i, h] * s_dec[h] + _bdot(k[i, h] * k_dec[h], v[i, h], TN)
        o_scr[i * chunk:(i + 1) * chunk, sl] = o

    hsum = hsum_ref[...]
    o = o_scr[...]
    d = o - _head_sum(o, hsum, 2) * (1.0 / B_HD)
    var = _head_sum(d * d, hsum, 1) * (1.0 / B_HD)
    gt = u_ref[:, :, 3 * BR_W:4 * BR_W].reshape(bb * chunk, BR_W)
    o_ref[...] = (d * lax.rsqrt(var + GN_EPS) * gn_ref[...] * _silu(gt)).reshape(bb, chunk, BR_W)

    @pl.when(n == n_chunks - 1)
    def _():
        s_out_ref[...] = s_scr[...]


def _retention(u, cos, sin, state, layer, gn, hsum, chunk, valid, bb):
    batch, seq, _ = u.shape
    n_chunks = seq // chunk
    assert valid == chunk or n_chunks == 1
    kern = functools.partial(_ret_kernel, chunk=chunk, valid=valid, n_chunks=n_chunks, bb=bb)
    return pl.pallas_call(
        kern,
        grid=(batch // bb, n_chunks),
        in_specs=[
            pl.BlockSpec((bb, chunk, B_COLS), lambda b, n: (b, n, U_B_OFF // B_COLS)),
            pl.BlockSpec((chunk, B_HD), lambda b, n: (n, 0)),
            pl.BlockSpec((chunk, B_HD), lambda b, n: (n, 0)),
            pl.BlockSpec((None, bb, B_HEADS, B_HD, B_HD), lambda b, n: (layer, b, 0, 0, 0)),
            pl.BlockSpec((1, BR_W), lambda b, n: (0, 0)),
            pl.BlockSpec((LANES, LANES), lambda b, n: (0, 0)),
        ],
        out_specs=[
            pl.BlockSpec((bb, chunk, BR_W), lambda b, n: (b, n, 0)),
            pl.BlockSpec((bb, B_HEADS, B_HD, B_HD), lambda b, n: (b, 0, 0, 0)),
        ],
        out_shape=[
            jax.ShapeDtypeStruct((batch, seq, BR_W), F32),
            jax.ShapeDtypeStruct((batch, B_HEADS, B_HD, B_HD), F32),
        ],
        scratch_shapes=[pltpu.VMEM((bb, B_HEADS, B_HD, B_HD), F32), pltpu.VMEM((bb * chunk, BR_W), F32)],
        compiler_params=_cparams(("parallel", "arbitrary")),
        name="retention",
    )(u, cos, sin, state, gn, hsum)


def _hgrn_kernel(u_ref, lb_ref, s0_ref, gn_ref, hsum_ref, o_ref, s_out_ref, st_scr, o_scr,
                 *, chunk, valid, n_chunks, bb):
    n = pl.program_id(1)
    rows = bb * chunk
    units = [(i, h) for i in range(bb) for h in range(C_HEADS)]

    @pl.when(n == 0)
    def _():
        for i, h in units:
            st_scr[i, h] = s0_ref[i, h].T

    lb = lb_ref[...]
    row = lax.broadcasted_iota(jnp.int32, (rows, 1), 0)
    sig_f = _sigmoid(u_ref[:, :, BR_W:2 * BR_W].reshape(rows, BR_W))
    lf = jnp.log(jnp.maximum(lb + (1.0 - lb) * sig_f, F_FLOOR))
    kc = (1.0 - lb) * (1.0 - sig_f)
    if valid < chunk:
        ok = (row & (chunk - 1)) < valid
        lf = jnp.where(ok, lf, 0.0)
        kc = jnp.where(ok, kc, 0.0)
    bc = _chunk_cumsum(lf, chunk)
    b_last = _last_rows(bc, chunk, bb)
    e_b = jnp.exp(bc)
    e_last = jnp.exp(b_last)
    k_hat = kc * jnp.exp(b_last - bc)
    q_all = u_ref[:, :, 0:BR_W].reshape(rows, BR_W)
    tr =lax.broadcasted_iota(jnp.int32, (chunk, chunk), 0)
    tc = lax.broadcasted_iota(jnp.int32, (chunk, chunk), 1)
    differ = tr ^ tc
    sub8 = lax.broadcasted_iota(jnp.int32, (8, 1), 0)
    width = min(chunk, C_HD)

    def blk(t, i, h):
        return t[i * chunk:(i + 1) * chunk, h * C_HD:(h + 1) * C_HD]

    def boundary(c):
        pieces = []
        for base in range(0, rows, 8):
            if 2 * c <= 8:
                parts = [jnp.broadcast_to(bc[base + m + c - 1:base + m + c, :], (8, BR_W))
                         for m in range(0, 8, 2 * c)]
                piece = parts[-1]
                for idx in range(len(parts) - 2, -1, -1):
                    piece = jnp.where(sub8 < (idx + 1) * 2 * c, parts[idx], piece)
            else:
                at = (base // (2 * c)) * (2 * c) + c - 1
                piece = jnp.broadcast_to(bc[at:at + 1, :], (8, BR_W))
            pieces.append(piece)
        return jnp.concatenate(pieces, axis=0)

    q, v, st, scores = {}, {}, {}, {}
    diag = _head_sum(q_all * kc, hsum_ref[...], 2)
    for i, h in units:
        v[i, h] = u_ref[i, :, 2 * BR_W + h * C_HD:2 * BR_W + (h + 1) * C_HD]
        q[i, h] = blk(q_all, i, h)
        st[i, h] = st_scr[i, h]
        scores[i, h] = jnp.where(tr == tc, diag[i * chunk:(i + 1) * chunk, h * C_HD:h * C_HD + width], 0.0)
    c = 1
    while c < chunk:
        w = jnp.exp(-jnp.abs(bc - boundary(c)))
        qz = q_all * w
        kz = kc * w
        here = (tr > tc) & (differ >= c) & (differ < 2 * c)
        for i, h in units:
            scores[i, h] = jnp.where(here, _bdot(blk(qz, i, h), blk(kz, i, h), NT), scores[i, h])
        c *= 2

    for i, h in units:
        sl = slice(h * C_HD, (h + 1) * C_HD)
        o = _bdot(scores[i, h], v[i, h]) + _bdot(q[i, h] * blk(e_b, i, h), st[i, h], NT)
        st_scr[i, h] = st[i, h] * e_last[i * chunk:i * chunk + 1, sl] + _bdot(v[i, h], blk(k_hat, i, h), TN)
        o_scr[i * chunk:(i + 1) * chunk, sl] = o

    o = o_scr[...]
    ms = _head_sum(o * o, hsum_ref[...], 1) * (1.0 / C_HD)
    gt = u_ref[:, :, 3 * BR_W:4 * BR_W].reshape(rows, BR_W)
    o_ref[...] = (o * lax.rsqrt(ms + GN_EPS) * gn_ref[...] * _silu(gt)).reshape(bb, chunk, BR_W)

    @pl.when(n == n_chunks - 1)
    def _():
        for i, h in units:
            s_out_ref[i, h] = st_scr[i, h].T


def _hgrn(u, lb, state, layer, gn, hsum, chunk, valid, bb):
    batch, seq, _ = u.shape
    n_chunks = seq // chunk
    assert valid == chunk or n_chunks == 1
    kern = functools.partial(_hgrn_kernel, chunk=chunk, valid=valid, n_chunks=n_chunks, bb=bb)
    return pl.pallas_call(
        kern,
        grid=(batch // bb, n_chunks),
        in_specs=[
            pl.BlockSpec((bb, chunk, C_COLS), lambda b, n: (b, n, U_C_OFF // C_COLS)),
            pl.BlockSpec((1, BR_W), lambda b, n: (0, 0)),
            pl.BlockSpec((None, bb, C_HEADS, C_HD, C_HD), lambda b, n: (layer, b, 0, 0, 0)),
            pl.BlockSpec((1, BR_W), lambda b, n: (0, 0)),
            pl.BlockSpec((LANES, LANES), lambda b, n: (0, 0)),
        ],
        out_specs=[
            pl.BlockSpec((bb, chunk, BR_W), lambda b, n: (b, n, 0)),
            pl.BlockSpec((bb, C_HEADS, C_HD, C_HD), lambda b, n: (b, 0, 0, 0)),
        ],
        out_shape=[
            jax.ShapeDtypeStruct((batch, seq, BR_W), F32),
            jax.ShapeDtypeStruct((batch, C_HEADS, C_HD, C_HD), F32),
        ],
        scratch_shapes=[pltpu.VMEM((bb, C_HEADS, C_HD, C_HD), F32), pltpu.VMEM((bb * chunk, BR_W), F32)],
        compiler_params=_cparams(("parallel", "arbitrary")),
        name="hgrn2",
    )(u, lb, state, gn, hsum)


def _merge_kernel(oa_ref, ob_ref, oc_ref, g0_ref, g1_ref, g2_ref, x_ref, wb_ref, wo_ref, o_ref):
    m = _sigmoid(g0_ref[...]) * _bdot(oa_ref[...], wb_ref[0])
    m = m + _sigmoid(g1_ref[...]) * _bdot(ob_ref[...], wb_ref[1])
    m = m + _sigmoid(g2_ref[...]) * _bdot(oc_ref[...], wb_ref[2])
    o_ref[...] = x_ref[...] + _bdot(m, wo_ref[...])


def _merge(oa, ob, oc, u, x, wb, wo, tm):
    n = x.shape[0]
    g_blk = U_G_OFF // D_MODEL
    br = pl.BlockSpec((tm, BR_W), lambda i: (i, 0))
    gate = lambda c: pl.BlockSpec((tm, D_MODEL), lambda i: (i, g_blk + c))
    return pl.pallas_call(
        _merge_kernel,
        grid=(n // tm,),
        in_specs=[br, br, br, gate(0), gate(1), gate(2),
                  pl.BlockSpec((tm, D_MODEL), lambda i: (i, 0)),
                  pl.BlockSpec((3, BR_W, D_MODEL), lambda i: (0, 0, 0)),
                  pl.BlockSpec((D_MODEL, D_MODEL), lambda i: (0, 0))],
        out_specs=pl.BlockSpec((tm, D_MODEL), lambda i: (i, 0)),
        out_shape=jax.ShapeDtypeStruct((n, D_MODEL), F32),
        compiler_params=_cparams(("parallel",)),
        name="merge",
    )(oa, ob, oc, u, u, u, x, wb, wo)


def _ffn_kernel(x_ref, g_ref, wg_ref, wu_ref, wo_ref, gfin_ref, o_ref, h_ref, *, n_ff, final):
    @pl.when(pl.program_id(1) == 0)
    def _():
        x = x_ref[...]
        ms = jnp.mean(x * x, axis=-1, keepdims=True)
        h_ref[...] = (x * lax.rsqrt(ms + RMS_EPS) * g_ref[...]).astype(BF16)
        o_ref[...] = x

    h = h_ref[...]
    gt = jnp.dot(h, wg_ref[...], preferred_element_type=F32)
    up = jnp.dot(h, wu_ref[...], preferred_element_type=F32)
    o_ref[...] += _bdot(_silu(gt) * up, wo_ref[...])

    if final:
        @pl.when(pl.program_id(1) == n_ff - 1)
        def _():
            x = o_ref[...]
            ms = jnp.mean(x * x, axis=-1, keepdims=True)
            o_ref[...] = x * lax.rsqrt(ms + RMS_EPS) * gfin_ref[...]


def _ffn(x, g, w_in, w_out, g_final, final, tm, tf):
    n = x.shape[0]
    n_ff = D_FF // tf
    return pl.pallas_call(
        functools.partial(_ffn_kernel, n_ff=n_ff, final=final),
        grid=(n // tm, n_ff),
        in_specs=[
            pl.BlockSpec((tm, D_MODEL), lambda i, j: (i, 0)),
            pl.BlockSpec((1, D_MODEL), lambda i, j: (0, 0)),
            pl.BlockSpec((D_MODEL, tf), lambda i, j: (0, j)),
            pl.BlockSpec((D_MODEL, tf), lambda i, j: (0, n_ff + j)),
            pl.BlockSpec((tf, D_MODEL), lambda i, j: (j, 0)),
            pl.BlockSpec((1, D_MODEL), lambda i, j: (0, 0)),
        ],
        out_specs=pl.BlockSpec((tm, D_MODEL), lambda i, j: (i, 0)),
        out_shape=jax.ShapeDtypeStruct((n, D_MODEL), F32),
        scratch_shapes=[pltpu.VMEM((tm, D_MODEL), BF16)],
        compiler_params=_cparams(("parallel", "arbitrary")),
        name="ffn",
    )(x, g, w_in, w_in, w_out, g_final)


def _rope_tables(pos):
    half = B_HD // 2
    inv = ROPE_BASE ** (-jnp.arange(half, dtype=F32) / half)
    ang = pos.astype(F32)[:, None] * inv[None, :]
    cos, sin = jnp.cos(ang), jnp.sin(ang)
    return jnp.concatenate([cos, cos], axis=-1), jnp.concatenate([-sin, sin], axis=-1)


def _row_tile(n, largest=512):
    t = largest
    while t >= 8:
        if n % t == 0:
            return t
        t //= 2
    raise ValueError(f"row count {n} is not a multiple of 8")


def _layer(x, states, layer, cos, sin, p, batch, seq, valid, cfg):
    shift, wkv, ret, hg = states
    tm = _row_tile(x.shape[0])
    u2 = _in_proj(x, p["norm_mix"], p["w_in"], _row_tile(x.shape[0], IN_ROW_TILE), IN_COL_TILE)
    u = u2.reshape(batch, seq, IN_COLS)
    full = lambda c: valid if c >= seq else c
    (ca, sa), (cb, sb), (cc, sc) = cfg["rwkv"], cfg["ret"], cfg["hgrn"]
    oa, shift_new, wkv_new = _rwkv(u, shift, wkv, layer, p, ca, full(ca), sa)
    ob, ret_new = _retention(u, cos, sin, ret, layer, p["ret_gn"], p["hsum128"], cb, full(cb), sb)
    oc, hg_new = _hgrn(u, p["lower"], hg, layer, p["hgrn_gn"], p["hsum128"], cc, full(cc), sc)
    flat = lambda o: o.reshape(batch * seq, BR_W)
    x = _merge(flat(oa), flat(ob), flat(oc), u2, x, p["w_branch"], p["w_out"], tm)
    x = _ffn(x, p["norm_ffn"], p["w_ffn_in"], p["w_ffn_out"], p["norm_final"], p["last"], tm, FF_TILE)
    return x, shift_new, wkv_new, ret_new, hg_new


def kernel(x_prompt, x_sample, state_rwkv_shift, state_rwkv_wkv, state_ret, state_hgrn,
           norm_mix, w_in, rwkv_mu, rwkv_w0, rwkv_w2, rwkv_a0, rwkv_a2, rwkv_g2, rwkv_kk,
           rwkv_ka, rwkv_rk, rwkv_ln_w, rwkv_ln_b, ret_gn, hgrn_lb, hgrn_gn, w_branch, w_out,
           norm_ffn, w_ffn_in, w_ffn_out, norm_final):
    depth = w_in.shape[0]
    bp, tp, _ = x_prompt.shape
    bs, ts, _ = x_sample.shape
    assert ts <= SAMPLE_CHUNK
    assert all(tp % c == 0 and bp % s == 0 for c, s in PROMPT_CFG.values())
    assert all(bs % s == 0 for _, s in SAMPLE_CFG.values())
    dt = x_prompt.dtype

    lower = _lower_bounds(hgrn_lb)
    w_in_r = jnp.concatenate(
        [w_in[:, :, A_COLS:A_COLS + B_COLS + C_COLS + GATE_COLS], w_in[:, :, :A_COLS]], axis=-1).astype(BF16)
    w_branch_b = w_branch.astype(BF16)
    w_out_b = w_out.astype(BF16)
    w_ffn_in_b = w_ffn_in.astype(BF16)
    w_ffn_out_b = w_ffn_out.astype(BF16)
    lanes = jnp.arange(BR_W)
    hsum = (lanes[:, None] // A_HD == lanes[None, :] // A_HD).astype(BF16)
    hsum128 = (lanes[:, None] // B_HD == lanes[None, :] // B_HD).astype(BF16)
    row = lambda a: a.reshape(1, -1).astype(F32)

    cos_p, sin_p = _rope_tables(jnp.arange(tp, dtype=jnp.int32))
    cos_s, sin_s = _rope_tables(PAST_LEN + jnp.arange(SAMPLE_CHUNK, dtype=jnp.int32))

    xp = x_prompt.reshape(bp * tp, D_MODEL)
    xs = jnp.pad(x_sample, ((0, 0), (0, SAMPLE_CHUNK - ts), (0, 0))).reshape(bs * SAMPLE_CHUNK, D_MODEL)
    zero_states = (jnp.zeros((1, bp, 1, A_COLS), F32), jnp.zeros((1, bp, A_HEADS, A_HD, A_HD), F32),
                   jnp.zeros((1, bp, B_HEADS, B_HD, B_HD), F32), jnp.zeros((1, bp, C_HEADS, C_HD, C_HD), F32))
    sample_states = (state_rwkv_shift.reshape(depth, bs, 1, A_COLS), state_rwkv_wkv, state_ret, state_hgrn)

    outs_p = [[], [], [], []]
    outs_s = [[], [], [], []]
    for l in range(depth):
        p = {
            "norm_mix": row(norm_mix[l]), "w_in": w_in_r[l], "mu": row(rwkv_mu[l]),
            "w0": row(rwkv_w0[l]), "w2": rwkv_w2[l].astype(BF16), "a0": row(rwkv_a0[l]),
            "a2": rwkv_a2[l].astype(BF16), "g2": rwkv_g2[l].astype(BF16), "kk": row(rwkv_kk[l]),
            "ka": row(rwkv_ka[l]), "rk": row(rwkv_rk[l]), "lnw": row(rwkv_ln_w[l]),
            "lnb": row(rwkv_ln_b[l]), "hsum": hsum, "hsum128": hsum128, "ret_gn": row(ret_gn[l]),
            "lower": lower[l:l + 1], "hgrn_gn": row(hgrn_gn[l]), "w_branch": w_branch_b[l],
            "w_out": w_out_b[l], "norm_ffn": row(norm_ffn[l]), "w_ffn_in": w_ffn_in_b[l],
            "w_ffn_out": w_ffn_out_b[l], "norm_final": row(norm_final), "last": l == depth - 1,
        }
        xp, a1, a2, a3, a4 = _layer(xp, zero_states, 0, cos_p, sin_p, p, bp, tp, tp, PROMPT_CFG)
        xs, b1, b2, b3, b4 = _layer(xs, sample_states, l, cos_s, sin_s, p, bs, SAMPLE_CHUNK, ts, SAMPLE_CFG)
        for acc, val in zip(outs_p, (a1.reshape(bp, A_COLS), a2, a3, a4)):
            acc.append(val.astype(dt))
        for acc, val in zip(outs_s, (b1.reshape(bs, A_COLS), b2, b3, b4)):
            acc.append(val.astype(dt))

    y_prompt = xp.reshape(bp, tp, D_MODEL)
    y_sample = xs.reshape(bs, SAMPLE_CHUNK, D_MODEL)[:, :ts]
    return (y_prompt, y_sample, *[jnp.stack(o) for o in outs_p], *[jnp.stack(o) for o in outs_s])
```

```python
import functools
import math

import jax
import jax.numpy as jnp
from jax import lax
from jax.experimental import pallas as pl
from jax.experimental.pallas import tpu as pltpu

F32 = jnp.float32
BF16 = jnp.bfloat16

D_MODEL = 1024
BR_W = 512
A_HD = 64
A_HEADS = BR_W // A_HD
LORA_W = 64
LORA_A = 64
LORA_G = 128
A_COLS = 3 * BR_W + LORA_W + LORA_A + LORA_G
B_HD = 128
B_HEADS = BR_W // B_HD
C_HD = 128
C_HEADS = BR_W // C_HD
B_COLS = 4 * BR_W
C_COLS = 4 * BR_W
GATE_COLS = 3 * D_MODEL
IN_COLS = A_COLS + B_COLS + C_COLS + GATE_COLS
D_FF = 2816
PAST_LEN = 16384
RMS_EPS = 1e-6
RWKV_GN_EPS = 64e-5
GN_EPS = 1e-5
F_FLOOR = 1e-30
ROPE_BASE = 10000.0

U_B_OFF = 0
U_C_OFF = B_COLS
U_G_OFF = B_COLS + C_COLS
U_A_OFF = B_COLS + C_COLS + GATE_COLS

SAMPLE_CHUNK = 8
PROMPT_CFG = {"rwkv": (64, 4), "ret": (128, 8), "hgrn": (128, 4)}
SAMPLE_CFG = {"rwkv": (SAMPLE_CHUNK, 4), "ret": (SAMPLE_CHUNK, 16), "hgrn": (SAMPLE_CHUNK, 16)}
IN_ROW_TILE = 1024
IN_COL_TILE = 1792
FF_TILE = 1408

VMEM_LIMIT = 56 * 1024 * 1024
LANES = 128

NN = (((1,), (0,)), ((), ()))
NT = (((1,), (1,)), ((), ()))
TN = (((0,), (0,)), ((), ()))


def _bdot(a, b, dims=NN):
    return lax.dot_general(a.astype(BF16), b.astype(BF16), dims, preferred_element_type=F32)


def _sel_dot(sel, x, passes, sel_first=True):
    sel = sel.astype(BF16)
    acc = None
    for _ in range(passes):
        piece = x.astype(BF16)
        ops = (sel, piece) if sel_first else (piece, sel)
        part = lax.dot_general(*ops, NN, preferred_element_type=F32)
        acc = part if acc is None else acc + part
        x = x - piece.astype(F32)
    return acc


def _head_sum(x, same_head, passes):
    return jnp.concatenate([_sel_dot(same_head, x[:, g:g + LANES], passes, sel_first=False)
                            for g in range(0, x.shape[1], LANES)], axis=1)


def _sigmoid(x):
    return 0.5 * jnp.tanh(0.5 * x) + 0.5


def _silu(x):
    return x * _sigmoid(x)


def _cparams(sem):
    return pltpu.CompilerParams(dimension_semantics=sem, vmem_limit_bytes=VMEM_LIMIT)


def _chunk_cumsum(x, chunk):
    rows = x.shape[0]
    group = max(chunk, min(rows, 64))
    assert rows % group == 0
    tr = lax.broadcasted_iota(jnp.int32, (group, group), 0)
    tc = lax.broadcasted_iota(jnp.int32, (group, group), 1)
    sh = chunk.bit_length() - 1
    tri = jnp.where((tr >= tc) & ((tr >> sh) == (tc >> sh)), 1.0, 0.0)
    return jnp.concatenate([_sel_dot(tri, x[r:r + group], 3) for r in range(0, rows, group)], axis=0)


def _last_rows(x, chunk, bb):
    return jnp.concatenate(
        [jnp.broadcast_to(x[(i + 1) * chunk - 1:(i + 1) * chunk, :], (chunk, x.shape[1])) for i in range(bb)],
        axis=0)


def _lower_kernel(lb_ref, o_ref):
    x = lb_ref[...]
    depth = x.shape[0]
    m = x[0:1]
    for l in range(1, depth):
        m = jnp.maximum(m, x[l:l + 1])
    e = jnp.exp(x - m)
    tot = e[0:1]
    for l in range(1, depth):
        tot = tot + e[l:l + 1]
    sm = e / tot
    acc = jnp.zeros_like(m)
    for l in range(depth):
        acc = acc + sm[l:l + 1]
        o_ref[l:l + 1, :] = acc - sm[0:1]


def _lower_bounds(hgrn_lb):
    return pl.pallas_call(
        _lower_kernel, out_shape=jax.ShapeDtypeStruct(hgrn_lb.shape, F32), name="hgrn_lower",
    )(hgrn_lb.astype(F32))


def _in_proj_kernel(x_ref, g_ref, w_ref, o_ref, h_ref):
    @pl.when(pl.program_id(1) == 0)
    def _():
        x = x_ref[...]
        ms = jnp.mean(x * x, axis=-1, keepdims=True)
        h_ref[...] = (x * lax.rsqrt(ms + RMS_EPS) * g_ref[...]).astype(BF16)

    o_ref[...] = jnp.dot(h_ref[...], w_ref[...], preferred_element_type=F32)


def _in_proj(x, g, w, layer, tm, tn):
    n, d = x.shape
    cols = w.shape[2]
    return pl.pallas_call(
        _in_proj_kernel,
        grid=(n // tm, cols // tn),
        in_specs=[
            pl.BlockSpec((tm, d), lambda i, j: (i, 0)),
            pl.BlockSpec((1, d), lambda i, j: (0, 0)),
            pl.BlockSpec((None, d, tn), lambda i, j: (layer, 0, j)),
        ],
        out_specs=pl.BlockSpec((tm, tn), lambda i, j: (i, j)),
        out_shape=jax.ShapeDtypeStruct((n, cols), F32),
        scratch_shapes=[pltpu.VMEM((tm, d), BF16)],
        compiler_params=_cparams(("parallel", "arbitrary")),
        name="in_proj",
    )(x, g, w)


def _rwkv_kernel(u_ref, shift_ref, s0_ref, mu_ref, w0_ref, w2_ref, a0_ref, a2_ref, g2_ref,
                 kk_ref, ka_ref, rk_ref, lnw_ref, lnb_ref, hsum_ref, stack_ref,
                 o_ref, shift_out_ref, s_out_ref, st_scr, prev_scr, o_scr, *, chunk, valid, n_chunks, bb):
    n = pl.program_id(1)
    rows = bb * chunk
    units = [(i, h) for i in range(bb) for h in range(A_HEADS)]

    @pl.when(n == 0)
    def _():
        for i, h in units:
            st_scr[i, h] = s0_ref[i, h].T
        prev_scr[...] = shift_ref[...]

    u = u_ref[...].reshape(rows, A_COLS)
    row = lax.broadcasted_iota(jnp.int32, (rows, 1), 0)
    pos = row & (chunk - 1)
    prev = jnp.concatenate([jnp.broadcast_to(prev_scr[i], (chunk, A_COLS)) for i in range(bb)], axis=0)
    shifted = jnp.where(pos == 0, prev, pltpu.roll(u, 1, axis=0))
    xm = u + mu_ref[...] * (shifted - u)
    for i in range(bb):
        prev_scr[i] = u[(i + 1) * chunk - 1:(i + 1) * chunk, :]

    r = xm[:, 0:BR_W]
    k = xm[:, BR_W:2 * BR_W]
    v = xm[:, 2 * BR_W:3 * BR_W]
    wl = xm[:, 3 * BR_W:3 * BR_W + LORA_W]
    al = xm[:, 3 * BR_W + LORA_W:3 * BR_W + LORA_W + LORA_A]
    gl = xm[:, 3 * BR_W + LORA_W + LORA_A:A_COLS]

    z = w0_ref[...] + _bdot(jnp.tanh(wl), w2_ref[...])
    ld = -math.exp(-0.5) * _sigmoid(z)
    a = _sigmoid(a0_ref[...] + _bdot(al, a2_ref[...]))
    g = _bdot(_sigmoid(gl), g2_ref[...])
    kk = k * kk_ref[...]
    ss = _head_sum(kk * kk, hsum_ref[...], 2)
    kk = kk * lax.rsqrt(jnp.maximum(ss, 1e-24))
    k = k * (1.0 + (a - 1.0) * ka_ref[...])
    if valid < chunk:
        ok = pos < valid
        ld = jnp.where(ok, ld, 0.0)
        kk = jnp.where(ok, kk, 0.0)
        k = jnp.where(ok, k, 0.0)
    bv = kk * a

    cl = _chunk_cumsum(ld, chunk)
    cl_last = _last_rows(cl, chunk, bb)
    e_ncl = jnp.exp(-cl)
    e_rem = jnp.exp(cl_last - cl)
    alpha_t = -kk * jnp.exp(cl - ld)
    r_t = r * jnp.exp(cl)
    b_bar = bv * e_ncl
    k_bar = k * e_ncl
    b_hat = bv * e_rem
    k_hat = k * e_rem
    e_last = jnp.exp(cl_last)
    rkk = r * k * rk_ref[...]

    qr = lax.broadcasted_iota(jnp.int32, (2 * chunk, 2 * chunk), 0)
    qc = lax.broadcasted_iota(jnp.int32, (2 * chunk, 2 * chunk), 1)
    qt, qs = qr & (chunk - 1), qc & (chunk - 1)
    keep = (qt > qs) | ((qr >= chunk) & (qt == qs))

    def blk(t, i, h):
        return t[i * chunk:(i + 1) * chunk, h * A_HD:(h + 1) * A_HD]

    aa, st0, vv, x, p = {}, {}, {}, {}, {}
    for i, h in units:
        lhs = jnp.concatenate([blk(alpha_t, i, h), blk(r_t, i, h)], axis=0)
        rhs = jnp.concatenate([blk(b_bar, i, h), blk(k_bar, i, h)], axis=0)
        aa[i, h] = jnp.where(keep, _bdot(lhs, rhs, NT), 0.0)
        st0[i, h] = st_scr[i, h]
        vv[i, h] = blk(v, i, h)
    for i, h in units:
        lhs = jnp.concatenate([blk(alpha_t, i, h), aa[i, h][0:chunk, chunk:2 * chunk]], axis=1)
        x[i, h] = _bdot(lhs, jnp.concatenate([st0[i, h], vv[i, h]], axis=0))
        p[i, h] = aa[i, h][0:chunk, 0:chunk]
    for i, h in units:
        x[i, h] = x[i, h] + _bdot(p[i, h], x[i, h])
    span = 2
    while span < chunk:
        for i, h in units:
            p[i, h] = _bdot(p[i, h], p[i, h])
        for i, h in units:
            x[i, h] = x[i, h] + _bdot(p[i, h], x[i, h])
        span *= 2
    for i, h in units:
        sl = slice(h * A_HD, (h + 1) * A_HD)
        lhs = jnp.concatenate([blk(r_t, i, h), aa[i, h][chunk:2 * chunk, :]], axis=1)
        o = _bdot(lhs, jnp.concatenate([st0[i, h], x[i, h], vv[i, h]], axis=0))
        upd = _bdot(jnp.concatenate([blk(b_hat, i, h), blk(k_hat, i, h)], axis=0),
                    jnp.concatenate([x[i, h], vv[i, h]], axis=0), TN)
        dec = jnp.broadcast_to(e_last[i * chunk:i * chunk + 1, sl], (A_HD, A_HD)).T
        st_scr[i, h] = st0[i, h] * dec + upd
        o_scr[i * chunk:(i + 1) * chunk, sl] = o

    hsum = hsum_ref[...]
    o = o_scr[...]
    d = o - _head_sum(o, hsum, 2) * (1.0 / A_HD)
    var = _head_sum(d * d, hsum, 1) * (1.0 / A_HD)
    o = d * lax.rsqrt(var + RWKV_GN_EPS) * lnw_ref[...] + lnb_ref[...]
    o = o + _head_sum(rkk, hsum, 1) * v
    o_ref[...] = (o * g).reshape(bb, chunk, BR_W)

    @pl.when(n == n_chunks - 1)
    def _():
        last = (valid - 1) if n_chunks == 1 else (chunk - 1)
        for i in range(bb):
            shift_out_ref[i] = u[i * chunk + last:i * chunk + last + 1, :]
        for i, h in units:
            s_out_ref[i, h] = st_scr[i, h].T


def _rwkv(u, shift, wkv, layer, stack, p, chunk, valid, bb):
    batch, seq, _ = u.shape
    n_chunks = seq // chunk
    assert valid == chunk or n_chunks == 1
    a_blk = U_A_OFF // A_COLS
    vec = lambda w: pl.BlockSpec((1, w), lambda b, n: (0, 0))
    full = lambda s: pl.BlockSpec(s, lambda b, n: (0,) * len(s))
    kern = functools.partial(_rwkv_kernel, chunk=chunk, valid=valid, n_chunks=n_chunks, bb=bb)
    return pl.pallas_call(
        kern,
        grid=(batch // bb, n_chunks),
        in_specs=[
            pl.BlockSpec((bb, chunk, A_COLS), lambda b, n: (b, n, a_blk)),
            pl.BlockSpec((None, bb, 1, A_COLS), lambda b, n: (layer, b, 0, 0)),
            pl.BlockSpec((None, bb, A_HEADS, A_HD, A_HD), lambda b, n: (layer, b, 0, 0, 0)),
            vec(A_COLS), vec(BR_W), full((LORA_W, BR_W)), vec(BR_W), full((LORA_A, BR_W)),
            full((LORA_G, BR_W)), vec(BR_W), vec(BR_W), vec(BR_W), vec(BR_W), vec(BR_W),
            pl.BlockSpec((LANES, LANES), lambda b, n: (0, 0)),
            pl.BlockSpec(memory_space=pl.ANY),
        ],
        input_output_aliases={15: 2},
        out_specs=[
            pl.BlockSpec((bb, chunk, BR_W), lambda b, n: (b, n, 0)),
            pl.BlockSpec((bb, 1, A_COLS), lambda b, n: (b, 0, 0)),
            pl.BlockSpec((None, bb, A_HEADS, A_HD, A_HD), lambda b, n: (p["layer"], b, 0, 0, 0)),
        ],
        out_shape=[
            jax.ShapeDtypeStruct((batch, seq, BR_W), F32),
            jax.ShapeDtypeStruct((batch, 1, A_COLS), F32),
            jax.ShapeDtypeStruct(stack.shape, F32),
        ],
        scratch_shapes=[pltpu.VMEM((bb, A_HEADS, A_HD, A_HD), F32), pltpu.VMEM((bb, 1, A_COLS), F32),
                        pltpu.VMEM((bb * chunk, BR_W), F32)],
        compiler_params=_cparams(("parallel", "arbitrary")),
        name="rwkv7",
    )(u, shift, wkv, p["mu"], p["w0"], p["w2"], p["a0"], p["a2"], p["g2"], p["kk"], p["ka"],
      p["rk"], p["lnw"], p["lnb"], p["hsum"], stack)


def _ret_kernel(u_ref, cos_ref, sin_ref, s0_ref, gn_ref, hsum_ref, stack_ref, o_ref, s_out_ref, s_scr, o_scr,
                *, chunk, valid, n_chunks, bb):
    n = pl.program_id(1)
    units = [(i, h) for i in range(bb) for h in range(B_HEADS)]

    @pl.when(n == 0)
    def _():
        s_scr[...] = s0_ref[...]

    cos = cos_ref[...]
    sin = sin_ref[...]
    tr = lax.broadcasted_iota(jnp.int32, (chunk, chunk), 0)
    tc = lax.broadcasted_iota(jnp.int32, (chunk, chunk), 1)
    diff = (tr - tc).astype(F32)
    causal = tr >= tc
    pos = lax.broadcasted_iota(jnp.int32, (chunk, 1), 0)
    posf = pos.astype(F32)
    lgs = [math.log(1.0 - 2.0 ** (-5.0 - h)) for h in range(B_HEADS)]
    intra = [jnp.where(causal, jnp.exp(lg * jnp.maximum(diff, 0.0)), 0.0) for lg in lgs]
    q_dec = [jnp.exp(lg * (posf + 1.0)) for lg in lgs]
    k_dec = [jnp.where(pos < valid, jnp.exp(lg * (valid - 1.0 - posf)), 0.0) for lg in lgs]
    s_dec = [math.exp(lg * valid) for lg in lgs]

    q, k, v, s0, sc = {}, {}, {}, {}, {}
    for i, h in units:
        ui = u_ref[i]
        qh = ui[:, h * B_HD:(h + 1) * B_HD]
        kh = ui[:, BR_W + h * B_HD:BR_W + (h + 1) * B_HD]
        q[i, h] = qh * cos + pltpu.roll(qh, B_HD // 2, axis=1) * sin
        k[i, h] = (kh * cos + pltpu.roll(kh, B_HD // 2, axis=1) * sin) * (B_HD ** -0.5)
        v[i, h] = ui[:, 2 * BR_W + h * B_HD:2 * BR_W + (h + 1) * B_HD]
        s0[i, h] = s_scr[i, h]
    for i, h in units:
        sc[i, h] = _bdot(q[i, h], k[i, h], NT) * intra[h]
    for i, h in units:
        sl = slice(h * B_HD, (h + 1) * B_HD)
        o = _bdot(sc[i, h], v[i, h]) + _bdot(q[i, h] * q_dec[h], s0[i, h])
        s_scr[i, h] = s0[i, h] * s_dec[h] + _bdot(k[i, h] * k_dec[h], v[i, h], TN)
        o_scr[i * chunk:(i + 1) * chunk, sl] = o

    hsum = hsum_ref[...]
    o = o_scr[...]
    d = o - _head_sum(o, hsum, 2) * (1.0 / B_HD)
    var = _head_sum(d * d, hsum, 1) * (1.0 / B_HD)
    gt = u_ref[:, :, 3 * BR_W:4 * BR_W].reshape(bb * chunk, BR_W)
    o_ref[...] = (d * lax.rsqrt(var + GN_EPS) * gn_ref[...] * _silu(gt)).reshape(bb, chunk, BR_W)

    @pl.when(n == n_chunks - 1)
    def _():
        s_out_ref[...] = s_scr[...]


def _retention(u, cos, sin, state, layer, stack, out_layer, gn, hsum, chunk, valid, bb):
    batch, seq, _ = u.shape
    n_chunks = seq // chunk
    assert valid == chunk or n_chunks == 1
    kern = functools.partial(_ret_kernel, chunk=chunk, valid=valid, n_chunks=n_chunks, bb=bb)
    return pl.pallas_call(
        kern,
        grid=(batch // bb, n_chunks),
        in_specs=[
            pl.BlockSpec((bb, chunk, B_COLS), lambda b, n: (b, n, U_B_OFF // B_COLS)),
            pl.BlockSpec((chunk, B_HD), lambda b, n: (n, 0)),
            pl.BlockSpec((chunk, B_HD), lambda b, n: (n, 0)),
            pl.BlockSpec((None, bb, B_HEADS, B_HD, B_HD), lambda b, n: (layer, b, 0, 0, 0)),
            pl.BlockSpec((1, BR_W), lambda b, n: (0, 0)),
            pl.BlockSpec((LANES, LANES), lambda b, n: (0, 0)),
            pl.BlockSpec(memory_space=pl.ANY),
        ],
        input_output_aliases={6: 1},
        out_specs=[
            pl.BlockSpec((bb, chunk, BR_W), lambda b, n: (b, n, 0)),
            pl.BlockSpec((None, bb, B_HEADS, B_HD, B_HD), lambda b, n: (out_layer, b, 0, 0, 0)),
        ],
        out_shape=[
            jax.ShapeDtypeStruct((batch, seq, BR_W), F32),
            jax.ShapeDtypeStruct(stack.shape, F32),
        ],
        scratch_shapes=[pltpu.VMEM((bb, B_HEADS, B_HD, B_HD), F32), pltpu.VMEM((bb * chunk, BR_W), F32)],
        compiler_params=_cparams(("parallel", "arbitrary")),
        name="retention",
    )(u, cos, sin, state, gn, hsum, stack)


def _hgrn_kernel(u_ref, lb_ref, s0_ref, gn_ref, hsum_ref, stack_ref, o_ref, s_out_ref, st_scr, o_scr,
                 *, chunk, valid, n_chunks, bb):
    n = pl.program_id(1)
    rows = bb * chunk
    units = [(i, h) for i in range(bb) for h in range(C_HEADS)]

    @pl.when(n == 0)
    def _():
        for i, h in units:
            st_scr[i, h] = s0_ref[i, h].T

    lb = lb_ref[...]
    row = lax.broadcasted_iota(jnp.int32, (rows, 1), 0)
    sig_f = _sigmoid(u_ref[:, :, BR_W:2 * BR_W].reshape(rows, BR_W))
    lf = jnp.log(jnp.maximum(lb + (1.0 - lb) * sig_f, F_FLOOR))
    kc = (1.0 - lb) * (1.0 - sig_f)
    if valid < chunk:
        ok = (row & (chunk - 1)) < valid
        lf = jnp.where(ok, lf, 0.0)
        kc = jnp.where(ok, kc, 0.0)
    bc = _chunk_cumsum(lf, chunk)
    b_last = _last_rows(bc, chunk, bb)
    e_b = jnp.exp(bc)
    e_last = jnp.exp(b_last)
    k_hat = kc * jnp.exp(b_last - bc)
    q_all = u_ref[:, :, 0:BR_W].reshape(rows, BR_W)
    tr =lax.broadcasted_iota(jnp.int32, (chunk, chunk), 0)
    tc = lax.broadcasted_iota(jnp.int32, (chunk, chunk), 1)
    differ = tr ^ tc
    sub8 = lax.broadcasted_iota(jnp.int32, (8, 1), 0)
    width = min(chunk, C_HD)

    def blk(t, i, h):
        return t[i * chunk:(i + 1) * chunk, h * C_HD:(h + 1) * C_HD]

    def boundary(c):
        pieces = []
        for base in range(0, rows, 8):
            if 2 * c <= 8:
                parts = [jnp.broadcast_to(bc[base + m + c - 1:base + m + c, :], (8, BR_W))
                         for m in range(0, 8, 2 * c)]
                piece = parts[-1]
                for idx in range(len(parts) - 2, -1, -1):
                    piece = jnp.where(sub8 < (idx + 1) * 2 * c, parts[idx], piece)
            else:
                at = (base // (2 * c)) * (2 * c) + c - 1
                piece = jnp.broadcast_to(bc[at:at + 1, :], (8, BR_W))
            pieces.append(piece)
        return jnp.concatenate(pieces, axis=0)

    q, v, st, scores = {}, {}, {}, {}
    diag = _head_sum(q_all * kc, hsum_ref[...], 2)
    for i, h in units:
        v[i, h] = u_ref[i, :, 2 * BR_W + h * C_HD:2 * BR_W + (h + 1) * C_HD]
        q[i, h] = blk(q_all, i, h)
        st[i, h] = st_scr[i, h]
        scores[i, h] = jnp.where(tr == tc, diag[i * chunk:(i + 1) * chunk, h * C_HD:h * C_HD + width], 0.0)
    c = 1
    while c < chunk:
        w = jnp.exp(-jnp.abs(bc - boundary(c)))
        qz = q_all * w
        kz = kc * w
        here = (tr > tc) & (differ >= c) & (differ < 2 * c)
        for i, h in units:
            scores[i, h] = jnp.where(here, _bdot(blk(qz, i, h), blk(kz, i, h), NT), scores[i, h])
        c *= 2

    for i, h in units:
        sl = slice(h * C_HD, (h + 1) * C_HD)
        o = _bdot(scores[i, h], v[i, h]) + _bdot(q[i, h] * blk(e_b, i, h), st[i, h], NT)
        st_scr[i, h] = st[i, h] * e_last[i * chunk:i * chunk + 1, sl] + _bdot(v[i, h], blk(k_hat, i, h), TN)
        o_scr[i * chunk:(i + 1) * chunk, sl] = o

    o = o_scr[...]
    ms = _head_sum(o * o, hsum_ref[...], 1) * (1.0 / C_HD)
    gt = u_ref[:, :, 3 * BR_W:4 * BR_W].reshape(rows, BR_W)
    o_ref[...] = (o * lax.rsqrt(ms + GN_EPS) * gn_ref[...] * _silu(gt)).reshape(bb, chunk, BR_W)

    @pl.when(n == n_chunks - 1)
    def _():
        for i, h in units:
            s_out_ref[i, h] = st_scr[i, h].T


def _hgrn(u, lb, state, layer, stack, out_layer, gn, hsum, chunk, valid, bb):
    batch, seq, _ = u.shape
    n_chunks = seq // chunk
    assert valid == chunk or n_chunks == 1
    kern = functools.partial(_hgrn_kernel, chunk=chunk, valid=valid, n_chunks=n_chunks, bb=bb)
    return pl.pallas_call(
        kern,
        grid=(batch // bb, n_chunks),
        in_specs=[
            pl.BlockSpec((bb, chunk, C_COLS), lambda b, n: (b, n, U_C_OFF // C_COLS)),
            pl.BlockSpec((1, BR_W), lambda b, n: (0, 0)),
            pl.BlockSpec((None, bb, C_HEADS, C_HD, C_HD), lambda b, n: (layer, b, 0, 0, 0)),
            pl.BlockSpec((1, BR_W), lambda b, n: (0, 0)),
            pl.BlockSpec((LANES, LANES), lambda b, n: (0, 0)),
            pl.BlockSpec(memory_space=pl.ANY),
        ],
        input_output_aliases={5: 1},
        out_specs=[
            pl.BlockSpec((bb, chunk, BR_W), lambda b, n: (b, n, 0)),
            pl.BlockSpec((None, bb, C_HEADS, C_HD, C_HD), lambda b, n: (out_layer, b, 0, 0, 0)),
        ],
        out_shape=[
            jax.ShapeDtypeStruct((batch, seq, BR_W), F32),
            jax.ShapeDtypeStruct(stack.shape, F32),
        ],
        scratch_shapes=[pltpu.VMEM((bb, C_HEADS, C_HD, C_HD), F32), pltpu.VMEM((bb * chunk, BR_W), F32)],
        compiler_params=_cparams(("parallel", "arbitrary")),
        name="hgrn2",
    )(u, lb, state, gn, hsum, stack)


def _merge_kernel(oa_ref, ob_ref, oc_ref, g0_ref, g1_ref, g2_ref, x_ref, wb_ref, wo_ref, o_ref):
    m = _sigmoid(g0_ref[...]) * _bdot(oa_ref[...], wb_ref[0])
    m = m + _sigmoid(g1_ref[...]) * _bdot(ob_ref[...], wb_ref[1])
    m = m + _sigmoid(g2_ref[...]) * _bdot(oc_ref[...], wb_ref[2])
    o_ref[...] = x_ref[...] + _bdot(m, wo_ref[...])


def _merge(oa, ob, oc, u, x, wb, wo, layer, tm):
    n = x.shape[0]
    g_blk = U_G_OFF // D_MODEL
    br = pl.BlockSpec((tm, BR_W), lambda i: (i, 0))
    gate = lambda c: pl.BlockSpec((tm, D_MODEL), lambda i: (i, g_blk + c))
    return pl.pallas_call(
        _merge_kernel,
        grid=(n // tm,),
        in_specs=[br, br, br, gate(0), gate(1), gate(2),
                  pl.BlockSpec((tm, D_MODEL), lambda i: (i, 0)),
                  pl.BlockSpec((None, 3, BR_W, D_MODEL), lambda i: (layer, 0, 0, 0)),
                  pl.BlockSpec((None, D_MODEL, D_MODEL), lambda i: (layer, 0, 0))],
        out_specs=pl.BlockSpec((tm, D_MODEL), lambda i: (i, 0)),
        out_shape=jax.ShapeDtypeStruct((n, D_MODEL), F32),
        compiler_params=_cparams(("parallel",)),
        name="merge",
    )(oa, ob, oc, u, u, u, x, wb, wo)


def _ffn_kernel(x_ref, g_ref, wg_ref, wu_ref, wo_ref, gfin_ref, o_ref, h_ref, *, n_ff, final):
    @pl.when(pl.program_id(1) == 0)
    def _():
        x = x_ref[...]
        ms = jnp.mean(x * x, axis=-1, keepdims=True)
        h_ref[...] = (x * lax.rsqrt(ms + RMS_EPS) * g_ref[...]).astype(BF16)
        o_ref[...] = x

    h = h_ref[...]
    gt = jnp.dot(h, wg_ref[...], preferred_element_type=F32)
    up = jnp.dot(h, wu_ref[...], preferred_element_type=F32)
    o_ref[...] += _bdot(_silu(gt) * up, wo_ref[...])

    if final:
        @pl.when(pl.program_id(1) == n_ff - 1)
        def _():
            x = o_ref[...]
            ms = jnp.mean(x * x, axis=-1, keepdims=True)
            o_ref[...] = x * lax.rsqrt(ms + RMS_EPS) * gfin_ref[...]


def _ffn(x, g, w_in, w_out, layer, g_final, final, tm, tf):
    n = x.shape[0]
    n_ff = D_FF // tf
    return pl.pallas_call(
        functools.partial(_ffn_kernel, n_ff=n_ff, final=final),
        grid=(n // tm, n_ff),
        in_specs=[
            pl.BlockSpec((tm, D_MODEL), lambda i, j: (i, 0)),
            pl.BlockSpec((1, D_MODEL), lambda i, j: (0, 0)),
            pl.BlockSpec((None, D_MODEL, tf), lambda i, j: (layer, 0, j)),
            pl.BlockSpec((None, D_MODEL, tf), lambda i, j: (layer, 0, n_ff + j)),
            pl.BlockSpec((None, tf, D_MODEL), lambda i, j: (layer, j, 0)),
            pl.BlockSpec((1, D_MODEL), lambda i, j: (0, 0)),
        ],
        out_specs=pl.BlockSpec((tm, D_MODEL), lambda i, j: (i, 0)),
        out_shape=jax.ShapeDtypeStruct((n, D_MODEL), F32),
        scratch_shapes=[pltpu.VMEM((tm, D_MODEL), BF16)],
        compiler_params=_cparams(("parallel", "arbitrary")),
        name="ffn",
    )(x, g, w_in, w_in, w_out, g_final)


def _rope_tables(pos):
    half = B_HD // 2
    inv = ROPE_BASE ** (-jnp.arange(half, dtype=F32) / half)
    ang = pos.astype(F32)[:, None] * inv[None, :]
    cos, sin = jnp.cos(ang), jnp.sin(ang)
    return jnp.concatenate([cos, cos], axis=-1), jnp.concatenate([-sin, sin], axis=-1)


def _row_tile(n, largest=512):
    t = largest
    while t >= 8:
        if n % t == 0:
            return t
        t //= 2
    raise ValueError(f"row count {n} is not a multiple of 8")


def _layer(x, states, layer, stacks, cos, sin, p, batch, seq, valid, cfg):
    shift, wkv, ret, hg = states
    wkv_stack, ret_stack, hg_stack = stacks
    tm = _row_tile(x.shape[0])
    wl = p["layer"]
    u2 = _in_proj(x, p["norm_mix"], p["w_in"], wl, _row_tile(x.shape[0], IN_ROW_TILE), IN_COL_TILE)
    u = u2.reshape(batch, seq, IN_COLS)
    full = lambda c: valid if c >= seq else c
    (ca, sa), (cb, sb), (cc, sc) = cfg["rwkv"], cfg["ret"], cfg["hgrn"]
    oa, shift_new, wkv_stack = _rwkv(u, shift, wkv, layer, wkv_stack, p, ca, full(ca), sa)
    ob, ret_stack = _retention(u, cos, sin, ret, layer, ret_stack, wl, p["ret_gn"], p["hsum128"], cb, full(cb), sb)
    oc, hg_stack = _hgrn(u, p["lower"], hg, layer, hg_stack, wl, p["hgrn_gn"], p["hsum128"], cc, full(cc), sc)
    flat = lambda o: o.reshape(batch * seq, BR_W)
    x = _merge(flat(oa), flat(ob), flat(oc), u2, x, p["w_branch"], p["w_out"], wl, tm)
    x = _ffn(x, p["norm_ffn"], p["w_ffn_in"], p["w_ffn_out"], wl, p["norm_final"], p["last"], tm, FF_TILE)
    return x, shift_new, (wkv_stack, ret_stack, hg_stack)


def kernel(x_prompt, x_sample, state_rwkv_shift, state_rwkv_wkv, state_ret, state_hgrn,
           norm_mix, w_in, rwkv_mu, rwkv_w0, rwkv_w2, rwkv_a0, rwkv_a2, rwkv_g2, rwkv_kk,
           rwkv_ka, rwkv_rk, rwkv_ln_w, rwkv_ln_b, ret_gn, hgrn_lb, hgrn_gn, w_branch, w_out,
           norm_ffn, w_ffn_in, w_ffn_out, norm_final):
    depth = w_in.shape[0]
    bp, tp, _ = x_prompt.shape
    bs, ts, _ = x_sample.shape
    assert ts <= SAMPLE_CHUNK
    assert all(tp % c == 0 and bp % s == 0 for c, s in PROMPT_CFG.values())
    assert all(bs % s == 0 for _, s in SAMPLE_CFG.values())
    dt = x_prompt.dtype

    lower = _lower_bounds(hgrn_lb)
    w_in_r = jnp.concatenate(
        [w_in[:, :, A_COLS:A_COLS + B_COLS + C_COLS + GATE_COLS], w_in[:, :, :A_COLS]], axis=-1).astype(BF16)
    w_branch_b = w_branch.astype(BF16)
    w_out_b = w_out.astype(BF16)
    w_ffn_in_b = w_ffn_in.astype(BF16)
    w_ffn_out_b = w_ffn_out.astype(BF16)
    lanes = jnp.arange(BR_W)
    hsum = (lanes[:, None] // A_HD == lanes[None, :] // A_HD).astype(BF16)
    hsum128 = (lanes[:, None] // B_HD == lanes[None, :] // B_HD).astype(BF16)
    row = lambda a: a.reshape(1, -1).astype(F32)

    cos_p, sin_p = _rope_tables(jnp.arange(tp, dtype=jnp.int32))
    cos_s, sin_s = _rope_tables(PAST_LEN + jnp.arange(SAMPLE_CHUNK, dtype=jnp.int32))

    xp = x_prompt.reshape(bp * tp, D_MODEL)
    xs = jnp.pad(x_sample, ((0, 0), (0, SAMPLE_CHUNK - ts), (0, 0))).reshape(bs * SAMPLE_CHUNK, D_MODEL)
    zero_states = (jnp.zeros((1, bp, 1, A_COLS), F32), jnp.zeros((1, bp, A_HEADS, A_HD, A_HD), F32),
                   jnp.zeros((1, bp, B_HEADS, B_HD, B_HD), F32), jnp.zeros((1, bp, C_HEADS, C_HD, C_HD), F32))
    sample_states = (state_rwkv_shift.reshape(depth, bs, 1, A_COLS), state_rwkv_wkv, state_ret, state_hgrn)

    new_states = lambda b: (jnp.zeros((depth, b, A_HEADS, A_HD, A_HD), F32),
                            jnp.zeros((depth, b, B_HEADS, B_HD, B_HD), F32),
                            jnp.zeros((depth, b, C_HEADS, C_HD, C_HD), F32))
    stacks_p, stacks_s = new_states(bp), new_states(bs)
    shifts_p, shifts_s = [], []
    for l in range(depth):
        p = {
            "layer": l, "norm_mix": row(norm_mix[l]), "w_in": w_in_r, "mu": row(rwkv_mu[l]),
            "w0": row(rwkv_w0[l]), "w2": rwkv_w2[l].astype(BF16), "a0": row(rwkv_a0[l]),
            "a2": rwkv_a2[l].astype(BF16), "g2": rwkv_g2[l].astype(BF16), "kk": row(rwkv_kk[l]),
            "ka": row(rwkv_ka[l]), "rk": row(rwkv_rk[l]), "lnw": row(rwkv_ln_w[l]),
            "lnb": row(rwkv_ln_b[l]), "hsum": hsum, "hsum128": hsum128, "ret_gn": row(ret_gn[l]),
            "lower": lower[l:l + 1], "hgrn_gn": row(hgrn_gn[l]), "w_branch": w_branch_b,
            "w_out": w_out_b, "norm_ffn": row(norm_ffn[l]), "w_ffn_in": w_ffn_in_b,
            "w_ffn_out": w_ffn_out_b, "norm_final": row(norm_final), "last": l == depth - 1,
        }
        xp, sh_p, stacks_p = _layer(xp, zero_states, 0, stacks_p, cos_p, sin_p, p, bp, tp, tp, PROMPT_CFG)
        xs, sh_s, stacks_s = _layer(xs, sample_states, l, stacks_s, cos_s, sin_s, p, bs, SAMPLE_CHUNK, ts,
                                    SAMPLE_CFG)
        shifts_p.append(sh_p.reshape(bp, A_COLS))
        shifts_s.append(sh_s.reshape(bs, A_COLS))

    y_prompt = xp.reshape(bp, tp, D_MODEL)
    y_sample = xs.reshape(bs, SAMPLE_CHUNK, D_MODEL)[:, :ts]
    cast = lambda a: a.astype(dt)
    return (y_prompt, y_sample, cast(jnp.stack(shifts_p)), *map(cast, stacks_p),
            cast(jnp.stack(shifts_s)), *map(cast, stacks_s))
```

```python
import functools
import math

import jax
import jax.numpy as jnp
from jax import lax
from jax.experimental import pallas as pl
from jax.experimental.pallas import tpu as pltpu

F32 = jnp.float32
BF16 = jnp.bfloat16

D_MODEL = 1024
BR_W = 512
A_HD = 64
A_HEADS = BR_W // A_HD
LORA_W = 64
LORA_A = 64
LORA_G = 128
A_COLS = 3 * BR_W + LORA_W + LORA_A + LORA_G
B_HD = 128
B_HEADS = BR_W // B_HD
C_HD = 128
C_HEADS = BR_W // C_HD
B_COLS = 4 * BR_W
C_COLS = 4 * BR_W
GATE_COLS = 3 * D_MODEL
IN_COLS = A_COLS + B_COLS + C_COLS + GATE_COLS
D_FF = 2816
PAST_LEN = 16384
RMS_EPS = 1e-6
RWKV_GN_EPS = 64e-5
GN_EPS = 1e-5
F_FLOOR = 1e-30
ROPE_BASE = 10000.0

U_B_OFF = 0
U_C_OFF = B_COLS
U_G_OFF = B_COLS + C_COLS
U_A_OFF = B_COLS + C_COLS + GATE_COLS

SAMPLE_CHUNK = 8
PROMPT_CFG = {"rwkv": (64, 4), "ret": (128, 8), "hgrn": (128, 4)}
SAMPLE_CFG = {"rwkv": (SAMPLE_CHUNK, 4), "ret": (SAMPLE_CHUNK, 16), "hgrn": (SAMPLE_CHUNK, 16)}
RWKV_GROUP = 8
IN_ROW_TILE = 256
IN_COL_TILE = IN_COLS
FF_TILE = D_FF

VMEM_LIMIT = 56 * 1024 * 1024
LANES = 128

NN = (((1,), (0,)), ((), ()))
NT = (((1,), (1,)), ((), ()))
TN = (((0,), (0,)), ((), ()))


def _bdot(a, b, dims=NN):
    return lax.dot_general(a.astype(BF16), b.astype(BF16), dims, preferred_element_type=F32)


def _sel_dot(sel, x, passes, sel_first=True):
    sel = sel.astype(BF16)
    acc = None
    for _ in range(passes):
        piece = x.astype(BF16)
        ops = (sel, piece) if sel_first else (piece, sel)
        part = lax.dot_general(*ops, NN, preferred_element_type=F32)
        acc = part if acc is None else acc + part
        x = x - piece.astype(F32)
    return acc


def _head_sum(x, same_head, passes):
    return jnp.concatenate([_sel_dot(same_head, x[:, g:g + LANES], passes, sel_first=False)
                            for g in range(0, x.shape[1], LANES)], axis=1)


def _sigmoid(x):
    return 0.5 * jnp.tanh(0.5 * x) + 0.5


def _silu(x):
    return x * _sigmoid(x)


def _cparams(sem):
    return pltpu.CompilerParams(dimension_semantics=sem, vmem_limit_bytes=VMEM_LIMIT)


def _chunk_cumsum(x, chunk):
    rows = x.shape[0]
    group = max(chunk, min(rows, 64))
    assert rows % group == 0
    tr = lax.broadcasted_iota(jnp.int32, (group, group), 0)
    tc = lax.broadcasted_iota(jnp.int32, (group, group), 1)
    sh = chunk.bit_length() - 1
    tri = jnp.where((tr >= tc) & ((tr >> sh) == (tc >> sh)), 1.0, 0.0)
    return jnp.concatenate([_sel_dot(tri, x[r:r + group], 3) for r in range(0, rows, group)], axis=0)


def _last_rows(x, chunk, bb):
    return jnp.concatenate(
        [jnp.broadcast_to(x[(i + 1) * chunk - 1:(i + 1) * chunk, :], (chunk, x.shape[1])) for i in range(bb)],
        axis=0)


def _lower_kernel(lb_ref, o_ref):
    x = lb_ref[...]
    depth = x.shape[0]
    m = x[0:1]
    for l in range(1, depth):
        m = jnp.maximum(m, x[l:l + 1])
    e = jnp.exp(x - m)
    tot = e[0:1]
    for l in range(1, depth):
        tot = tot + e[l:l + 1]
    sm = e / tot
    acc = jnp.zeros_like(m)
    for l in range(depth):
        acc = acc + sm[l:l + 1]
        o_ref[l:l + 1, :] = acc - sm[0:1]


def _lower_bounds(hgrn_lb):
    return pl.pallas_call(
        _lower_kernel, out_shape=jax.ShapeDtypeStruct(hgrn_lb.shape, F32), name="hgrn_lower",
    )(hgrn_lb.astype(F32))


def _in_proj_kernel(x_ref, g_ref, w_ref, o_ref, h_ref):
    @pl.when(pl.program_id(1) == 0)
    def _():
        x = x_ref[...]
        ms = jnp.mean(x * x, axis=-1, keepdims=True)
        h_ref[...] = (x * lax.rsqrt(ms + RMS_EPS) * g_ref[...]).astype(BF16)

    o_ref[...] = jnp.dot(h_ref[...], w_ref[...], preferred_element_type=F32)


def _in_proj(x, g, w, layer, tm, tn):
    n, d = x.shape
    cols = w.shape[2]
    resident = {"pipeline_mode": pl.Buffered(1)} if tn == cols else {}
    return pl.pallas_call(
        _in_proj_kernel,
        grid=(n // tm, cols // tn),
        in_specs=[
            pl.BlockSpec((tm, d), lambda i, j: (i, 0)),
            pl.BlockSpec((1, d), lambda i, j: (0, 0)),
            pl.BlockSpec((None, d, tn), lambda i, j: (layer, 0, j), **resident),
        ],
        out_specs=pl.BlockSpec((tm, tn), lambda i, j: (i, j)),
        out_shape=jax.ShapeDtypeStruct((n, cols), F32),
        scratch_shapes=[pltpu.VMEM((tm, d), BF16)],
        compiler_params=_cparams(("parallel", "arbitrary")),
        name="in_proj",
    )(x, g, w)


def _rwkv_kernel(u_ref, shift_ref, s0_ref, mu_ref, w0_ref, w2_ref, a0_ref, a2_ref, g2_ref,
                 kk_ref, ka_ref, rk_ref, lnw_ref, lnb_ref, hsum_ref, stack_ref,
                 o_ref, shift_out_ref, s_out_ref, st_scr, prev_scr, o_scr, *, chunk, valid, n_chunks, bb, gb):
    n = pl.program_id(1)

    @pl.when(n == 0)
    def _():
        for i in range(bb):
            for h in range(A_HEADS):
                st_scr[i, h] = s0_ref[i, h].T
        prev_scr[...] = shift_ref[...]

    for g0 in range(0, bb, gb):
        seqs = pl.ds(g0, gb)
        _rwkv_group(u_ref.at[seqs], mu_ref, w0_ref, w2_ref, a0_ref, a2_ref, g2_ref, kk_ref, ka_ref, rk_ref,
                    lnw_ref, lnb_ref, hsum_ref, o_ref.at[seqs], st_scr.at[seqs], prev_scr.at[seqs],
                    o_scr.at[pl.ds(g0 * chunk, gb * chunk)], chunk=chunk, valid=valid, bb=gb)

    @pl.when(n == n_chunks - 1)
    def _():
        last = (valid - 1) if n_chunks == 1 else (chunk - 1)
        for i in range(bb):
            shift_out_ref[i] = u_ref[i, last:last + 1, :]
            for h in range(A_HEADS):
                s_out_ref[i, h] = st_scr[i, h].T


def _rwkv_group(u_ref, mu_ref, w0_ref, w2_ref, a0_ref, a2_ref, g2_ref, kk_ref, ka_ref, rk_ref,
                lnw_ref, lnb_ref, hsum_ref, o_ref, st_scr, prev_scr, o_scr, *, chunk, valid, bb):
    rows = bb * chunk
    units = [(i, h) for i in range(bb) for h in range(A_HEADS)]
    u = u_ref[...].reshape(rows, A_COLS)
    row = lax.broadcasted_iota(jnp.int32, (rows, 1), 0)
    pos = row & (chunk - 1)
    prev = jnp.concatenate([jnp.broadcast_to(prev_scr[i], (chunk, A_COLS)) for i in range(bb)], axis=0)
    shifted = jnp.where(pos == 0, prev, pltpu.roll(u, 1, axis=0))
    xm = u + mu_ref[...] * (shifted - u)
    for i in range(bb):
        prev_scr[i] = u[(i + 1) * chunk - 1:(i + 1) * chunk, :]

    r = xm[:, 0:BR_W]
    k = xm[:, BR_W:2 * BR_W]
    v = xm[:, 2 * BR_W:3 * BR_W]
    wl = xm[:, 3 * BR_W:3 * BR_W + LORA_W]
    al = xm[:, 3 * BR_W + LORA_W:3 * BR_W + LORA_W + LORA_A]
    gl = xm[:, 3 * BR_W + LORA_W + LORA_A:A_COLS]

    z = w0_ref[...] + _bdot(jnp.tanh(wl), w2_ref[...])
    ld = -math.exp(-0.5) * _sigmoid(z)
    a = _sigmoid(a0_ref[...] + _bdot(al, a2_ref[...]))
    g = _bdot(_sigmoid(gl), g2_ref[...])
    kk = k * kk_ref[...]
    ss = _head_sum(kk * kk, hsum_ref[...], 2)
    kk = kk * lax.rsqrt(jnp.maximum(ss, 1e-24))
    k = k * (1.0 + (a - 1.0) * ka_ref[...])
    if valid < chunk:
        ok = pos < valid
        ld = jnp.where(ok, ld, 0.0)
        kk = jnp.where(ok, kk, 0.0)
        k = jnp.where(ok, k, 0.0)
    bv = kk * a

    cl = _chunk_cumsum(ld, chunk)
    cl_last = _last_rows(cl, chunk, bb)
    e_ncl = jnp.exp(-cl)
    e_rem = jnp.exp(cl_last - cl)
    alpha_t = -kk * jnp.exp(cl - ld)
    r_t = r * jnp.exp(cl)
    b_bar = bv * e_ncl
    k_bar = k * e_ncl
    b_hat = bv * e_rem
    k_hat = k * e_rem
    e_last = jnp.exp(cl_last)
    rkk = r * k * rk_ref[...]

    qr = lax.broadcasted_iota(jnp.int32, (2 * chunk, 2 * chunk), 0)
    qc = lax.broadcasted_iota(jnp.int32, (2 * chunk, 2 * chunk), 1)
    qt, qs = qr & (chunk - 1), qc & (chunk - 1)
    keep = (qt > qs) | ((qr >= chunk) & (qt == qs))

    def blk(t, i, h):
        return t[i * chunk:(i + 1) * chunk, h * A_HD:(h + 1) * A_HD]

    aa, st0, vv, x, p = {}, {}, {}, {}, {}
    for i, h in units:
        lhs = jnp.concatenate([blk(alpha_t, i, h), blk(r_t, i, h)], axis=0)
        rhs = jnp.concatenate([blk(b_bar, i, h), blk(k_bar, i, h)], axis=0)
        aa[i, h] = jnp.where(keep, _bdot(lhs, rhs, NT), 0.0)
        st0[i, h] = st_scr[i, h]
        vv[i, h] = blk(v, i, h)
    for i, h in units:
        lhs = jnp.concatenate([blk(alpha_t, i, h), aa[i, h][0:chunk, chunk:2 * chunk]], axis=1)
        x[i, h] = _bdot(lhs, jnp.concatenate([st0[i, h], vv[i, h]], axis=0))
        p[i, h] = aa[i, h][0:chunk, 0:chunk]
    for i, h in units:
        x[i, h] = x[i, h] + _bdot(p[i, h], x[i, h])
    span = 2
    while span < chunk:
        for i, h in units:
            p[i, h] = _bdot(p[i, h], p[i, h])
        for i, h in units:
            x[i, h] = x[i, h] + _bdot(p[i, h], x[i, h])
        span *= 2
    for i, h in units:
        sl = slice(h * A_HD, (h + 1) * A_HD)
        lhs = jnp.concatenate([blk(r_t, i, h), aa[i, h][chunk:2 * chunk, :]], axis=1)
        o = _bdot(lhs, jnp.concatenate([st0[i, h], x[i, h], vv[i, h]], axis=0))
        upd = _bdot(jnp.concatenate([blk(b_hat, i, h), blk(k_hat, i, h)], axis=0),
                    jnp.concatenate([x[i, h], vv[i, h]], axis=0), TN)
        dec = jnp.broadcast_to(e_last[i * chunk:i * chunk + 1, sl], (A_HD, A_HD)).T
        st_scr[i, h] = st0[i, h] * dec + upd
        o_scr[i * chunk:(i + 1) * chunk, sl] = o

    hsum = hsum_ref[...]
    o = o_scr[...]
    d = o - _head_sum(o, hsum, 2) * (1.0 / A_HD)
    var = _head_sum(d * d, hsum, 1) * (1.0 / A_HD)
    o = d * lax.rsqrt(var + RWKV_GN_EPS) * lnw_ref[...] + lnb_ref[...]
    o = o + _head_sum(rkk, hsum, 1) * v
    o_ref[...] = (o * g).reshape(bb, chunk, BR_W)


def _rwkv(u, shift, wkv, layer, stack, p, chunk, valid, bb):
    batch, seq, _ = u.shape
    n_chunks = seq // chunk
    assert valid == chunk or n_chunks == 1
    a_blk = U_A_OFF // A_COLS
    vec = lambda w: pl.BlockSpec((1, w), lambda b, n: (0, 0))
    full = lambda s: pl.BlockSpec(s, lambda b, n: (0,) * len(s))
    gb = math.gcd(bb, RWKV_GROUP)
    kern = functools.partial(_rwkv_kernel, chunk=chunk, valid=valid, n_chunks=n_chunks, bb=bb, gb=gb)
    return pl.pallas_call(
        kern,
        grid=(batch // bb, n_chunks),
        in_specs=[
            pl.BlockSpec((bb, chunk, A_COLS), lambda b, n: (b, n, a_blk)),
            pl.BlockSpec((None, bb, 1, A_COLS), lambda b, n: (layer, b, 0, 0)),
            pl.BlockSpec((None, bb, A_HEADS, A_HD, A_HD), lambda b, n: (layer, b, 0, 0, 0)),
            vec(A_COLS), vec(BR_W), full((LORA_W, BR_W)), vec(BR_W), full((LORA_A, BR_W)),
            full((LORA_G, BR_W)), vec(BR_W), vec(BR_W), vec(BR_W), vec(BR_W), vec(BR_W),
            pl.BlockSpec((LANES, LANES), lambda b, n: (0, 0)),
            pl.BlockSpec(memory_space=pl.ANY),
        ],
        input_output_aliases={15: 2},
        out_specs=[
            pl.BlockSpec((bb, chunk, BR_W), lambda b, n: (b, n, 0)),
            pl.BlockSpec((bb, 1, A_COLS), lambda b, n: (b, 0, 0)),
            pl.BlockSpec((None, bb, A_HEADS, A_HD, A_HD), lambda b, n: (p["layer"], b, 0, 0, 0)),
        ],
        out_shape=[
            jax.ShapeDtypeStruct((batch, seq, BR_W), F32),
            jax.ShapeDtypeStruct((batch, 1, A_COLS), F32),
            jax.ShapeDtypeStruct(stack.shape, F32),
        ],
        scratch_shapes=[pltpu.VMEM((bb, A_HEADS, A_HD, A_HD), F32), pltpu.VMEM((bb, 1, A_COLS), F32),
                        pltpu.VMEM((bb * chunk, BR_W), F32)],
        compiler_params=_cparams(("parallel", "arbitrary")),
        name="rwkv7",
    )(u, shift, wkv, p["mu"], p["w0"], p["w2"], p["a0"], p["a2"], p["g2"], p["kk"], p["ka"],
      p["rk"], p["lnw"], p["lnb"], p["hsum"], stack)


def _ret_kernel(u_ref, cos_ref, sin_ref, s0_ref, gn_ref, hsum_ref, stack_ref, o_ref, s_out_ref, s_scr, o_scr,
                *, chunk, valid, n_chunks, bb):
    n = pl.program_id(1)
    units = [(i, h) for i in range(bb) for h in range(B_HEADS)]

    @pl.when(n == 0)
    def _():
        s_scr[...] = s0_ref[...]

    cos = cos_ref[...]
    sin = sin_ref[...]
    tr = lax.broadcasted_iota(jnp.int32, (chunk, chunk), 0)
    tc = lax.broadcasted_iota(jnp.int32, (chunk, chunk), 1)
    diff = (tr - tc).astype(F32)
    causal = tr >= tc
    pos = lax.broadcasted_iota(jnp.int32, (chunk, 1), 0)
    posf = pos.astype(F32)
    lgs = [math.log(1.0 - 2.0 ** (-5.0 - h)) for h in range(B_HEADS)]
    intra = [jnp.where(causal, jnp.exp(lg * jnp.maximum(diff, 0.0)), 0.0) for lg in lgs]
    q_dec = [jnp.exp(lg * (posf + 1.0)) for lg in lgs]
    k_dec = [jnp.where(pos < valid, jnp.exp(lg * (valid - 1.0 - posf)), 0.0) for lg in lgs]
    s_dec = [math.exp(lg * valid) for lg in lgs]

    q, k, v, s0, sc = {}, {}, {}, {}, {}
    for i, h in units:
        ui = u_ref[i]
        qh = ui[:, h * B_HD:(h + 1) * B_HD]
        kh = ui[:, BR_W + h * B_HD:BR_W + (h + 1) * B_HD]
        q[i, h] = qh * cos + pltpu.roll(qh, B_HD // 2, axis=1) * sin
        k[i, h] = (kh * cos + pltpu.roll(kh, B_HD // 2, axis=1) * sin) * (B_HD ** -0.5)
        v[i, h] = ui[:, 2 * BR_W + h * B_HD:2 * BR_W + (h + 1) * B_HD]
        s0[i, h] = s_scr[i, h]
    for i, h in units:
        sc[i, h] = _bdot(q[i, h], k[i, h], NT) * intra[h]
    for i, h in units:
        sl = slice(h * B_HD, (h + 1) * B_HD)
        o = _bdot(sc[i, h], v[i, h]) + _bdot(q[i, h] * q_dec[h], s0[i, h])
        s_scr[i, h] = s0[i, h] * s_dec[h] + _bdot(k[i, h] * k_dec[h], v[i, h], TN)
        o_scr[i * chunk:(i + 1) * chunk, sl] = o

    hsum = hsum_ref[...]
    o = o_scr[...]
    d = o - _head_sum(o, hsum, 2) * (1.0 / B_HD)
    var = _head_sum(d * d, hsum, 1) * (1.0 / B_HD)
    gt = u_ref[:, :, 3 * BR_W:4 * BR_W].reshape(bb * chunk, BR_W)
    o_ref[...] = (d * lax.rsqrt(var + GN_EPS) * gn_ref[...] * _silu(gt)).reshape(bb, chunk, BR_W)

    @pl.when(n == n_chunks - 1)
    def _():
        s_out_ref[...] = s_scr[...]


def _retention(u, cos, sin, state, layer, stack, out_layer, gn, hsum, chunk, valid, bb):
    batch, seq, _ = u.shape
    n_chunks = seq // chunk
    assert valid == chunk or n_chunks == 1
    kern = functools.partial(_ret_kernel, chunk=chunk, valid=valid, n_chunks=n_chunks, bb=bb)
    return pl.pallas_call(
        kern,
        grid=(batch // bb, n_chunks),
        in_specs=[
            pl.BlockSpec((bb, chunk, B_COLS), lambda b, n: (b, n, U_B_OFF // B_COLS)),
            pl.BlockSpec((chunk, B_HD), lambda b, n: (n, 0)),
            pl.BlockSpec((chunk, B_HD), lambda b, n: (n, 0)),
            pl.BlockSpec((None, bb, B_HEADS, B_HD, B_HD), lambda b, n: (layer, b, 0, 0, 0)),
            pl.BlockSpec((1, BR_W), lambda b, n: (0, 0)),
            pl.BlockSpec((LANES, LANES), lambda b, n: (0, 0)),
            pl.BlockSpec(memory_space=pl.ANY),
        ],
        input_output_aliases={6: 1},
        out_specs=[
            pl.BlockSpec((bb, chunk, BR_W), lambda b, n: (b, n, 0)),
            pl.BlockSpec((None, bb, B_HEADS, B_HD, B_HD), lambda b, n: (out_layer, b, 0, 0, 0)),
        ],
        out_shape=[
            jax.ShapeDtypeStruct((batch, seq, BR_W), F32),
            jax.ShapeDtypeStruct(stack.shape, F32),
        ],
        scratch_shapes=[pltpu.VMEM((bb, B_HEADS, B_HD, B_HD), F32), pltpu.VMEM((bb * chunk, BR_W), F32)],
        compiler_params=_cparams(("parallel", "arbitrary")),
        name="retention",
    )(u, cos, sin, state, gn, hsum, stack)


def _hgrn_kernel(u_ref, lb_ref, s0_ref, gn_ref, hsum_ref, stack_ref, o_ref, s_out_ref, st_scr, o_scr,
                 *, chunk, valid, n_chunks, bb):
    n = pl.program_id(1)
    rows = bb * chunk
    units = [(i, h) for i in range(bb) for h in range(C_HEADS)]

    @pl.when(n == 0)
    def _():
        for i, h in units:
            st_scr[i, h] = s0_ref[i, h].T

    lb = lb_ref[...]
    row = lax.broadcasted_iota(jnp.int32, (rows, 1), 0)
    sig_f = _sigmoid(u_ref[:, :, BR_W:2 * BR_W].reshape(rows, BR_W))
    lf = jnp.log(jnp.maximum(lb + (1.0 - lb) * sig_f, F_FLOOR))
    kc = (1.0 - lb) * (1.0 - sig_f)
    if valid < chunk:
        ok = (row & (chunk - 1)) < valid
        lf = jnp.where(ok, lf, 0.0)
        kc = jnp.where(ok, kc, 0.0)
    bc = _chunk_cumsum(lf, chunk)
    b_last = _last_rows(bc, chunk, bb)
    e_b = jnp.exp(bc)
    e_last = jnp.exp(b_last)
    k_hat = kc * jnp.exp(b_last - bc)
    q_all = u_ref[:, :, 0:BR_W].reshape(rows, BR_W)
    tr =lax.broadcasted_iota(jnp.int32, (chunk, chunk), 0)
    tc = lax.broadcasted_iota(jnp.int32, (chunk, chunk), 1)
    differ = tr ^ tc
    sub8 = lax.broadcasted_iota(jnp.int32, (8, 1), 0)
    width = min(chunk, C_HD)

    def blk(t, i, h):
        return t[i * chunk:(i + 1) * chunk, h * C_HD:(h + 1) * C_HD]

    def boundary(c):
        pieces = []
        for base in range(0, rows, 8):
            if 2 * c <= 8:
                parts = [jnp.broadcast_to(bc[base + m + c - 1:base + m + c, :], (8, BR_W))
                         for m in range(0, 8, 2 * c)]
                piece = parts[-1]
                for idx in range(len(parts) - 2, -1, -1):
                    piece = jnp.where(sub8 < (idx + 1) * 2 * c, parts[idx], piece)
            else:
                at = (base // (2 * c)) * (2 * c) + c - 1
                piece = jnp.broadcast_to(bc[at:at + 1, :], (8, BR_W))
            pieces.append(piece)
        return jnp.concatenate(pieces, axis=0)

    q, v, st, scores = {}, {}, {}, {}
    diag = _head_sum(q_all * kc, hsum_ref[...], 2)
    for i, h in units:
        v[i, h] = u_ref[i, :, 2 * BR_W + h * C_HD:2 * BR_W + (h + 1) * C_HD]
        q[i, h] = blk(q_all, i, h)
        st[i, h] = st_scr[i, h]
        scores[i, h] = jnp.where(tr == tc, diag[i * chunk:(i + 1) * chunk, h * C_HD:h * C_HD + width], 0.0)
    c = 1
    while c < chunk:
        w = jnp.exp(-jnp.abs(bc - boundary(c)))
        qz = q_all * w
        kz = kc * w
        here = (tr > tc) & (differ >= c) & (differ < 2 * c)
        for i, h in units:
            scores[i, h] = jnp.where(here, _bdot(blk(qz, i, h), blk(kz, i, h), NT), scores[i, h])
        c *= 2

    for i, h in units:
        sl = slice(h * C_HD, (h + 1) * C_HD)
        o = _bdot(scores[i, h], v[i, h]) + _bdot(q[i, h] * blk(e_b, i, h), st[i, h], NT)
        st_scr[i, h] = st[i, h] * e_last[i * chunk:i * chunk + 1, sl] + _bdot(v[i, h], blk(k_hat, i, h), TN)
        o_scr[i * chunk:(i + 1) * chunk, sl] = o

    o = o_scr[...]
    ms = _head_sum(o * o, hsum_ref[...], 1) * (1.0 / C_HD)
    gt = u_ref[:, :, 3 * BR_W:4 * BR_W].reshape(rows, BR_W)
    o_ref[...] = (o * lax.rsqrt(ms + GN_EPS) * gn_ref[...] * _silu(gt)).reshape(bb, chunk, BR_W)

    @pl.when(n == n_chunks - 1)
    def _():
        for i, h in units:
            s_out_ref[i, h] = st_scr[i, h].T


def _hgrn(u, lb, state, layer, stack, out_layer, gn, hsum, chunk, valid, bb):
    batch, seq, _ = u.shape
    n_chunks = seq // chunk
    assert valid == chunk or n_chunks == 1
    kern = functools.partial(_hgrn_kernel, chunk=chunk, valid=valid, n_chunks=n_chunks, bb=bb)
    return pl.pallas_call(
        kern,
        grid=(batch // bb, n_chunks),
        in_specs=[
            pl.BlockSpec((bb, chunk, C_COLS), lambda b, n: (b, n, U_C_OFF // C_COLS)),
            pl.BlockSpec((1, BR_W), lambda b, n: (0, 0)),
            pl.BlockSpec((None, bb, C_HEADS, C_HD, C_HD), lambda b, n: (layer, b, 0, 0, 0)),
            pl.BlockSpec((1, BR_W), lambda b, n: (0, 0)),
            pl.BlockSpec((LANES, LANES), lambda b, n: (0, 0)),
            pl.BlockSpec(memory_space=pl.ANY),
        ],
        input_output_aliases={5: 1},
        out_specs=[
            pl.BlockSpec((bb, chunk, BR_W), lambda b, n: (b, n, 0)),
            pl.BlockSpec((None, bb, C_HEADS, C_HD, C_HD), lambda b, n: (out_layer, b, 0, 0, 0)),
        ],
        out_shape=[
            jax.ShapeDtypeStruct((batch, seq, BR_W), F32),
            jax.ShapeDtypeStruct(stack.shape, F32),
        ],
        scratch_shapes=[pltpu.VMEM((bb, C_HEADS, C_HD, C_HD), F32), pltpu.VMEM((bb * chunk, BR_W), F32)],
        compiler_params=_cparams(("parallel", "arbitrary")),
        name="hgrn2",
    )(u, lb, state, gn, hsum, stack)


def _merge_kernel(oa_ref, ob_ref, oc_ref, g0_ref, g1_ref, g2_ref, x_ref, wb_ref, wo_ref, o_ref):
    m = _sigmoid(g0_ref[...]) * _bdot(oa_ref[...], wb_ref[0])
    m = m + _sigmoid(g1_ref[...]) * _bdot(ob_ref[...], wb_ref[1])
    m = m + _sigmoid(g2_ref[...]) * _bdot(oc_ref[...], wb_ref[2])
    o_ref[...] = x_ref[...] + _bdot(m, wo_ref[...])


def _merge(oa, ob, oc, u, x, wb, wo, layer, tm):
    n = x.shape[0]
    g_blk = U_G_OFF // D_MODEL
    br = pl.BlockSpec((tm, BR_W), lambda i: (i, 0))
    gate = lambda c: pl.BlockSpec((tm, D_MODEL), lambda i: (i, g_blk + c))
    return pl.pallas_call(
        _merge_kernel,
        grid=(n // tm,),
        in_specs=[br, br, br, gate(0), gate(1), gate(2),
                  pl.BlockSpec((tm, D_MODEL), lambda i: (i, 0)),
                  pl.BlockSpec((None, 3, BR_W, D_MODEL), lambda i: (layer, 0, 0, 0)),
                  pl.BlockSpec((None, D_MODEL, D_MODEL), lambda i: (layer, 0, 0))],
        out_specs=pl.BlockSpec((tm, D_MODEL), lambda i: (i, 0)),
        out_shape=jax.ShapeDtypeStruct((n, D_MODEL), F32),
        compiler_params=_cparams(("parallel",)),
        name="merge",
    )(oa, ob, oc, u, u, u, x, wb, wo)


def _ffn_kernel(x_ref, g_ref, wg_ref, wu_ref, wo_ref, gfin_ref, o_ref, h_ref, *, n_ff, final):
    @pl.when(pl.program_id(1) == 0)
    def _():
        x = x_ref[...]
        ms = jnp.mean(x * x, axis=-1, keepdims=True)
        h_ref[...] = (x * lax.rsqrt(ms + RMS_EPS) * g_ref[...]).astype(BF16)
        o_ref[...] = x

    h = h_ref[...]
    gt = jnp.dot(h, wg_ref[...], preferred_element_type=F32)
    up = jnp.dot(h, wu_ref[...], preferred_element_type=F32)
    o_ref[...] += _bdot(_silu(gt) * up, wo_ref[...])

    if final:
        @pl.when(pl.program_id(1) == n_ff - 1)
        def _():
            x = o_ref[...]
            ms = jnp.mean(x * x, axis=-1, keepdims=True)
            o_ref[...] = x * lax.rsqrt(ms + RMS_EPS) * gfin_ref[...]


def _ffn(x, g, w_in, w_out, layer, g_final, final, tm, tf):
    n = x.shape[0]
    n_ff = D_FF // tf
    resident = {"pipeline_mode": pl.Buffered(1)} if n_ff == 1 else {}
    return pl.pallas_call(
        functools.partial(_ffn_kernel, n_ff=n_ff, final=final),
        grid=(n // tm, n_ff),
        in_specs=[
            pl.BlockSpec((tm, D_MODEL), lambda i, j: (i, 0)),
            pl.BlockSpec((1, D_MODEL), lambda i, j: (0, 0)),
            pl.BlockSpec((None, D_MODEL, tf), lambda i, j: (layer, 0, j), **resident),
            pl.BlockSpec((None, D_MODEL, tf), lambda i, j: (layer, 0, n_ff + j), **resident),
            pl.BlockSpec((None, tf, D_MODEL), lambda i, j: (layer, j, 0), **resident),
            pl.BlockSpec((1, D_MODEL), lambda i, j: (0, 0)),
        ],
        out_specs=pl.BlockSpec((tm, D_MODEL), lambda i, j: (i, 0)),
        out_shape=jax.ShapeDtypeStruct((n, D_MODEL), F32),
        scratch_shapes=[pltpu.VMEM((tm, D_MODEL), BF16)],
        compiler_params=_cparams(("parallel", "arbitrary")),
        name="ffn",
    )(x, g, w_in, w_in, w_out, g_final)


def _rope_tables(pos):
    half = B_HD // 2
    inv = ROPE_BASE ** (-jnp.arange(half, dtype=F32) / half)
    ang = pos.astype(F32)[:, None] * inv[None, :]
    cos, sin = jnp.cos(ang), jnp.sin(ang)
    return jnp.concatenate([cos, cos], axis=-1), jnp.concatenate([-sin, sin], axis=-1)


def _row_tile(n, largest=512):
    t = largest
    while t >= 8:
        if n % t == 0:
            return t
        t //= 2
    raise ValueError(f"row count {n} is not a multiple of 8")


def _layer(x, states, layer, stacks, cos, sin, p, batch, seq, valid, cfg):
    shift, wkv, ret, hg = states
    wkv_stack, ret_stack, hg_stack = stacks
    tm = _row_tile(x.shape[0])
    wl = p["layer"]
    u2 = _in_proj(x, p["norm_mix"], p["w_in"], wl, _row_tile(x.shape[0], IN_ROW_TILE), IN_COL_TILE)
    u = u2.reshape(batch, seq, IN_COLS)
    full = lambda c: valid if c >= seq else c
    (ca, sa), (cb, sb), (cc, sc) = cfg["rwkv"], cfg["ret"], cfg["hgrn"]
    oa, shift_new, wkv_stack = _rwkv(u, shift, wkv, layer, wkv_stack, p, ca, full(ca), sa)
    ob, ret_stack = _retention(u, cos, sin, ret, layer, ret_stack, wl, p["ret_gn"], p["hsum128"], cb, full(cb), sb)
    oc, hg_stack = _hgrn(u, p["lower"], hg, layer, hg_stack, wl, p["hgrn_gn"], p["hsum128"], cc, full(cc), sc)
    flat = lambda o: o.reshape(batch * seq, BR_W)
    x = _merge(flat(oa), flat(ob), flat(oc), u2, x, p["w_branch"], p["w_out"], wl, tm)
    x = _ffn(x, p["norm_ffn"], p["w_ffn_in"], p["w_ffn_out"], wl, p["norm_final"], p["last"], tm, FF_TILE)
    return x, shift_new, (wkv_stack, ret_stack, hg_stack)


def kernel(x_prompt, x_sample, state_rwkv_shift, state_rwkv_wkv, state_ret, state_hgrn,
           norm_mix, w_in, rwkv_mu, rwkv_w0, rwkv_w2, rwkv_a0, rwkv_a2, rwkv_g2, rwkv_kk,
           rwkv_ka, rwkv_rk, rwkv_ln_w, rwkv_ln_b, ret_gn, hgrn_lb, hgrn_gn, w_branch, w_out,
           norm_ffn, w_ffn_in, w_ffn_out, norm_final):
    depth = w_in.shape[0]
    bp, tp, _ = x_prompt.shape
    bs, ts, _ = x_sample.shape
    assert ts <= SAMPLE_CHUNK
    assert all(tp % c == 0 and bp % s == 0 for c, s in PROMPT_CFG.values())
    assert all(bs % s == 0 for _, s in SAMPLE_CFG.values())
    dt = x_prompt.dtype

    lower = _lower_bounds(hgrn_lb)
    w_in_r = jnp.concatenate(
        [w_in[:, :, A_COLS:A_COLS + B_COLS + C_COLS + GATE_COLS], w_in[:, :, :A_COLS]], axis=-1).astype(BF16)
    w_branch_b = w_branch.astype(BF16)
    w_out_b = w_out.astype(BF16)
    w_ffn_in_b = w_ffn_in.astype(BF16)
    w_ffn_out_b = w_ffn_out.astype(BF16)
    lanes = jnp.arange(BR_W)
    hsum = (lanes[:, None] // A_HD == lanes[None, :] // A_HD).astype(BF16)
    hsum128 = (lanes[:, None] // B_HD == lanes[None, :] // B_HD).astype(BF16)
    row = lambda a: a.reshape(1, -1).astype(F32)

    cos_p, sin_p = _rope_tables(jnp.arange(tp, dtype=jnp.int32))
    cos_s, sin_s = _rope_tables(PAST_LEN + jnp.arange(SAMPLE_CHUNK, dtype=jnp.int32))

    xp = x_prompt.reshape(bp * tp, D_MODEL)
    xs = jnp.pad(x_sample, ((0, 0), (0, SAMPLE_CHUNK - ts), (0, 0))).reshape(bs * SAMPLE_CHUNK, D_MODEL)
    zero_states = (jnp.zeros((1, bp, 1, A_COLS), F32), jnp.zeros((1, bp, A_HEADS, A_HD, A_HD), F32),
                   jnp.zeros((1, bp, B_HEADS, B_HD, B_HD), F32), jnp.zeros((1, bp, C_HEADS, C_HD, C_HD), F32))
    sample_states = (state_rwkv_shift.reshape(depth, bs, 1, A_COLS), state_rwkv_wkv, state_ret, state_hgrn)

    new_states = lambda b: (jnp.zeros((depth, b, A_HEADS, A_HD, A_HD), F32),
                            jnp.zeros((depth, b, B_HEADS, B_HD, B_HD), F32),
                            jnp.zeros((depth, b, C_HEADS, C_HD, C_HD), F32))
    stacks_p, stacks_s = new_states(bp), new_states(bs)
    shifts_p, shifts_s = [], []
    for l in range(depth):
        p = {
            "layer": l, "norm_mix": row(norm_mix[l]), "w_in": w_in_r, "mu": row(rwkv_mu[l]),
            "w0": row(rwkv_w0[l]), "w2": rwkv_w2[l].astype(BF16), "a0": row(rwkv_a0[l]),
            "a2": rwkv_a2[l].astype(BF16), "g2": rwkv_g2[l].astype(BF16), "kk": row(rwkv_kk[l]),
            "ka": row(rwkv_ka[l]), "rk": row(rwkv_rk[l]), "lnw": row(rwkv_ln_w[l]),
            "lnb": row(rwkv_ln_b[l]), "hsum": hsum, "hsum128": hsum128, "ret_gn": row(ret_gn[l]),
            "lower": lower[l:l + 1], "hgrn_gn": row(hgrn_gn[l]), "w_branch": w_branch_b,
            "w_out": w_out_b, "norm_ffn": row(norm_ffn[l]), "w_ffn_in": w_ffn_in_b,
            "w_ffn_out": w_ffn_out_b, "norm_final": row(norm_final), "last": l == depth - 1,
        }
        xp, sh_p, stacks_p = _layer(xp, zero_states, 0, stacks_p, cos_p, sin_p, p, bp, tp, tp, PROMPT_CFG)
        xs, sh_s, stacks_s = _layer(xs, sample_states, l, stacks_s, cos_s, sin_s, p, bs, SAMPLE_CHUNK, ts,
                                    SAMPLE_CFG)
        shifts_p.append(sh_p.reshape(bp, A_COLS))
        shifts_s.append(sh_s.reshape(bs, A_COLS))

    y_prompt = xp.reshape(bp, tp, D_MODEL)
    y_sample = xs.reshape(bs, SAMPLE_CHUNK, D_MODEL)[:, :ts]
    cast = lambda a: a.astype(dt)
    return (y_prompt, y_sample, cast(jnp.stack(shifts_p)), *map(cast, stacks_p),
            cast(jnp.stack(shifts_s)), *map(cast, stacks_s))
```

```python
import functools
import math

import jax
import jax.numpy as jnp
from jax import lax
from jax.experimental import pallas as pl
from jax.experimental.pallas import tpu as pltpu

F32 = jnp.float32
BF16 = jnp.bfloat16

D_MODEL = 1024
BR_W = 512
A_HD = 64
A_HEADS = BR_W // A_HD
LORA_W = 64
LORA_A = 64
LORA_G = 128
A_COLS = 3 * BR_W + LORA_W + LORA_A + LORA_G
B_HD = 128
B_HEADS = BR_W // B_HD
C_HD = 128
C_HEADS = BR_W // C_HD
B_COLS = 4 * BR_W
C_COLS = 4 * BR_W
GATE_COLS = 3 * D_MODEL
IN_COLS = A_COLS + B_COLS + C_COLS + GATE_COLS
D_FF = 2816
PAST_LEN = 16384
RMS_EPS = 1e-6
RWKV_GN_EPS = 64e-5
GN_EPS = 1e-5
F_FLOOR = 1e-30
ROPE_BASE = 10000.0

U_B_OFF = 0
U_C_OFF = B_COLS
U_G_OFF = B_COLS + C_COLS
U_A_OFF = B_COLS + C_COLS + GATE_COLS

SAMPLE_CHUNK = 8
PROMPT_CFG = {"rwkv": (64, 4), "ret": (128, 8), "hgrn": (128, 4)}
SAMPLE_CFG = {"rwkv": (SAMPLE_CHUNK, 4), "ret": (SAMPLE_CHUNK, 16), "hgrn": (SAMPLE_CHUNK, 16)}
RWKV_GROUP = 8
IN_ROW_TILE = 256
FF_TILE = D_FF

VMEM_LIMIT = 56 * 1024 * 1024
LANES = 128

NN = (((1,), (0,)), ((), ()))
NT = (((1,), (1,)), ((), ()))
TN = (((0,), (0,)), ((), ()))


def _bdot(a, b, dims=NN):
    return lax.dot_general(a.astype(BF16), b.astype(BF16), dims, preferred_element_type=F32)


def _sel_dot(sel, x, passes, sel_first=True):
    sel = sel.astype(BF16)
    acc = None
    for _ in range(passes):
        piece = x.astype(BF16)
        ops = (sel, piece) if sel_first else (piece, sel)
        part = lax.dot_general(*ops, NN, preferred_element_type=F32)
        acc = part if acc is None else acc + part
        x = x - piece.astype(F32)
    return acc


def _branch_dtype(chunk):
    return BF16 if chunk % 16 == 0 else F32


def _head_sum(x, same_head, passes):
    return jnp.concatenate([_sel_dot(same_head, x[:, g:g + LANES], passes, sel_first=False)
                            for g in range(0, x.shape[1], LANES)], axis=1)


def _sigmoid(x):
    return 0.5 * jnp.tanh(0.5 * x) + 0.5


def _silu(x):
    return x * _sigmoid(x)


def _cparams(sem):
    return pltpu.CompilerParams(dimension_semantics=sem, vmem_limit_bytes=VMEM_LIMIT)


def _chunk_cumsum(x, chunk):
    rows = x.shape[0]
    group = max(chunk, min(rows, 64))
    assert rows % group == 0
    tr = lax.broadcasted_iota(jnp.int32, (group, group), 0)
    tc = lax.broadcasted_iota(jnp.int32, (group, group), 1)
    sh = chunk.bit_length() - 1
    tri = jnp.where((tr >= tc) & ((tr >> sh) == (tc >> sh)), 1.0, 0.0)
    return jnp.concatenate([_sel_dot(tri, x[r:r + group], 3) for r in range(0, rows, group)], axis=0)


def _last_rows(x, chunk, bb):
    return jnp.concatenate(
        [jnp.broadcast_to(x[(i + 1) * chunk - 1:(i + 1) * chunk, :], (chunk, x.shape[1])) for i in range(bb)],
        axis=0)


def _lower_kernel(lb_ref, o_ref):
    x = lb_ref[...]
    depth = x.shape[0]
    m = x[0:1]
    for l in range(1, depth):
        m = jnp.maximum(m, x[l:l + 1])
    e = jnp.exp(x - m)
    tot = e[0:1]
    for l in range(1, depth):
        tot = tot + e[l:l + 1]
    sm = e / tot
    acc = jnp.zeros_like(m)
    for l in range(depth):
        acc = acc + sm[l:l + 1]
        o_ref[l:l + 1, :] = acc - sm[0:1]


def _lower_bounds(hgrn_lb):
    return pl.pallas_call(
        _lower_kernel, out_shape=jax.ShapeDtypeStruct(hgrn_lb.shape, F32), name="hgrn_lower",
    )(hgrn_lb.astype(F32))


def _in_proj_kernel(x_ref, g_ref, w_ref, ubc_ref, gate_ref, ua_ref):
    x = x_ref[...]
    ms = jnp.mean(x * x, axis=-1, keepdims=True)
    h = (x * lax.rsqrt(ms + RMS_EPS) * g_ref[...]).astype(BF16)
    proj = lambda lo, hi: jnp.dot(h, w_ref[:, lo:hi], preferred_element_type=F32)
    ubc_ref[...] = proj(U_B_OFF, U_G_OFF)
    gate_ref[...] = _sigmoid(proj(U_G_OFF, U_A_OFF)).astype(BF16)
    ua_ref[...] = proj(U_A_OFF, IN_COLS)


def _in_proj(x, g, w, layer, tm):
    n, d = x.shape
    rows = lambda width: pl.BlockSpec((tm, width), lambda i: (i, 0))
    return pl.pallas_call(
        _in_proj_kernel,
        grid=(n // tm,),
        in_specs=[
            rows(d),
            pl.BlockSpec((1, d), lambda i: (0, 0)),
            pl.BlockSpec((None, d, IN_COLS), lambda i: (layer, 0, 0), pipeline_mode=pl.Buffered(1)),
        ],
        out_specs=[rows(B_COLS + C_COLS), rows(GATE_COLS), rows(A_COLS)],
        out_shape=[
            jax.ShapeDtypeStruct((n, B_COLS + C_COLS), F32),
            jax.ShapeDtypeStruct((n, GATE_COLS), BF16),
            jax.ShapeDtypeStruct((n, A_COLS), F32),
        ],
        compiler_params=_cparams(("parallel",)),
        name="in_proj",
    )(x, g, w)


def _rwkv_kernel(u_ref, shift_ref, s0_ref, mu_ref, w0_ref, w2_ref, a0_ref, a2_ref, g2_ref,
                 kk_ref, ka_ref, rk_ref, lnw_ref, lnb_ref, hsum_ref, stack_ref,
                 o_ref, shift_out_ref, s_out_ref, st_scr, prev_scr, o_scr, *, chunk, valid, n_chunks, bb, gb):
    n = pl.program_id(1)

    @pl.when(n == 0)
    def _():
        for i in range(bb):
            for h in range(A_HEADS):
                st_scr[i, h] = s0_ref[i, h].T
        prev_scr[...] = shift_ref[...]

    for g0 in range(0, bb, gb):
        seqs = pl.ds(g0, gb)
        _rwkv_group(u_ref.at[seqs], mu_ref, w0_ref, w2_ref, a0_ref, a2_ref, g2_ref, kk_ref, ka_ref, rk_ref,
                    lnw_ref, lnb_ref, hsum_ref, o_ref.at[seqs], st_scr.at[seqs], prev_scr.at[seqs],
                    o_scr.at[pl.ds(g0 * chunk, gb * chunk)], chunk=chunk, valid=valid, bb=gb)

    @pl.when(n == n_chunks - 1)
    def _():
        last = (valid - 1) if n_chunks == 1 else (chunk - 1)
        for i in range(bb):
            shift_out_ref[i] = u_ref[i, last:last + 1, :]
            for h in range(A_HEADS):
                s_out_ref[i, h] = st_scr[i, h].T


def _rwkv_group(u_ref, mu_ref, w0_ref, w2_ref, a0_ref, a2_ref, g2_ref, kk_ref, ka_ref, rk_ref,
                lnw_ref, lnb_ref, hsum_ref, o_ref, st_scr, prev_scr, o_scr, *, chunk, valid, bb):
    rows = bb * chunk
    units = [(i, h) for i in range(bb) for h in range(A_HEADS)]
    u = u_ref[...].reshape(rows, A_COLS)
    row = lax.broadcasted_iota(jnp.int32, (rows, 1), 0)
    pos = row & (chunk - 1)
    prev = jnp.concatenate([jnp.broadcast_to(prev_scr[i], (chunk, A_COLS)) for i in range(bb)], axis=0)
    shifted = jnp.where(pos == 0, prev, pltpu.roll(u, 1, axis=0))
    xm = u + mu_ref[...] * (shifted - u)
    for i in range(bb):
        prev_scr[i] = u[(i + 1) * chunk - 1:(i + 1) * chunk, :]

    r = xm[:, 0:BR_W]
    k = xm[:, BR_W:2 * BR_W]
    v = xm[:, 2 * BR_W:3 * BR_W]
    wl = xm[:, 3 * BR_W:3 * BR_W + LORA_W]
    al = xm[:, 3 * BR_W + LORA_W:3 * BR_W + LORA_W + LORA_A]
    gl = xm[:, 3 * BR_W + LORA_W + LORA_A:A_COLS]

    z = w0_ref[...] + _bdot(jnp.tanh(wl), w2_ref[...])
    ld = -math.exp(-0.5) * _sigmoid(z)
    a = _sigmoid(a0_ref[...] + _bdot(al, a2_ref[...]))
    g = _bdot(_sigmoid(gl), g2_ref[...])
    kk = k * kk_ref[...]
    ss = _head_sum(kk * kk, hsum_ref[...], 2)
    kk = kk * lax.rsqrt(jnp.maximum(ss, 1e-24))
    k = k * (1.0 + (a - 1.0) * ka_ref[...])
    if valid < chunk:
        ok = pos < valid
        ld = jnp.where(ok, ld, 0.0)
        kk = jnp.where(ok, kk, 0.0)
        k = jnp.where(ok, k, 0.0)
    bv = kk * a

    cl = _chunk_cumsum(ld, chunk)
    cl_last = _last_rows(cl, chunk, bb)
    e_ncl = jnp.exp(-cl)
    e_rem = jnp.exp(cl_last - cl)
    alpha_t = -kk * jnp.exp(cl - ld)
    r_t = r * jnp.exp(cl)
    b_bar = bv * e_ncl
    k_bar = k * e_ncl
    b_hat = bv * e_rem
    k_hat = k * e_rem
    e_last = jnp.exp(cl_last)
    rkk = r * k * rk_ref[...]

    qr = lax.broadcasted_iota(jnp.int32, (2 * chunk, 2 * chunk), 0)
    qc = lax.broadcasted_iota(jnp.int32, (2 * chunk, 2 * chunk), 1)
    qt, qs = qr & (chunk - 1), qc & (chunk - 1)
    keep = (qt > qs) | ((qr >= chunk) & (qt == qs))

    def blk(t, i, h):
        return t[i * chunk:(i + 1) * chunk, h * A_HD:(h + 1) * A_HD]

    aa, st0, vv, x, p = {}, {}, {}, {}, {}
    for i, h in units:
        lhs = jnp.concatenate([blk(alpha_t, i, h), blk(r_t, i, h)], axis=0)
        rhs = jnp.concatenate([blk(b_bar, i, h), blk(k_bar, i, h)], axis=0)
        aa[i, h] = jnp.where(keep, _bdot(lhs, rhs, NT), 0.0)
        st0[i, h] = st_scr[i, h]
        vv[i, h] = blk(v, i, h)
    for i, h in units:
        lhs = jnp.concatenate([blk(alpha_t, i, h), aa[i, h][0:chunk, chunk:2 * chunk]], axis=1)
        x[i, h] = _bdot(lhs, jnp.concatenate([st0[i, h], vv[i, h]], axis=0))
        p[i, h] = aa[i, h][0:chunk, 0:chunk]
    for i, h in units:
        x[i, h] = x[i, h] + _bdot(p[i, h], x[i, h])
    span = 2
    while span < chunk:
        for i, h in units:
            p[i, h] = _bdot(p[i, h], p[i, h])
        for i, h in units:
            x[i, h] = x[i, h] + _bdot(p[i, h], x[i, h])
        span *= 2
    for i, h in units:
        sl = slice(h * A_HD, (h + 1) * A_HD)
        lhs = jnp.concatenate([blk(r_t, i, h), aa[i, h][chunk:2 * chunk, :]], axis=1)
        o = _bdot(lhs, jnp.concatenate([st0[i, h], x[i, h], vv[i, h]], axis=0))
        upd = _bdot(jnp.concatenate([blk(b_hat, i, h), blk(k_hat, i, h)], axis=0),
                    jnp.concatenate([x[i, h], vv[i, h]], axis=0), TN)
        dec = jnp.broadcast_to(e_last[i * chunk:i * chunk + 1, sl], (A_HD, A_HD)).T
        st_scr[i, h] = st0[i, h] * dec + upd
        o_scr[i * chunk:(i + 1) * chunk, sl] = o

    hsum = hsum_ref[...]
    o = o_scr[...]
    d = o - _head_sum(o, hsum, 2) * (1.0 / A_HD)
    var = _head_sum(d * d, hsum, 1) * (1.0 / A_HD)
    o = d * lax.rsqrt(var + RWKV_GN_EPS) * lnw_ref[...] + lnb_ref[...]
    o = o + _head_sum(rkk, hsum, 1) * v
    o_ref[...] = (o * g).astype(o_ref.dtype).reshape(bb, chunk, BR_W)


def _rwkv(u, shift, wkv, layer, stack, p, chunk, valid, bb):
    batch, seq, _ = u.shape
    n_chunks = seq // chunk
    assert valid == chunk or n_chunks == 1
    a_blk = 0
    vec = lambda w: pl.BlockSpec((1, w), lambda b, n: (0, 0))
    full = lambda s: pl.BlockSpec(s, lambda b, n: (0,) * len(s))
    gb = math.gcd(bb, RWKV_GROUP)
    kern = functools.partial(_rwkv_kernel, chunk=chunk, valid=valid, n_chunks=n_chunks, bb=bb, gb=gb)
    return pl.pallas_call(
        kern,
        grid=(batch // bb, n_chunks),
        in_specs=[
            pl.BlockSpec((bb, chunk, A_COLS), lambda b, n: (b, n, a_blk)),
            pl.BlockSpec((None, bb, 1, A_COLS), lambda b, n: (layer, b, 0, 0)),
            pl.BlockSpec((None, bb, A_HEADS, A_HD, A_HD), lambda b, n: (layer, b, 0, 0, 0)),
            vec(A_COLS), vec(BR_W), full((LORA_W, BR_W)), vec(BR_W), full((LORA_A, BR_W)),
            full((LORA_G, BR_W)), vec(BR_W), vec(BR_W), vec(BR_W), vec(BR_W), vec(BR_W),
            pl.BlockSpec((LANES, LANES), lambda b, n: (0, 0)),
            pl.BlockSpec(memory_space=pl.ANY),
        ],
        input_output_aliases={15: 2},
        out_specs=[
            pl.BlockSpec((bb, chunk, BR_W), lambda b, n: (b, n, 0)),
            pl.BlockSpec((bb, 1, A_COLS), lambda b, n: (b, 0, 0)),
            pl.BlockSpec((None, bb, A_HEADS, A_HD, A_HD), lambda b, n: (p["layer"], b, 0, 0, 0)),
        ],
        out_shape=[
            jax.ShapeDtypeStruct((batch, seq, BR_W), _branch_dtype(chunk)),
            jax.ShapeDtypeStruct((batch, 1, A_COLS), F32),
            jax.ShapeDtypeStruct(stack.shape, F32),
        ],
        scratch_shapes=[pltpu.VMEM((bb, A_HEADS, A_HD, A_HD), F32), pltpu.VMEM((bb, 1, A_COLS), F32),
                        pltpu.VMEM((bb * chunk, BR_W), F32)],
        compiler_params=_cparams(("parallel", "arbitrary")),
        name="rwkv7",
    )(u, shift, wkv, p["mu"], p["w0"], p["w2"], p["a0"], p["a2"], p["g2"], p["kk"], p["ka"],
      p["rk"], p["lnw"], p["lnb"], p["hsum"], stack)


def _ret_kernel(u_ref, cos_ref, sin_ref, s0_ref, gn_ref, hsum_ref, stack_ref, o_ref, s_out_ref, s_scr, o_scr,
                *, chunk, valid, n_chunks, bb):
    n = pl.program_id(1)
    units = [(i, h) for i in range(bb) for h in range(B_HEADS)]

    @pl.when(n == 0)
    def _():
        s_scr[...] = s0_ref[...]

    cos = cos_ref[...]
    sin = sin_ref[...]
    tr = lax.broadcasted_iota(jnp.int32, (chunk, chunk), 0)
    tc = lax.broadcasted_iota(jnp.int32, (chunk, chunk), 1)
    diff = (tr - tc).astype(F32)
    causal = tr >= tc
    pos = lax.broadcasted_iota(jnp.int32, (chunk, 1), 0)
    posf = pos.astype(F32)
    lgs = [math.log(1.0 - 2.0 ** (-5.0 - h)) for h in range(B_HEADS)]
    intra = [jnp.where(causal, jnp.exp(lg * jnp.maximum(diff, 0.0)), 0.0) for lg in lgs]
    q_dec = [jnp.exp(lg * (posf + 1.0)) for lg in lgs]
    k_dec = [jnp.where(pos < valid, jnp.exp(lg * (valid - 1.0 - posf)), 0.0) for lg in lgs]
    s_dec = [math.exp(lg * valid) for lg in lgs]

    q, k, v, s0, sc = {}, {}, {}, {}, {}
    for i, h in units:
        ui = u_ref[i]
        qh = ui[:, h * B_HD:(h + 1) * B_HD]
        kh = ui[:, BR_W + h * B_HD:BR_W + (h + 1) * B_HD]
        q[i, h] = qh * cos + pltpu.roll(qh, B_HD // 2, axis=1) * sin
        k[i, h] = (kh * cos + pltpu.roll(kh, B_HD // 2, axis=1) * sin) * (B_HD ** -0.5)
        v[i, h] = ui[:, 2 * BR_W + h * B_HD:2 * BR_W + (h + 1) * B_HD]
        s0[i, h] = s_scr[i, h]
    for i, h in units:
        sc[i, h] = _bdot(q[i, h], k[i, h], NT) * intra[h]
    for i, h in units:
        sl = slice(h * B_HD, (h + 1) * B_HD)
        o = _bdot(sc[i, h], v[i, h]) + _bdot(q[i, h] * q_dec[h], s0[i, h])
        s_scr[i, h] = s0[i, h] * s_dec[h] + _bdot(k[i, h] * k_dec[h], v[i, h], TN)
        o_scr[i * chunk:(i + 1) * chunk, sl] = o

    hsum = hsum_ref[...]
    o = o_scr[...]
    d = o - _head_sum(o, hsum, 2) * (1.0 / B_HD)
    var = _head_sum(d * d, hsum, 1) * (1.0 / B_HD)
    gt = u_ref[:, :, 3 * BR_W:4 * BR_W].reshape(bb * chunk, BR_W)
    o = d * lax.rsqrt(var + GN_EPS) * gn_ref[...] * _silu(gt)
    o_ref[...] = o.astype(o_ref.dtype).reshape(bb, chunk, BR_W)

    @pl.when(n == n_chunks - 1)
    def _():
        s_out_ref[...] = s_scr[...]


def _retention(u, cos, sin, state, layer, stack, out_layer, gn, hsum, chunk, valid, bb):
    batch, seq, _ = u.shape
    n_chunks = seq // chunk
    assert valid == chunk or n_chunks == 1
    kern = functools.partial(_ret_kernel, chunk=chunk, valid=valid, n_chunks=n_chunks, bb=bb)
    return pl.pallas_call(
        kern,
        grid=(batch // bb, n_chunks),
        in_specs=[
            pl.BlockSpec((bb, chunk, B_COLS), lambda b, n: (b, n, U_B_OFF // B_COLS)),
            pl.BlockSpec((chunk, B_HD), lambda b, n: (n, 0)),
            pl.BlockSpec((chunk, B_HD), lambda b, n: (n, 0)),
            pl.BlockSpec((None, bb, B_HEADS, B_HD, B_HD), lambda b, n: (layer, b, 0, 0, 0)),
            pl.BlockSpec((1, BR_W), lambda b, n: (0, 0)),
            pl.BlockSpec((LANES, LANES), lambda b, n: (0, 0)),
            pl.BlockSpec(memory_space=pl.ANY),
        ],
        input_output_aliases={6: 1},
        out_specs=[
            pl.BlockSpec((bb, chunk, BR_W), lambda b, n: (b, n, 0)),
            pl.BlockSpec((None, bb, B_HEADS, B_HD, B_HD), lambda b, n: (out_layer, b, 0, 0, 0)),
        ],
        out_shape=[
            jax.ShapeDtypeStruct((batch, seq, BR_W), _branch_dtype(chunk)),
            jax.ShapeDtypeStruct(stack.shape, F32),
        ],
        scratch_shapes=[pltpu.VMEM((bb, B_HEADS, B_HD, B_HD), F32), pltpu.VMEM((bb * chunk, BR_W), F32)],
        compiler_params=_cparams(("parallel", "arbitrary")),
        name="retention",
    )(u, cos, sin, state, gn, hsum, stack)


def _hgrn_kernel(u_ref, lb_ref, s0_ref, gn_ref, hsum_ref, stack_ref, o_ref, s_out_ref, st_scr, o_scr,
                 *, chunk, valid, n_chunks, bb):
    n = pl.program_id(1)
    rows = bb * chunk
    units = [(i, h) for i in range(bb) for h in range(C_HEADS)]

    @pl.when(n == 0)
    def _():
        for i, h in units:
            st_scr[i, h] = s0_ref[i, h].T

    lb = lb_ref[...]
    row = lax.broadcasted_iota(jnp.int32, (rows, 1), 0)
    sig_f = _sigmoid(u_ref[:, :, BR_W:2 * BR_W].reshape(rows, BR_W))
    lf = jnp.log(jnp.maximum(lb + (1.0 - lb) * sig_f, F_FLOOR))
    kc = (1.0 - lb) * (1.0 - sig_f)
    if valid < chunk:
        ok = (row & (chunk - 1)) < valid
        lf = jnp.where(ok, lf, 0.0)
        kc = jnp.where(ok, kc, 0.0)
    bc = _chunk_cumsum(lf, chunk)
    b_last = _last_rows(bc, chunk, bb)
    e_b = jnp.exp(bc)
    e_last = jnp.exp(b_last)
    k_hat = kc * jnp.exp(b_last - bc)
    q_all = u_ref[:, :, 0:BR_W].reshape(rows, BR_W)
    tr =lax.broadcasted_iota(jnp.int32, (chunk, chunk), 0)
    tc = lax.broadcasted_iota(jnp.int32, (chunk, chunk), 1)
    differ = tr ^ tc
    sub8 = lax.broadcasted_iota(jnp.int32, (8, 1), 0)
    width = min(chunk, C_HD)

    def blk(t, i, h):
        return t[i * chunk:(i + 1) * chunk, h * C_HD:(h + 1) * C_HD]

    def boundary(c):
        pieces = []
        for base in range(0, rows, 8):
            if 2 * c <= 8:
                parts = [jnp.broadcast_to(bc[base + m + c - 1:base + m + c, :], (8, BR_W))
                         for m in range(0, 8, 2 * c)]
                piece = parts[-1]
                for idx in range(len(parts) - 2, -1, -1):
                    piece = jnp.where(sub8 < (idx + 1) * 2 * c, parts[idx], piece)
            else:
                at = (base // (2 * c)) * (2 * c) + c - 1
                piece = jnp.broadcast_to(bc[at:at + 1, :], (8, BR_W))
            pieces.append(piece)
        return jnp.concatenate(pieces, axis=0)

    q, v, st, scores = {}, {}, {}, {}
    diag = _head_sum(q_all * kc, hsum_ref[...], 2)
    for i, h in units:
        v[i, h] = u_ref[i, :, 2 * BR_W + h * C_HD:2 * BR_W + (h + 1) * C_HD]
        q[i, h] = blk(q_all, i, h)
        st[i, h] = st_scr[i, h]
        scores[i, h] = jnp.where(tr == tc, diag[i * chunk:(i + 1) * chunk, h * C_HD:h * C_HD + width], 0.0)
    c = 1
    while c < chunk:
        w = jnp.exp(-jnp.abs(bc - boundary(c)))
        qz = q_all * w
        kz = kc * w
        here = (tr > tc) & (differ >= c) & (differ < 2 * c)
        for i, h in units:
            scores[i, h] = jnp.where(here, _bdot(blk(qz, i, h), blk(kz, i, h), NT), scores[i, h])
        c *= 2

    for i, h in units:
        sl = slice(h * C_HD, (h + 1) * C_HD)
        o = _bdot(scores[i, h], v[i, h]) + _bdot(q[i, h] * blk(e_b, i, h), st[i, h], NT)
        st_scr[i, h] = st[i, h] * e_last[i * chunk:i * chunk + 1, sl] + _bdot(v[i, h], blk(k_hat, i, h), TN)
        o_scr[i * chunk:(i + 1) * chunk, sl] = o

    o = o_scr[...]
    ms = _head_sum(o * o, hsum_ref[...], 1) * (1.0 / C_HD)
    gt = u_ref[:, :, 3 * BR_W:4 * BR_W].reshape(rows, BR_W)
    o = o * lax.rsqrt(ms + GN_EPS) * gn_ref[...] * _silu(gt)
    o_ref[...] = o.astype(o_ref.dtype).reshape(bb, chunk, BR_W)

    @pl.when(n == n_chunks - 1)
    def _():
        for i, h in units:
            s_out_ref[i, h] = st_scr[i, h].T


def _hgrn(u, lb, state, layer, stack, out_layer, gn, hsum, chunk, valid, bb):
    batch, seq, _ = u.shape
    n_chunks = seq // chunk
    assert valid == chunk or n_chunks == 1
    kern = functools.partial(_hgrn_kernel, chunk=chunk, valid=valid, n_chunks=n_chunks, bb=bb)
    return pl.pallas_call(
        kern,
        grid=(batch // bb, n_chunks),
        in_specs=[
            pl.BlockSpec((bb, chunk, C_COLS), lambda b, n: (b, n, U_C_OFF // C_COLS)),
            pl.BlockSpec((1, BR_W), lambda b, n: (0, 0)),
            pl.BlockSpec((None, bb, C_HEADS, C_HD, C_HD), lambda b, n: (layer, b, 0, 0, 0)),
            pl.BlockSpec((1, BR_W), lambda b, n: (0, 0)),
            pl.BlockSpec((LANES, LANES), lambda b, n: (0, 0)),
            pl.BlockSpec(memory_space=pl.ANY),
        ],
        input_output_aliases={5: 1},
        out_specs=[
            pl.BlockSpec((bb, chunk, BR_W), lambda b, n: (b, n, 0)),
            pl.BlockSpec((None, bb, C_HEADS, C_HD, C_HD), lambda b, n: (out_layer, b, 0, 0, 0)),
        ],
        out_shape=[
            jax.ShapeDtypeStruct((batch, seq, BR_W), _branch_dtype(chunk)),
            jax.ShapeDtypeStruct(stack.shape, F32),
        ],
        scratch_shapes=[pltpu.VMEM((bb, C_HEADS, C_HD, C_HD), F32), pltpu.VMEM((bb * chunk, BR_W), F32)],
        compiler_params=_cparams(("parallel", "arbitrary")),
        name="hgrn2",
    )(u, lb, state, gn, hsum, stack)


def _merge_kernel(oa_ref, ob_ref, oc_ref, g0_ref, g1_ref, g2_ref, x_ref, wb_ref, wo_ref, o_ref):
    m = g0_ref[...] * _bdot(oa_ref[...], wb_ref[0])
    m = m + g1_ref[...] * _bdot(ob_ref[...], wb_ref[1])
    m = m + g2_ref[...] * _bdot(oc_ref[...], wb_ref[2])
    o_ref[...] = x_ref[...] + _bdot(m, wo_ref[...])


def _merge(oa, ob, oc, u, x, wb, wo, layer, tm):
    n = x.shape[0]
    br = pl.BlockSpec((tm, BR_W), lambda i: (i, 0))
    gate = lambda c: pl.BlockSpec((tm, D_MODEL), lambda i: (i, c))
    return pl.pallas_call(
        _merge_kernel,
        grid=(n // tm,),
        in_specs=[br, br, br, gate(0), gate(1), gate(2),
                  pl.BlockSpec((tm, D_MODEL), lambda i: (i, 0)),
                  pl.BlockSpec((None, 3, BR_W, D_MODEL), lambda i: (layer, 0, 0, 0)),
                  pl.BlockSpec((None, D_MODEL, D_MODEL), lambda i: (layer, 0, 0))],
        out_specs=pl.BlockSpec((tm, D_MODEL), lambda i: (i, 0)),
        out_shape=jax.ShapeDtypeStruct((n, D_MODEL), F32),
        compiler_params=_cparams(("parallel",)),
        name="merge",
    )(oa, ob, oc, u, u, u, x, wb, wo)


def _ffn_kernel(x_ref, g_ref, wg_ref, wu_ref, wo_ref, gfin_ref, o_ref, h_ref, *, n_ff, final):
    @pl.when(pl.program_id(1) == 0)
    def _():
        x = x_ref[...]
        ms = jnp.mean(x * x, axis=-1, keepdims=True)
        h_ref[...] = (x * lax.rsqrt(ms + RMS_EPS) * g_ref[...]).astype(BF16)
        o_ref[...] = x

    h = h_ref[...]
    gt = jnp.dot(h, wg_ref[...], preferred_element_type=F32)
    up = jnp.dot(h, wu_ref[...], preferred_element_type=F32)
    o_ref[...] += _bdot(_silu(gt) * up, wo_ref[...])

    if final:
        @pl.when(pl.program_id(1) == n_ff - 1)
        def _():
            x = o_ref[...]
            ms = jnp.mean(x * x, axis=-1, keepdims=True)
            o_ref[...] = x * lax.rsqrt(ms + RMS_EPS) * gfin_ref[...]


def _ffn(x, g, w_in, w_out, layer, g_final, final, tm, tf):
    n = x.shape[0]
    n_ff = D_FF // tf
    resident = {"pipeline_mode": pl.Buffered(1)} if n_ff == 1 else {}
    return pl.pallas_call(
        functools.partial(_ffn_kernel, n_ff=n_ff, final=final),
        grid=(n // tm, n_ff),
        in_specs=[
            pl.BlockSpec((tm, D_MODEL), lambda i, j: (i, 0)),
            pl.BlockSpec((1, D_MODEL), lambda i, j: (0, 0)),
            pl.BlockSpec((None, D_MODEL, tf), lambda i, j: (layer, 0, j), **resident),
            pl.BlockSpec((None, D_MODEL, tf), lambda i, j: (layer, 0, n_ff + j), **resident),
            pl.BlockSpec((None, tf, D_MODEL), lambda i, j: (layer, j, 0), **resident),
            pl.BlockSpec((1, D_MODEL), lambda i, j: (0, 0)),
        ],
        out_specs=pl.BlockSpec((tm, D_MODEL), lambda i, j: (i, 0)),
        out_shape=jax.ShapeDtypeStruct((n, D_MODEL), F32),
        scratch_shapes=[pltpu.VMEM((tm, D_MODEL), BF16)],
        compiler_params=_cparams(("parallel", "arbitrary")),
        name="ffn",
    )(x, g, w_in, w_in, w_out, g_final)


def _rope_tables(pos):
    half = B_HD // 2
    inv = ROPE_BASE ** (-jnp.arange(half, dtype=F32) / half)
    ang = pos.astype(F32)[:, None] * inv[None, :]
    cos, sin = jnp.cos(ang), jnp.sin(ang)
    return jnp.concatenate([cos, cos], axis=-1), jnp.concatenate([-sin, sin], axis=-1)


def _row_tile(n, largest=512):
    t = largest
    while t >= 8:
        if n % t == 0:
            return t
        t //= 2
    raise ValueError(f"row count {n} is not a multiple of 8")


def _layer(x, states, layer, stacks, cos, sin, p, batch, seq, valid, cfg):
    shift, wkv, ret, hg = states
    wkv_stack, ret_stack, hg_stack = stacks
    tm = _row_tile(x.shape[0])
    wl = p["layer"]
    ubc, gates, ua = _in_proj(x, p["norm_mix"], p["w_in"], wl, _row_tile(x.shape[0], IN_ROW_TILE))
    ubc = ubc.reshape(batch, seq, B_COLS + C_COLS)
    ua = ua.reshape(batch, seq, A_COLS)
    full = lambda c: valid if c >= seq else c
    (ca, sa), (cb, sb), (cc, sc) = cfg["rwkv"], cfg["ret"], cfg["hgrn"]
    oa, shift_new, wkv_stack = _rwkv(ua, shift, wkv, layer, wkv_stack, p, ca, full(ca), sa)
    ob, ret_stack = _retention(ubc, cos, sin, ret, layer, ret_stack, wl, p["ret_gn"], p["hsum128"], cb, full(cb), sb)
    oc, hg_stack = _hgrn(ubc, p["lower"], hg, layer, hg_stack, wl, p["hgrn_gn"], p["hsum128"], cc, full(cc), sc)
    flat = lambda o: o.reshape(batch * seq, BR_W)
    x = _merge(flat(oa), flat(ob), flat(oc), gates, x, p["w_branch"], p["w_out"], wl, tm)
    x = _ffn(x, p["norm_ffn"], p["w_ffn_in"], p["w_ffn_out"], wl, p["norm_final"], p["last"], tm, FF_TILE)
    return x, shift_new, (wkv_stack, ret_stack, hg_stack)


def kernel(x_prompt, x_sample, state_rwkv_shift, state_rwkv_wkv, state_ret, state_hgrn,
           norm_mix, w_in, rwkv_mu, rwkv_w0, rwkv_w2, rwkv_a0, rwkv_a2, rwkv_g2, rwkv_kk,
           rwkv_ka, rwkv_rk, rwkv_ln_w, rwkv_ln_b, ret_gn, hgrn_lb, hgrn_gn, w_branch, w_out,
           norm_ffn, w_ffn_in, w_ffn_out, norm_final):
    depth = w_in.shape[0]
    bp, tp, _ = x_prompt.shape
    bs, ts, _ = x_sample.shape
    assert ts <= SAMPLE_CHUNK
    assert all(tp % c == 0 and bp % s == 0 for c, s in PROMPT_CFG.values())
    assert all(bs % s == 0 for _, s in SAMPLE_CFG.values())
    dt = x_prompt.dtype

    lower = _lower_bounds(hgrn_lb)
    w_in_r = jnp.concatenate(
        [w_in[:, :, A_COLS:A_COLS + B_COLS + C_COLS + GATE_COLS], w_in[:, :, :A_COLS]], axis=-1).astype(BF16)
    w_branch_b = w_branch.astype(BF16)
    w_out_b = w_out.astype(BF16)
    w_ffn_in_b = w_ffn_in.astype(BF16)
    w_ffn_out_b = w_ffn_out.astype(BF16)
    lanes = jnp.arange(BR_W)
    hsum = (lanes[:, None] // A_HD == lanes[None, :] // A_HD).astype(BF16)
    hsum128 = (lanes[:, None] // B_HD == lanes[None, :] // B_HD).astype(BF16)
    row = lambda a: a.reshape(1, -1).astype(F32)

    cos_p, sin_p = _rope_tables(jnp.arange(tp, dtype=jnp.int32))
    cos_s, sin_s = _rope_tables(PAST_LEN + jnp.arange(SAMPLE_CHUNK, dtype=jnp.int32))

    xp = x_prompt.reshape(bp * tp, D_MODEL)
    xs = jnp.pad(x_sample, ((0, 0), (0, SAMPLE_CHUNK - ts), (0, 0))).reshape(bs * SAMPLE_CHUNK, D_MODEL)
    zero_states = (jnp.zeros((1, bp, 1, A_COLS), F32), jnp.zeros((1, bp, A_HEADS, A_HD, A_HD), F32),
                   jnp.zeros((1, bp, B_HEADS, B_HD, B_HD), F32), jnp.zeros((1, bp, C_HEADS, C_HD, C_HD), F32))
    sample_states = (state_rwkv_shift.reshape(depth, bs, 1, A_COLS), state_rwkv_wkv, state_ret, state_hgrn)

    new_states = lambda b: (jnp.zeros((depth, b, A_HEADS, A_HD, A_HD), F32),
                            jnp.zeros((depth, b, B_HEADS, B_HD, B_HD), F32),
                            jnp.zeros((depth, b, C_HEADS, C_HD, C_HD), F32))
    stacks_p, stacks_s = new_states(bp), new_states(bs)
    shifts_p, shifts_s = [], []
    for l in range(depth):
        p = {
            "layer": l, "norm_mix": row(norm_mix[l]), "w_in": w_in_r, "mu": row(rwkv_mu[l]),
            "w0": row(rwkv_w0[l]), "w2": rwkv_w2[l].astype(BF16), "a0": row(rwkv_a0[l]),
            "a2": rwkv_a2[l].astype(BF16), "g2": rwkv_g2[l].astype(BF16), "kk": row(rwkv_kk[l]),
            "ka": row(rwkv_ka[l]), "rk": row(rwkv_rk[l]), "lnw": row(rwkv_ln_w[l]),
            "lnb": row(rwkv_ln_b[l]), "hsum": hsum, "hsum128": hsum128, "ret_gn": row(ret_gn[l]),
            "lower": lower[l:l + 1], "hgrn_gn": row(hgrn_gn[l]), "w_branch": w_branch_b,
            "w_out": w_out_b, "norm_ffn": row(norm_ffn[l]), "w_ffn_in": w_ffn_in_b,
            "w_ffn_out": w_ffn_out_b, "norm_final": row(norm_final), "last": l == depth - 1,
        }
        xp, sh_p, stacks_p = _layer(xp, zero_states, 0, stacks_p, cos_p, sin_p, p, bp, tp, tp, PROMPT_CFG)
        xs, sh_s, stacks_s = _layer(xs, sample_states, l, stacks_s, cos_s, sin_s, p, bs, SAMPLE_CHUNK, ts,
                                    SAMPLE_CFG)
        shifts_p.append(sh_p.reshape(bp, A_COLS))
        shifts_s.append(sh_s.reshape(bs, A_COLS))

    y_prompt = xp.reshape(bp, tp, D_MODEL)
    y_sample = xs.reshape(bs, SAMPLE_CHUNK, D_MODEL)[:, :ts]
    cast = lambda a: a.astype(dt)
    return (y_prompt, y_sample, cast(jnp.stack(shifts_p)), *map(cast, stacks_p),
            cast(jnp.stack(shifts_s)), *map(cast, stacks_s))
```

```python
import functools
import math

import jax
import jax.numpy as jnp
from jax import lax
from jax.experimental import pallas as pl
from jax.experimental.pallas import tpu as pltpu

F32 = jnp.float32
BF16 = jnp.bfloat16

D_MODEL = 1024
BR_W = 512
A_HD = 64
A_HEADS = BR_W // A_HD
LORA_W = 64
LORA_A = 64
LORA_G = 128
A_COLS = 3 * BR_W + LORA_W + LORA_A + LORA_G
B_HD = 128
B_HEADS = BR_W // B_HD
C_HD = 128
C_HEADS = BR_W // C_HD
B_COLS = 4 * BR_W
C_COLS = 4 * BR_W
GATE_COLS = 3 * D_MODEL
IN_COLS = A_COLS + B_COLS + C_COLS + GATE_COLS
D_FF = 2816
PAST_LEN = 16384
RMS_EPS = 1e-6
RWKV_GN_EPS = 64e-5
GN_EPS = 1e-5
F_FLOOR = 1e-30
ROPE_BASE = 10000.0

W_B_OFF = A_COLS
W_G_OFF = A_COLS + B_COLS + C_COLS

SAMPLE_CHUNK = 8
PROMPT_CFG = {"rwkv": (64, 4), "ret": (128, 8), "hgrn": (128, 4)}
SAMPLE_CFG = {"rwkv": (SAMPLE_CHUNK, 4), "ret": (SAMPLE_CHUNK, 16), "hgrn": (SAMPLE_CHUNK, 16)}
RWKV_GROUP = 8
IN_ROW_TILE = 256
FF_TILE = D_FF

VMEM_LIMIT = 56 * 1024 * 1024
LANES = 128

NN = (((1,), (0,)), ((), ()))
NT = (((1,), (1,)), ((), ()))
TN = (((0,), (0,)), ((), ()))


def _bdot(a, b, dims=NN):
    return lax.dot_general(a.astype(BF16), b.astype(BF16), dims, preferred_element_type=F32)


def _sel_dot(sel, x, passes, sel_first=True):
    sel = sel.astype(BF16)
    acc = None
    for _ in range(passes):
        piece = x.astype(BF16)
        ops = (sel, piece) if sel_first else (piece, sel)
        part = lax.dot_general(*ops, NN, preferred_element_type=F32)
        acc = part if acc is None else acc + part
        x = x - piece.astype(F32)
    return acc


def _branch_dtype(chunk):
    return BF16 if chunk % 16 == 0 else F32


def _head_sum(x, same_head, passes):
    return jnp.concatenate([_sel_dot(same_head, x[:, g:g + LANES], passes, sel_first=False)
                            for g in range(0, x.shape[1], LANES)], axis=1)


def _sigmoid(x):
    return 0.5 * jnp.tanh(0.5 * x) + 0.5


def _silu(x):
    return x * _sigmoid(x)


def _cparams(sem):
    return pltpu.CompilerParams(dimension_semantics=sem, vmem_limit_bytes=VMEM_LIMIT)


def _chunk_cumsum(x, chunk):
    rows = x.shape[0]
    group = max(chunk, min(rows, 64))
    assert rows % group == 0
    tr = lax.broadcasted_iota(jnp.int32, (group, group), 0)
    tc = lax.broadcasted_iota(jnp.int32, (group, group), 1)
    sh = chunk.bit_length() - 1
    tri = jnp.where((tr >= tc) & ((tr >> sh) == (tc >> sh)), 1.0, 0.0)
    return jnp.concatenate([_sel_dot(tri, x[r:r + group], 3) for r in range(0, rows, group)], axis=0)


def _last_rows(x, chunk, bb):
    return jnp.concatenate(
        [jnp.broadcast_to(x[(i + 1) * chunk - 1:(i + 1) * chunk, :], (chunk, x.shape[1])) for i in range(bb)],
        axis=0)


def _lower_kernel(lb_ref, o_ref):
    x = lb_ref[...]
    depth = x.shape[0]
    m = x[0:1]
    for l in range(1, depth):
        m = jnp.maximum(m, x[l:l + 1])
    e = jnp.exp(x - m)
    tot = e[0:1]
    for l in range(1, depth):
        tot = tot + e[l:l + 1]
    sm = e / tot
    acc = jnp.zeros_like(m)
    for l in range(depth):
        acc = acc + sm[l:l + 1]
        o_ref[l:l + 1, :] = acc - sm[0:1]


def _lower_bounds(hgrn_lb):
    return pl.pallas_call(
        _lower_kernel, out_shape=jax.ShapeDtypeStruct(hgrn_lb.shape, F32), name="hgrn_lower",
    )(hgrn_lb.astype(F32))


def _in_proj_kernel(x_ref, g_ref, w_ref, ubc_ref, gate_ref, ua_ref):
    x = x_ref[...]
    ms = jnp.mean(x * x, axis=-1, keepdims=True)
    h = (x * lax.rsqrt(ms + RMS_EPS) * g_ref[...]).astype(BF16)
    proj = lambda lo, hi: jnp.dot(h, w_ref[:, lo:hi], preferred_element_type=F32)
    ua_ref[...] = proj(0, W_B_OFF)
    ubc_ref[...] = proj(W_B_OFF, W_G_OFF)
    gate_ref[...] = _sigmoid(proj(W_G_OFF, IN_COLS)).astype(BF16)


def _in_proj(x, g, w, layer, tm):
    n, d = x.shape
    rows = lambda width: pl.BlockSpec((tm, width), lambda i: (i, 0))
    return pl.pallas_call(
        _in_proj_kernel,
        grid=(n // tm,),
        in_specs=[
            rows(d),
            pl.BlockSpec((1, d), lambda i: (0, 0)),
            pl.BlockSpec((None, d, IN_COLS), lambda i: (layer, 0, 0), pipeline_mode=pl.Buffered(1)),
        ],
        out_specs=[rows(B_COLS + C_COLS), rows(GATE_COLS), rows(A_COLS)],
        out_shape=[
            jax.ShapeDtypeStruct((n, B_COLS + C_COLS), F32),
            jax.ShapeDtypeStruct((n, GATE_COLS), BF16),
            jax.ShapeDtypeStruct((n, A_COLS), F32),
        ],
        compiler_params=_cparams(("parallel",)),
        name="in_proj",
    )(x, g, w)


def _rwkv_kernel(u_ref, shift_ref, s0_ref, mu_ref, w0_ref, w2_ref, a0_ref, a2_ref, g2_ref,
                 kk_ref, ka_ref, rk_ref, lnw_ref, lnb_ref, hsum_ref, stack_ref,
                 o_ref, shift_out_ref, s_out_ref, st_scr, prev_scr, o_scr, *, chunk, valid, n_chunks, bb, gb):
    n = pl.program_id(1)

    @pl.when(n == 0)
    def _():
        for i in range(bb):
            for h in range(A_HEADS):
                st_scr[i, h] = s0_ref[i, h].T
        prev_scr[...] = shift_ref[...]

    for g0 in range(0, bb, gb):
        seqs = pl.ds(g0, gb)
        _rwkv_group(u_ref.at[seqs], mu_ref, w0_ref, w2_ref, a0_ref, a2_ref, g2_ref, kk_ref, ka_ref, rk_ref,
                    lnw_ref, lnb_ref, hsum_ref, o_ref.at[seqs], st_scr.at[seqs], prev_scr.at[seqs],
                    o_scr.at[pl.ds(g0 * chunk, gb * chunk)], chunk=chunk, valid=valid, bb=gb)

    @pl.when(n == n_chunks - 1)
    def _():
        last = (valid - 1) if n_chunks == 1 else (chunk - 1)
        for i in range(bb):
            shift_out_ref[i] = u_ref[i, last:last + 1, :]
            for h in range(A_HEADS):
                s_out_ref[i, h] = st_scr[i, h].T


def _rwkv_group(u_ref, mu_ref, w0_ref, w2_ref, a0_ref, a2_ref, g2_ref, kk_ref, ka_ref, rk_ref,
                lnw_ref, lnb_ref, hsum_ref, o_ref, st_scr, prev_scr, o_scr, *, chunk, valid, bb):
    rows = bb * chunk
    units = [(i, h) for i in range(bb) for h in range(A_HEADS)]
    u = u_ref[...].reshape(rows, A_COLS)
    row = lax.broadcasted_iota(jnp.int32, (rows, 1), 0)
    pos = row & (chunk - 1)
    prev = jnp.concatenate([jnp.broadcast_to(prev_scr[i], (chunk, A_COLS)) for i in range(bb)], axis=0)
    shifted = jnp.where(pos == 0, prev, pltpu.roll(u, 1, axis=0))
    xm = u + mu_ref[...] * (shifted - u)
    for i in range(bb):
        prev_scr[i] = u[(i + 1) * chunk - 1:(i + 1) * chunk, :]

    r = xm[:, 0:BR_W]
    k = xm[:, BR_W:2 * BR_W]
    v = xm[:, 2 * BR_W:3 * BR_W]
    wl = xm[:, 3 * BR_W:3 * BR_W + LORA_W]
    al = xm[:, 3 * BR_W + LORA_W:3 * BR_W + LORA_W + LORA_A]
    gl = xm[:, 3 * BR_W + LORA_W + LORA_A:A_COLS]

    z = w0_ref[...] + _bdot(jnp.tanh(wl), w2_ref[...])
    ld = -math.exp(-0.5) * _sigmoid(z)
    a = _sigmoid(a0_ref[...] + _bdot(al, a2_ref[...]))
    g = _bdot(_sigmoid(gl), g2_ref[...])
    kk = k * kk_ref[...]
    ss = _head_sum(kk * kk, hsum_ref[...], 2)
    kk = kk * lax.rsqrt(jnp.maximum(ss, 1e-24))
    k = k * (1.0 + (a - 1.0) * ka_ref[...])
    if valid < chunk:
        ok = pos < valid
        ld = jnp.where(ok, ld, 0.0)
        kk = jnp.where(ok, kk, 0.0)
        k = jnp.where(ok, k, 0.0)
    bv = kk * a

    cl = _chunk_cumsum(ld, chunk)
    cl_last = _last_rows(cl, chunk, bb)
    e_ncl = jnp.exp(-cl)
    e_rem = jnp.exp(cl_last - cl)
    alpha_t = -kk * jnp.exp(cl - ld)
    r_t = r * jnp.exp(cl)
    b_bar = bv * e_ncl
    k_bar = k * e_ncl
    b_hat = bv * e_rem
    k_hat = k * e_rem
    e_last = jnp.exp(cl_last)
    rkk = r * k * rk_ref[...]

    qr = lax.broadcasted_iota(jnp.int32, (2 * chunk, 2 * chunk), 0)
    qc = lax.broadcasted_iota(jnp.int32, (2 * chunk, 2 * chunk), 1)
    qt, qs = qr & (chunk - 1), qc & (chunk - 1)
    keep = (qt > qs) | ((qr >= chunk) & (qt == qs))

    def blk(t, i, h):
        return t[i * chunk:(i + 1) * chunk, h * A_HD:(h + 1) * A_HD]

    aa, st0, vv, x, p = {}, {}, {}, {}, {}
    for i, h in units:
        lhs = jnp.concatenate([blk(alpha_t, i, h), blk(r_t, i, h)], axis=0)
        rhs = jnp.concatenate([blk(b_bar, i, h), blk(k_bar, i, h)], axis=0)
        aa[i, h] = jnp.where(keep, _bdot(lhs, rhs, NT), 0.0)
        st0[i, h] = st_scr[i, h]
        vv[i, h] = blk(v, i, h)
    for i, h in units:
        lhs = jnp.concatenate([blk(alpha_t, i, h), aa[i, h][0:chunk, chunk:2 * chunk]], axis=1)
        x[i, h] = _bdot(lhs, jnp.concatenate([st0[i, h], vv[i, h]], axis=0))
        p[i, h] = aa[i, h][0:chunk, 0:chunk]
    for i, h in units:
        x[i, h] = x[i, h] + _bdot(p[i, h], x[i, h])
    span = 2
    while span < chunk:
        for i, h in units:
            p[i, h] = _bdot(p[i, h], p[i, h])
        for i, h in units:
            x[i, h] = x[i, h] + _bdot(p[i, h], x[i, h])
        span *= 2
    for i, h in units:
        sl = slice(h * A_HD, (h + 1) * A_HD)
        lhs = jnp.concatenate([blk(r_t, i, h), aa[i, h][chunk:2 * chunk, :]], axis=1)
        o = _bdot(lhs, jnp.concatenate([st0[i, h], x[i, h], vv[i, h]], axis=0))
        upd = _bdot(jnp.concatenate([blk(b_hat, i, h), blk(k_hat, i, h)], axis=0),
                    jnp.concatenate([x[i, h], vv[i, h]], axis=0), TN)
        dec = jnp.broadcast_to(e_last[i * chunk:i * chunk + 1, sl], (A_HD, A_HD)).T
        st_scr[i, h] = st0[i, h] * dec + upd
        o_scr[i * chunk:(i + 1) * chunk, sl] = o

    hsum = hsum_ref[...]
    o = o_scr[...]
    d = o - _head_sum(o, hsum, 2) * (1.0 / A_HD)
    var = _head_sum(d * d, hsum, 1) * (1.0 / A_HD)
    o = d * lax.rsqrt(var + RWKV_GN_EPS) * lnw_ref[...] + lnb_ref[...]
    o = o + _head_sum(rkk, hsum, 1) * v
    o_ref[...] = (o * g).astype(o_ref.dtype).reshape(bb, chunk, BR_W)


def _rwkv(u, shift, wkv, layer, stack, p, chunk, valid, bb):
    batch, seq, _ = u.shape
    n_chunks = seq // chunk
    assert valid == chunk or n_chunks == 1
    a_blk = 0
    vec = lambda w: pl.BlockSpec((1, w), lambda b, n: (0, 0))
    full = lambda s: pl.BlockSpec(s, lambda b, n: (0,) * len(s))
    gb = math.gcd(bb, RWKV_GROUP)
    kern = functools.partial(_rwkv_kernel, chunk=chunk, valid=valid, n_chunks=n_chunks, bb=bb, gb=gb)
    return pl.pallas_call(
        kern,
        grid=(batch // bb, n_chunks),
        in_specs=[
            pl.BlockSpec((bb, chunk, A_COLS), lambda b, n: (b, n, a_blk)),
            pl.BlockSpec((None, bb, 1, A_COLS), lambda b, n: (layer, b, 0, 0)),
            pl.BlockSpec((None, bb, A_HEADS, A_HD, A_HD), lambda b, n: (layer, b, 0, 0, 0)),
            vec(A_COLS), vec(BR_W), full((LORA_W, BR_W)), vec(BR_W), full((LORA_A, BR_W)),
            full((LORA_G, BR_W)), vec(BR_W), vec(BR_W), vec(BR_W), vec(BR_W), vec(BR_W),
            pl.BlockSpec((LANES, LANES), lambda b, n: (0, 0)),
            pl.BlockSpec(memory_space=pl.ANY),
        ],
        input_output_aliases={15: 2},
        out_specs=[
            pl.BlockSpec((bb, chunk, BR_W), lambda b, n: (b, n, 0)),
            pl.BlockSpec((bb, 1, A_COLS), lambda b, n: (b, 0, 0)),
            pl.BlockSpec((None, bb, A_HEADS, A_HD, A_HD), lambda b, n: (p["layer"], b, 0, 0, 0)),
        ],
        out_shape=[
            jax.ShapeDtypeStruct((batch, seq, BR_W), _branch_dtype(chunk)),
            jax.ShapeDtypeStruct((batch, 1, A_COLS), F32),
            jax.ShapeDtypeStruct(stack.shape, F32),
        ],
        scratch_shapes=[pltpu.VMEM((bb, A_HEADS, A_HD, A_HD), F32), pltpu.VMEM((bb, 1, A_COLS), F32),
                        pltpu.VMEM((bb * chunk, BR_W), F32)],
        compiler_params=_cparams(("parallel", "arbitrary")),
        name="rwkv7",
    )(u, shift, wkv, p["mu"], p["w0"], p["w2"], p["a0"], p["a2"], p["g2"], p["kk"], p["ka"],
      p["rk"], p["lnw"], p["lnb"], p["hsum"], stack)


def _ret_kernel(u_ref, cos_ref, sin_ref, s0_ref, gn_ref, hsum_ref, stack_ref, o_ref, s_out_ref, s_scr, o_scr,
                *, chunk, valid, n_chunks, bb):
    n = pl.program_id(1)
    units = [(i, h) for i in range(bb) for h in range(B_HEADS)]

    @pl.when(n == 0)
    def _():
        s_scr[...] = s0_ref[...]

    cos = cos_ref[...]
    sin = sin_ref[...]
    tr = lax.broadcasted_iota(jnp.int32, (chunk, chunk), 0)
    tc = lax.broadcasted_iota(jnp.int32, (chunk, chunk), 1)
    diff = (tr - tc).astype(F32)
    causal = tr >= tc
    pos = lax.broadcasted_iota(jnp.int32, (chunk, 1), 0)
    posf = pos.astype(F32)
    lgs = [math.log(1.0 - 2.0 ** (-5.0 - h)) for h in range(B_HEADS)]
    intra = [jnp.where(causal, jnp.exp(lg * jnp.maximum(diff, 0.0)), 0.0) for lg in lgs]
    q_dec = [jnp.exp(lg * (posf + 1.0)) for lg in lgs]
    k_dec = [jnp.where(pos < valid, jnp.exp(lg * (valid - 1.0 - posf)), 0.0) for lg in lgs]
    s_dec = [math.exp(lg * valid) for lg in lgs]

    q, k, v, s0, sc = {}, {}, {}, {}, {}
    for i, h in units:
        ui = u_ref[i]
        qh = ui[:, h * B_HD:(h + 1) * B_HD]
        kh = ui[:, BR_W + h * B_HD:BR_W + (h + 1) * B_HD]
        q[i, h] = qh * cos + pltpu.roll(qh, B_HD // 2, axis=1) * sin
        k[i, h] = (kh * cos + pltpu.roll(kh, B_HD // 2, axis=1) * sin) * (B_HD ** -0.5)
        v[i, h] = ui[:, 2 * BR_W + h * B_HD:2 * BR_W + (h + 1) * B_HD]
        s0[i, h] = s_scr[i, h]
    for i, h in units:
        sc[i, h] = _bdot(q[i, h], k[i, h], NT) * intra[h]
    for i, h in units:
        sl = slice(h * B_HD, (h + 1) * B_HD)
        o = _bdot(sc[i, h], v[i, h]) + _bdot(q[i, h] * q_dec[h], s0[i, h])
        s_scr[i, h] = s0[i, h] * s_dec[h] + _bdot(k[i, h] * k_dec[h], v[i, h], TN)
        o_scr[i * chunk:(i + 1) * chunk, sl] = o

    hsum = hsum_ref[...]
    o = o_scr[...]
    d = o - _head_sum(o, hsum, 2) * (1.0 / B_HD)
    var = _head_sum(d * d, hsum, 1) * (1.0 / B_HD)
    gt = u_ref[:, :, 3 * BR_W:4 * BR_W].reshape(bb * chunk, BR_W)
    o = d * lax.rsqrt(var + GN_EPS) * gn_ref[...] * _silu(gt)
    o_ref[...] = o.astype(o_ref.dtype).reshape(bb, chunk, BR_W)

    @pl.when(n == n_chunks - 1)
    def _():
        s_out_ref[...] = s_scr[...]


def _retention(u, cos, sin, state, layer, stack, out_layer, gn, hsum, chunk, valid, bb):
    batch, seq, _ = u.shape
    n_chunks = seq // chunk
    assert valid == chunk or n_chunks == 1
    kern = functools.partial(_ret_kernel, chunk=chunk, valid=valid, n_chunks=n_chunks, bb=bb)
    return pl.pallas_call(
        kern,
        grid=(batch // bb, n_chunks),
        in_specs=[
            pl.BlockSpec((bb, chunk, B_COLS), lambda b, n: (b, n, 0)),
            pl.BlockSpec((chunk, B_HD), lambda b, n: (n, 0)),
            pl.BlockSpec((chunk, B_HD), lambda b, n: (n, 0)),
            pl.BlockSpec((None, bb, B_HEADS, B_HD, B_HD), lambda b, n: (layer, b, 0, 0, 0)),
            pl.BlockSpec((1, BR_W), lambda b, n: (0, 0)),
            pl.BlockSpec((LANES, LANES), lambda b, n: (0, 0)),
            pl.BlockSpec(memory_space=pl.ANY),
        ],
        input_output_aliases={6: 1},
        out_specs=[
            pl.BlockSpec((bb, chunk, BR_W), lambda b, n: (b, n, 0)),
            pl.BlockSpec((None, bb, B_HEADS, B_HD, B_HD), lambda b, n: (out_layer, b, 0, 0, 0)),
        ],
        out_shape=[
            jax.ShapeDtypeStruct((batch, seq, BR_W), _branch_dtype(chunk)),
            jax.ShapeDtypeStruct(stack.shape, F32),
        ],
        scratch_shapes=[pltpu.VMEM((bb, B_HEADS, B_HD, B_HD), F32), pltpu.VMEM((bb * chunk, BR_W), F32)],
        compiler_params=_cparams(("parallel", "arbitrary")),
        name="retention",
    )(u, cos, sin, state, gn, hsum, stack)


def _hgrn_kernel(u_ref, lb_ref, s0_ref, gn_ref, hsum_ref, stack_ref, o_ref, s_out_ref, st_scr, o_scr,
                 *, chunk, valid, n_chunks, bb):
    n = pl.program_id(1)
    rows = bb * chunk
    units = [(i, h) for i in range(bb) for h in range(C_HEADS)]

    @pl.when(n == 0)
    def _():
        for i, h in units:
            st_scr[i, h] = s0_ref[i, h].T

    lb = lb_ref[...]
    row = lax.broadcasted_iota(jnp.int32, (rows, 1), 0)
    sig_f = _sigmoid(u_ref[:, :, BR_W:2 * BR_W].reshape(rows, BR_W))
    lf = jnp.log(jnp.maximum(lb + (1.0 - lb) * sig_f, F_FLOOR))
    kc = (1.0 - lb) * (1.0 - sig_f)
    if valid < chunk:
        ok = (row & (chunk - 1)) < valid
        lf = jnp.where(ok, lf, 0.0)
        kc = jnp.where(ok, kc, 0.0)
    bc = _chunk_cumsum(lf, chunk)
    b_last = _last_rows(bc, chunk, bb)
    e_b = jnp.exp(bc)
    e_last = jnp.exp(b_last)
    k_hat = kc * jnp.exp(b_last - bc)
    q_all = u_ref[:, :, 0:BR_W].reshape(rows, BR_W)
    tr =lax.broadcasted_iota(jnp.int32, (chunk, chunk), 0)
    tc = lax.broadcasted_iota(jnp.int32, (chunk, chunk), 1)
    differ = tr ^ tc
    sub8 = lax.broadcasted_iota(jnp.int32, (8, 1), 0)
    width = min(chunk, C_HD)

    def blk(t, i, h):
        return t[i * chunk:(i + 1) * chunk, h * C_HD:(h + 1) * C_HD]

    def boundary(c):
        pieces = []
        for base in range(0, rows, 8):
            if 2 * c <= 8:
                parts = [jnp.broadcast_to(bc[base + m + c - 1:base + m + c, :], (8, BR_W))
                         for m in range(0, 8, 2 * c)]
                piece = parts[-1]
                for idx in range(len(parts) - 2, -1, -1):
                    piece = jnp.where(sub8 < (idx + 1) * 2 * c, parts[idx], piece)
            else:
                at = (base // (2 * c)) * (2 * c) + c - 1
                piece = jnp.broadcast_to(bc[at:at + 1, :], (8, BR_W))
            pieces.append(piece)
        return jnp.concatenate(pieces, axis=0)

    q, v, st, scores = {}, {}, {}, {}
    diag = _head_sum(q_all * kc, hsum_ref[...], 2)
    for i, h in units:
        v[i, h] = u_ref[i, :, 2 * BR_W + h * C_HD:2 * BR_W + (h + 1) * C_HD]
        q[i, h] = blk(q_all, i, h)
        st[i, h] = st_scr[i, h]
        scores[i, h] = jnp.where(tr == tc, diag[i * chunk:(i + 1) * chunk, h * C_HD:h * C_HD + width], 0.0)
    c = 1
    while c < chunk:
        w = jnp.exp(-jnp.abs(bc - boundary(c)))
        qz = q_all * w
        kz = kc * w
        here = (tr > tc) & (differ >= c) & (differ < 2 * c)
        for i, h in units:
            scores[i, h] = jnp.where(here, _bdot(blk(qz, i, h), blk(kz, i, h), NT), scores[i, h])
        c *= 2

    for i, h in units:
        sl = slice(h * C_HD, (h + 1) * C_HD)
        o = _bdot(scores[i, h], v[i, h]) + _bdot(q[i, h] * blk(e_b, i, h), st[i, h], NT)
        st_scr[i, h] = st[i, h] * e_last[i * chunk:i * chunk + 1, sl] + _bdot(v[i, h], blk(k_hat, i, h), TN)
        o_scr[i * chunk:(i + 1) * chunk, sl] = o

    o = o_scr[...]
    ms = _head_sum(o * o, hsum_ref[...], 1) * (1.0 / C_HD)
    gt = u_ref[:, :, 3 * BR_W:4 * BR_W].reshape(rows, BR_W)
    o = o * lax.rsqrt(ms + GN_EPS) * gn_ref[...] * _silu(gt)
    o_ref[...] = o.astype(o_ref.dtype).reshape(bb, chunk, BR_W)

    @pl.when(n == n_chunks - 1)
    def _():
        for i, h in units:
            s_out_ref[i, h] = st_scr[i, h].T


def _hgrn(u, lb, state, layer, stack, out_layer, gn, hsum, chunk, valid, bb):
    batch, seq, _ = u.shape
    n_chunks = seq // chunk
    assert valid == chunk or n_chunks == 1
    kern = functools.partial(_hgrn_kernel, chunk=chunk, valid=valid, n_chunks=n_chunks, bb=bb)
    return pl.pallas_call(
        kern,
        grid=(batch // bb, n_chunks),
        in_specs=[
            pl.BlockSpec((bb, chunk, C_COLS), lambda b, n: (b, n, 1)),
            pl.BlockSpec((1, BR_W), lambda b, n: (0, 0)),
            pl.BlockSpec((None, bb, C_HEADS, C_HD, C_HD), lambda b, n: (layer, b, 0, 0, 0)),
            pl.BlockSpec((1, BR_W), lambda b, n: (0, 0)),
            pl.BlockSpec((LANES, LANES), lambda b, n: (0, 0)),
            pl.BlockSpec(memory_space=pl.ANY),
        ],
        input_output_aliases={5: 1},
        out_specs=[
            pl.BlockSpec((bb, chunk, BR_W), lambda b, n: (b, n, 0)),
            pl.BlockSpec((None, bb, C_HEADS, C_HD, C_HD), lambda b, n: (out_layer, b, 0, 0, 0)),
        ],
        out_shape=[
            jax.ShapeDtypeStruct((batch, seq, BR_W), _branch_dtype(chunk)),
            jax.ShapeDtypeStruct(stack.shape, F32),
        ],
        scratch_shapes=[pltpu.VMEM((bb, C_HEADS, C_HD, C_HD), F32), pltpu.VMEM((bb * chunk, BR_W), F32)],
        compiler_params=_cparams(("parallel", "arbitrary")),
        name="hgrn2",
    )(u, lb, state, gn, hsum, stack)


def _merge_kernel(oa_ref, ob_ref, oc_ref, g0_ref, g1_ref, g2_ref, x_ref, wb_ref, wo_ref, o_ref):
    m = g0_ref[...] * _bdot(oa_ref[...], wb_ref[0])
    m = m + g1_ref[...] * _bdot(ob_ref[...], wb_ref[1])
    m = m + g2_ref[...] * _bdot(oc_ref[...], wb_ref[2])
    o_ref[...] = x_ref[...] + _bdot(m, wo_ref[...])


def _merge(oa, ob, oc, u, x, wb, wo, layer, tm):
    n = x.shape[0]
    br = pl.BlockSpec((tm, BR_W), lambda i: (i, 0))
    gate = lambda c: pl.BlockSpec((tm, D_MODEL), lambda i: (i, c))
    return pl.pallas_call(
        _merge_kernel,
        grid=(n // tm,),
        in_specs=[br, br, br, gate(0), gate(1), gate(2),
                  pl.BlockSpec((tm, D_MODEL), lambda i: (i, 0)),
                  pl.BlockSpec((None, 3, BR_W, D_MODEL), lambda i: (layer, 0, 0, 0)),
                  pl.BlockSpec((None, D_MODEL, D_MODEL), lambda i: (layer, 0, 0))],
        out_specs=pl.BlockSpec((tm, D_MODEL), lambda i: (i, 0)),
        out_shape=jax.ShapeDtypeStruct((n, D_MODEL), F32),
        compiler_params=_cparams(("parallel",)),
        name="merge",
    )(oa, ob, oc, u, u, u, x, wb, wo)


def _ffn_kernel(x_ref, g_ref, wg_ref, wu_ref, wo_ref, gfin_ref, o_ref, h_ref, *, n_ff, final):
    @pl.when(pl.program_id(1) == 0)
    def _():
        x = x_ref[...]
        ms = jnp.mean(x * x, axis=-1, keepdims=True)
        h_ref[...] = (x * lax.rsqrt(ms + RMS_EPS) * g_ref[...]).astype(BF16)
        o_ref[...] = x

    h = h_ref[...]
    gt = jnp.dot(h, wg_ref[...], preferred_element_type=F32)
    up = jnp.dot(h, wu_ref[...], preferred_element_type=F32)
    o_ref[...] += _bdot(_silu(gt) * up, wo_ref[...])

    if final:
        @pl.when(pl.program_id(1) == n_ff - 1)
        def _():
            x = o_ref[...]
            ms = jnp.mean(x * x, axis=-1, keepdims=True)
            o_ref[...] = x * lax.rsqrt(ms + RMS_EPS) * gfin_ref[...]


def _ffn(x, g, w_in, w_out, layer, g_final, final, tm, tf):
    n = x.shape[0]
    n_ff = D_FF // tf
    resident = {"pipeline_mode": pl.Buffered(1)} if n_ff == 1 else {}
    return pl.pallas_call(
        functools.partial(_ffn_kernel, n_ff=n_ff, final=final),
        grid=(n // tm, n_ff),
        in_specs=[
            pl.BlockSpec((tm, D_MODEL), lambda i, j: (i, 0)),
            pl.BlockSpec((1, D_MODEL), lambda i, j: (0, 0)),
            pl.BlockSpec((None, D_MODEL, tf), lambda i, j: (layer, 0, j), **resident),
            pl.BlockSpec((None, D_MODEL, tf), lambda i, j: (layer, 0, n_ff + j), **resident),
            pl.BlockSpec((None, tf, D_MODEL), lambda i, j: (layer, j, 0), **resident),
            pl.BlockSpec((1, D_MODEL), lambda i, j: (0, 0)),
        ],
        out_specs=pl.BlockSpec((tm, D_MODEL), lambda i, j: (i, 0)),
        out_shape=jax.ShapeDtypeStruct((n, D_MODEL), F32),
        scratch_shapes=[pltpu.VMEM((tm, D_MODEL), BF16)],
        compiler_params=_cparams(("parallel", "arbitrary")),
        name="ffn",
    )(x, g, w_in, w_in, w_out, g_final)


def _rope_tables(pos):
    half = B_HD // 2
    inv = ROPE_BASE ** (-jnp.arange(half, dtype=F32) / half)
    ang = pos.astype(F32)[:, None] * inv[None, :]
    cos, sin = jnp.cos(ang), jnp.sin(ang)
    return jnp.concatenate([cos, cos], axis=-1), jnp.concatenate([-sin, sin], axis=-1)


def _row_tile(n, largest=512):
    t = largest
    while t >= 8:
        if n % t == 0:
            return t
        t //= 2
    raise ValueError(f"row count {n} is not a multiple of 8")


def _layer(x, states, layer, stacks, cos, sin, p, batch, seq, valid, cfg):
    shift, wkv, ret, hg = states
    wkv_stack, ret_stack, hg_stack = stacks
    tm = _row_tile(x.shape[0])
    wl = p["layer"]
    ubc, gates, ua = _in_proj(x, p["norm_mix"], p["w_in"], wl, _row_tile(x.shape[0], IN_ROW_TILE))
    ubc = ubc.reshape(batch, seq, B_COLS + C_COLS)
    ua = ua.reshape(batch, seq, A_COLS)
    full = lambda c: valid if c >= seq else c
    (ca, sa), (cb, sb), (cc, sc) = cfg["rwkv"], cfg["ret"], cfg["hgrn"]
    oa, shift_new, wkv_stack = _rwkv(ua, shift, wkv, layer, wkv_stack, p, ca, full(ca), sa)
    ob, ret_stack = _retention(ubc, cos, sin, ret, layer, ret_stack, wl, p["ret_gn"], p["hsum128"], cb, full(cb), sb)
    oc, hg_stack = _hgrn(ubc, p["lower"], hg, layer, hg_stack, wl, p["hgrn_gn"], p["hsum128"], cc, full(cc), sc)
    flat = lambda o: o.reshape(batch * seq, BR_W)
    x = _merge(flat(oa), flat(ob), flat(oc), gates, x, p["w_branch"], p["w_out"], wl, tm)
    x = _ffn(x, p["norm_ffn"], p["w_ffn_in"], p["w_ffn_out"], wl, p["norm_final"], p["last"], tm, FF_TILE)
    return x, shift_new, (wkv_stack, ret_stack, hg_stack)


def kernel(x_prompt, x_sample, state_rwkv_shift, state_rwkv_wkv, state_ret, state_hgrn,
           norm_mix, w_in, rwkv_mu, rwkv_w0, rwkv_w2, rwkv_a0, rwkv_a2, rwkv_g2, rwkv_kk,
           rwkv_ka, rwkv_rk, rwkv_ln_w, rwkv_ln_b, ret_gn, hgrn_lb, hgrn_gn, w_branch, w_out,
           norm_ffn, w_ffn_in, w_ffn_out, norm_final):
    depth = w_in.shape[0]
    bp, tp, _ = x_prompt.shape
    bs, ts, _ = x_sample.shape
    assert ts <= SAMPLE_CHUNK
    assert all(tp % c == 0 and bp % s == 0 for c, s in PROMPT_CFG.values())
    assert all(bs % s == 0 for _, s in SAMPLE_CFG.values())
    dt = x_prompt.dtype

    lower = _lower_bounds(hgrn_lb)
    w_in_r = w_in.astype(BF16)
    w_branch_b = w_branch.astype(BF16)
    w_out_b = w_out.astype(BF16)
    w_ffn_in_b = w_ffn_in.astype(BF16)
    w_ffn_out_b = w_ffn_out.astype(BF16)
    lanes = jnp.arange(BR_W)
    hsum = (lanes[:, None] // A_HD == lanes[None, :] // A_HD).astype(BF16)
    hsum128 = (lanes[:, None] // B_HD == lanes[None, :] // B_HD).astype(BF16)
    row = lambda a: a.reshape(1, -1).astype(F32)

    cos_p, sin_p = _rope_tables(jnp.arange(tp, dtype=jnp.int32))
    cos_s, sin_s = _rope_tables(PAST_LEN + jnp.arange(SAMPLE_CHUNK, dtype=jnp.int32))

    xp = x_prompt.reshape(bp * tp, D_MODEL)
    xs = jnp.pad(x_sample, ((0, 0), (0, SAMPLE_CHUNK - ts), (0, 0))).reshape(bs * SAMPLE_CHUNK, D_MODEL)
    zero_states = (jnp.zeros((1, bp, 1, A_COLS), F32), jnp.zeros((1, bp, A_HEADS, A_HD, A_HD), F32),
                   jnp.zeros((1, bp, B_HEADS, B_HD, B_HD), F32), jnp.zeros((1, bp, C_HEADS, C_HD, C_HD), F32))
    sample_states = (state_rwkv_shift.reshape(depth, bs, 1, A_COLS), state_rwkv_wkv, state_ret, state_hgrn)

    new_states = lambda b: (jnp.zeros((depth, b, A_HEADS, A_HD, A_HD), F32),
                            jnp.zeros((depth, b, B_HEADS, B_HD, B_HD), F32),
                            jnp.zeros((depth, b, C_HEADS, C_HD, C_HD), F32))
    stacks_p, stacks_s = new_states(bp), new_states(bs)
    shifts_p, shifts_s = [], []
    for l in range(depth):
        p = {
            "layer": l, "norm_mix": row(norm_mix[l]), "w_in": w_in_r, "mu": row(rwkv_mu[l]),
            "w0": row(rwkv_w0[l]), "w2": rwkv_w2[l].astype(BF16), "a0": row(rwkv_a0[l]),
            "a2": rwkv_a2[l].astype(BF16), "g2": rwkv_g2[l].astype(BF16), "kk": row(rwkv_kk[l]),
            "ka": row(rwkv_ka[l]), "rk": row(rwkv_rk[l]), "lnw": row(rwkv_ln_w[l]),
            "lnb": row(rwkv_ln_b[l]), "hsum": hsum, "hsum128": hsum128, "ret_gn": row(ret_gn[l]),
            "lower": lower[l:l + 1], "hgrn_gn": row(hgrn_gn[l]), "w_branch": w_branch_b,
            "w_out": w_out_b, "norm_ffn": row(norm_ffn[l]), "w_ffn_in": w_ffn_in_b,
            "w_ffn_out": w_ffn_out_b, "norm_final": row(norm_final), "last": l == depth - 1,
        }
        xp, sh_p, stacks_p = _layer(xp, zero_states, 0, stacks_p, cos_p, sin_p, p, bp, tp, tp, PROMPT_CFG)
        xs, sh_s, stacks_s = _layer(xs, sample_states, l, stacks_s, cos_s, sin_s, p, bs, SAMPLE_CHUNK, ts,
                                    SAMPLE_CFG)
        shifts_p.append(sh_p.reshape(bp, A_COLS))
        shifts_s.append(sh_s.reshape(bs, A_COLS))

    y_prompt = xp.reshape(bp, tp, D_MODEL)
    y_sample = xs.reshape(bs, SAMPLE_CHUNK, D_MODEL)[:, :ts]
    cast = lambda a: a.astype(dt)
    return (y_prompt, y_sample, cast(jnp.stack(shifts_p)), *map(cast, stacks_p),
            cast(jnp.stack(shifts_s)), *map(cast, stacks_s))
```

```python
import functools
import math

import jax
import jax.numpy as jnp
from jax import lax
from jax.experimental import pallas as pl
from jax.experimental.pallas import tpu as pltpu

F32 = jnp.float32
BF16 = jnp.bfloat16

D_MODEL = 1024
BR_W = 512
A_HD = 64
A_HEADS = BR_W // A_HD
LORA_W = 64
LORA_A = 64
LORA_G = 128
A_COLS = 3 * BR_W + LORA_W + LORA_A + LORA_G
B_HD = 128
B_HEADS = BR_W // B_HD
C_HD = 128
C_HEADS = BR_W // C_HD
B_COLS = 4 * BR_W
C_COLS = 4 * BR_W
GATE_COLS = 3 * D_MODEL
IN_COLS = A_COLS + B_COLS + C_COLS + GATE_COLS
D_FF = 2816
PAST_LEN = 16384
RMS_EPS = 1e-6
RWKV_GN_EPS = 64e-5
GN_EPS = 1e-5
F_FLOOR = 1e-30
ROPE_BASE = 10000.0

W_B_OFF = A_COLS
W_G_OFF = A_COLS + B_COLS + C_COLS

SAMPLE_CHUNK = 8
PROMPT_CFG = {"rwkv": (64, 8), "ret": (128, 8), "hgrn": (128, 4)}
SAMPLE_CFG = {"rwkv": (SAMPLE_CHUNK, 4), "ret": (SAMPLE_CHUNK, 16), "hgrn": (SAMPLE_CHUNK, 16)}
RWKV_GROUP = 8
IN_ROW_TILE = 256
FF_TILE = D_FF
MERGE_ROW_TILE = 1024

VMEM_LIMIT = 56 * 1024 * 1024
LANES = 128

NN = (((1,), (0,)), ((), ()))
NT = (((1,), (1,)), ((), ()))
TN = (((0,), (0,)), ((), ()))


def _bdot(a, b, dims=NN):
    return lax.dot_general(a.astype(BF16), b.astype(BF16), dims, preferred_element_type=F32)


def _sel_dot(sel, x, passes, sel_first=True):
    sel = sel.astype(BF16)
    acc = None
    for _ in range(passes):
        piece = x.astype(BF16)
        ops = (sel, piece) if sel_first else (piece, sel)
        part = lax.dot_general(*ops, NN, preferred_element_type=F32)
        acc = part if acc is None else acc + part
        x = x - piece.astype(F32)
    return acc


def _branch_dtype(chunk):
    return BF16 if chunk % 16 == 0 else F32


def _head_sum(x, same_head, passes):
    return jnp.concatenate([_sel_dot(same_head, x[:, g:g + LANES], passes, sel_first=False)
                            for g in range(0, x.shape[1], LANES)], axis=1)


def _sigmoid(x):
    return 0.5 * jnp.tanh(0.5 * x) + 0.5


def _silu(x):
    return x * _sigmoid(x)


def _cparams(sem):
    return pltpu.CompilerParams(dimension_semantics=sem, vmem_limit_bytes=VMEM_LIMIT)


def _chunk_cumsum(x, chunk):
    rows = x.shape[0]
    group = max(chunk, min(rows, 64))
    assert rows % group == 0
    tr = lax.broadcasted_iota(jnp.int32, (group, group), 0)
    tc = lax.broadcasted_iota(jnp.int32, (group, group), 1)
    sh = chunk.bit_length() - 1
    tri = jnp.where((tr >= tc) & ((tr >> sh) == (tc >> sh)), 1.0, 0.0)
    return jnp.concatenate([_sel_dot(tri, x[r:r + group], 3) for r in range(0, rows, group)], axis=0)


def _last_rows(x, chunk, bb):
    return jnp.concatenate(
        [jnp.broadcast_to(x[(i + 1) * chunk - 1:(i + 1) * chunk, :], (chunk, x.shape[1])) for i in range(bb)],
        axis=0)


def _lower_kernel(lb_ref, o_ref):
    x = lb_ref[...]
    depth = x.shape[0]
    m = x[0:1]
    for l in range(1, depth):
        m = jnp.maximum(m, x[l:l + 1])
    e = jnp.exp(x - m)
    tot = e[0:1]
    for l in range(1, depth):
        tot = tot + e[l:l + 1]
    sm = e / tot
    acc = jnp.zeros_like(m)
    for l in range(depth):
        acc = acc + sm[l:l + 1]
        o_ref[l:l + 1, :] = acc - sm[0:1]


def _lower_bounds(hgrn_lb):
    return pl.pallas_call(
        _lower_kernel, out_shape=jax.ShapeDtypeStruct(hgrn_lb.shape, F32), name="hgrn_lower",
    )(hgrn_lb.astype(F32))


def _in_proj_kernel(x_ref, g_ref, w_ref, ubc_ref, gate_ref, ua_ref):
    x = x_ref[...]
    ms = jnp.mean(x * x, axis=-1, keepdims=True)
    h = (x * lax.rsqrt(ms + RMS_EPS) * g_ref[...]).astype(BF16)
    proj = lambda lo, hi: jnp.dot(h, w_ref[:, lo:hi], preferred_element_type=F32)
    ua_ref[...] = proj(0, W_B_OFF)
    ubc_ref[...] = proj(W_B_OFF, W_G_OFF)
    gate_ref[...] = _sigmoid(proj(W_G_OFF, IN_COLS)).astype(BF16)


def _in_proj(x, g, w, layer, tm):
    n, d = x.shape
    rows = lambda width: pl.BlockSpec((tm, width), lambda i: (i, 0))
    return pl.pallas_call(
        _in_proj_kernel,
        grid=(n // tm,),
        in_specs=[
            rows(d),
            pl.BlockSpec((1, d), lambda i: (0, 0)),
            pl.BlockSpec((None, d, IN_COLS), lambda i: (layer, 0, 0), pipeline_mode=pl.Buffered(1)),
        ],
        out_specs=[rows(B_COLS + C_COLS), rows(GATE_COLS), rows(A_COLS)],
        out_shape=[
            jax.ShapeDtypeStruct((n, B_COLS + C_COLS), F32),
            jax.ShapeDtypeStruct((n, GATE_COLS), BF16),
            jax.ShapeDtypeStruct((n, A_COLS), F32),
        ],
        compiler_params=_cparams(("parallel",)),
        name="in_proj",
    )(x, g, w)


def _rwkv_kernel(u_ref, shift_ref, s0_ref, mu_ref, w0_ref, w2_ref, a0_ref, a2_ref, g2_ref,
                 kk_ref, ka_ref, rk_ref, lnw_ref, lnb_ref, hsum_ref, stack_ref,
                 o_ref, shift_out_ref, s_out_ref, st_scr, prev_scr, o_scr, *, chunk, valid, n_chunks, bb, gb):
    n = pl.program_id(1)

    @pl.when(n == 0)
    def _():
        for i in range(bb):
            for h in range(A_HEADS):
                st_scr[i, h] = s0_ref[i, h].T
        prev_scr[...] = shift_ref[...]

    for g0 in range(0, bb, gb):
        seqs = pl.ds(g0, gb)
        _rwkv_group(u_ref.at[seqs], mu_ref, w0_ref, w2_ref, a0_ref, a2_ref, g2_ref, kk_ref, ka_ref, rk_ref,
                    lnw_ref, lnb_ref, hsum_ref, o_ref.at[seqs], st_scr.at[seqs], prev_scr.at[seqs],
                    o_scr.at[pl.ds(g0 * chunk, gb * chunk)], chunk=chunk, valid=valid, bb=gb)

    @pl.when(n == n_chunks - 1)
    def _():
        last = (valid - 1) if n_chunks == 1 else (chunk - 1)
        for i in range(bb):
            shift_out_ref[i] = u_ref[i, last:last + 1, :]
            for h in range(A_HEADS):
                s_out_ref[i, h] = st_scr[i, h].T


def _rwkv_group(u_ref, mu_ref, w0_ref, w2_ref, a0_ref, a2_ref, g2_ref, kk_ref, ka_ref, rk_ref,
                lnw_ref, lnb_ref, hsum_ref, o_ref, st_scr, prev_scr, o_scr, *, chunk, valid, bb):
    rows = bb * chunk
    units = [(i, h) for i in range(bb) for h in range(A_HEADS)]
    u = u_ref[...].reshape(rows, A_COLS)
    row = lax.broadcasted_iota(jnp.int32, (rows, 1), 0)
    pos = row & (chunk - 1)
    prev = jnp.concatenate([jnp.broadcast_to(prev_scr[i], (chunk, A_COLS)) for i in range(bb)], axis=0)
    shifted = jnp.where(pos == 0, prev, pltpu.roll(u, 1, axis=0))
    xm = u + mu_ref[...] * (shifted - u)
    for i in range(bb):
        prev_scr[i] = u[(i + 1) * chunk - 1:(i + 1) * chunk, :]

    r = xm[:, 0:BR_W]
    k = xm[:, BR_W:2 * BR_W]
    v = xm[:, 2 * BR_W:3 * BR_W]
    wl = xm[:, 3 * BR_W:3 * BR_W + LORA_W]
    al = xm[:, 3 * BR_W + LORA_W:3 * BR_W + LORA_W + LORA_A]
    gl = xm[:, 3 * BR_W + LORA_W + LORA_A:A_COLS]

    z = w0_ref[...] + _bdot(jnp.tanh(wl), w2_ref[...])
    ld = -math.exp(-0.5) * _sigmoid(z)
    a = _sigmoid(a0_ref[...] + _bdot(al, a2_ref[...]))
    g = _bdot(_sigmoid(gl), g2_ref[...])
    kk = k * kk_ref[...]
    ss = _head_sum(kk * kk, hsum_ref[...], 2)
    kk = kk * lax.rsqrt(jnp.maximum(ss, 1e-24))
    k = k * (1.0 + (a - 1.0) * ka_ref[...])
    if valid < chunk:
        ok = pos < valid
        ld = jnp.where(ok, ld, 0.0)
        kk = jnp.where(ok, kk, 0.0)
        k = jnp.where(ok, k, 0.0)
    bv = kk * a

    cl = _chunk_cumsum(ld, chunk)
    cl_last = _last_rows(cl, chunk, bb)
    e_ncl = jnp.exp(-cl)
    e_rem = jnp.exp(cl_last - cl)
    alpha_t = -kk * jnp.exp(cl - ld)
    r_t = r * jnp.exp(cl)
    b_bar = bv * e_ncl
    k_bar = k * e_ncl
    b_hat = bv * e_rem
    k_hat = k * e_rem
    e_last = jnp.exp(cl_last)
    rkk = r * k * rk_ref[...]

    qr = lax.broadcasted_iota(jnp.int32, (2 * chunk, 2 * chunk), 0)
    qc = lax.broadcasted_iota(jnp.int32, (2 * chunk, 2 * chunk), 1)
    qt, qs = qr & (chunk - 1), qc & (chunk - 1)
    keep = (qt > qs) | ((qr >= chunk) & (qt == qs))

    def blk(t, i, h):
        return t[i * chunk:(i + 1) * chunk, h * A_HD:(h + 1) * A_HD]

    aa, st0, vv, x, p = {}, {}, {}, {}, {}
    for i, h in units:
        lhs = jnp.concatenate([blk(alpha_t, i, h), blk(r_t, i, h)], axis=0)
        rhs = jnp.concatenate([blk(b_bar, i, h), blk(k_bar, i, h)], axis=0)
        aa[i, h] = jnp.where(keep, _bdot(lhs, rhs, NT), 0.0)
        st0[i, h] = st_scr[i, h]
        vv[i, h] = blk(v, i, h)
    for i, h in units:
        lhs = jnp.concatenate([blk(alpha_t, i, h), aa[i, h][0:chunk, chunk:2 * chunk]], axis=1)
        x[i, h] = _bdot(lhs, jnp.concatenate([st0[i, h], vv[i, h]], axis=0))
        p[i, h] = aa[i, h][0:chunk, 0:chunk]
    for i, h in units:
        x[i, h] = x[i, h] + _bdot(p[i, h], x[i, h])
    span = 2
    while span < chunk:
        for i, h in units:
            p[i, h] = _bdot(p[i, h], p[i, h])
        for i, h in units:
            x[i, h] = x[i, h] + _bdot(p[i, h], x[i, h])
        span *= 2
    for i, h in units:
        sl = slice(h * A_HD, (h + 1) * A_HD)
        lhs = jnp.concatenate([blk(r_t, i, h), aa[i, h][chunk:2 * chunk, :]], axis=1)
        o = _bdot(lhs, jnp.concatenate([st0[i, h], x[i, h], vv[i, h]], axis=0))
        upd = _bdot(jnp.concatenate([blk(b_hat, i, h), blk(k_hat, i, h)], axis=0),
                    jnp.concatenate([x[i, h], vv[i, h]], axis=0), TN)
        dec = jnp.broadcast_to(e_last[i * chunk:i * chunk + 1, sl], (A_HD, A_HD)).T
        st_scr[i, h] = st0[i, h] * dec + upd
        o_scr[i * chunk:(i + 1) * chunk, sl] = o

    hsum = hsum_ref[...]
    o = o_scr[...]
    d = o - _head_sum(o, hsum, 2) * (1.0 / A_HD)
    var = _head_sum(d * d, hsum, 1) * (1.0 / A_HD)
    o = d * lax.rsqrt(var + RWKV_GN_EPS) * lnw_ref[...] + lnb_ref[...]
    o = o + _head_sum(rkk, hsum, 1) * v
    o_ref[...] = (o * g).astype(o_ref.dtype).reshape(bb, chunk, BR_W)


def _rwkv(u, shift, wkv, layer, stack, p, chunk, valid, bb):
    batch, seq, _ = u.shape
    n_chunks = seq // chunk
    assert valid == chunk or n_chunks == 1
    a_blk = 0
    vec = lambda w: pl.BlockSpec((1, w), lambda b, n: (0, 0))
    full = lambda s: pl.BlockSpec(s, lambda b, n: (0,) * len(s))
    gb = math.gcd(bb, RWKV_GROUP)
    kern = functools.partial(_rwkv_kernel, chunk=chunk, valid=valid, n_chunks=n_chunks, bb=bb, gb=gb)
    return pl.pallas_call(
        kern,
        grid=(batch // bb, n_chunks),
        in_specs=[
            pl.BlockSpec((bb, chunk, A_COLS), lambda b, n: (b, n, a_blk)),
            pl.BlockSpec((None, bb, 1, A_COLS), lambda b, n: (layer, b, 0, 0)),
            pl.BlockSpec((None, bb, A_HEADS, A_HD, A_HD), lambda b, n: (layer, b, 0, 0, 0)),
            vec(A_COLS), vec(BR_W), full((LORA_W, BR_W)), vec(BR_W), full((LORA_A, BR_W)),
            full((LORA_G, BR_W)), vec(BR_W), vec(BR_W), vec(BR_W), vec(BR_W), vec(BR_W),
            pl.BlockSpec((LANES, LANES), lambda b, n: (0, 0)),
            pl.BlockSpec(memory_space=pl.ANY),
        ],
        input_output_aliases={15: 2},
        out_specs=[
            pl.BlockSpec((bb, chunk, BR_W), lambda b, n: (b, n, 0)),
            pl.BlockSpec((bb, 1, A_COLS), lambda b, n: (b, 0, 0)),
            pl.BlockSpec((None, bb, A_HEADS, A_HD, A_HD), lambda b, n: (p["layer"], b, 0, 0, 0)),
        ],
        out_shape=[
            jax.ShapeDtypeStruct((batch, seq, BR_W), _branch_dtype(chunk)),
            jax.ShapeDtypeStruct((batch, 1, A_COLS), F32),
            jax.ShapeDtypeStruct(stack.shape, F32),
        ],
        scratch_shapes=[pltpu.VMEM((bb, A_HEADS, A_HD, A_HD), F32), pltpu.VMEM((bb, 1, A_COLS), F32),
                        pltpu.VMEM((bb * chunk, BR_W), F32)],
        compiler_params=_cparams(("parallel", "arbitrary")),
        name="rwkv7",
    )(u, shift, wkv, p["mu"], p["w0"], p["w2"], p["a0"], p["a2"], p["g2"], p["kk"], p["ka"],
      p["rk"], p["lnw"], p["lnb"], p["hsum"], stack)


def _ret_kernel(u_ref, cos_ref, sin_ref, s0_ref, gn_ref, hsum_ref, stack_ref, o_ref, s_out_ref, s_scr, o_scr,
                *, chunk, valid, n_chunks, bb):
    n = pl.program_id(1)
    units = [(i, h) for i in range(bb) for h in range(B_HEADS)]

    @pl.when(n == 0)
    def _():
        s_scr[...] = s0_ref[...]

    cos = cos_ref[...]
    sin = sin_ref[...]
    tr = lax.broadcasted_iota(jnp.int32, (chunk, chunk), 0)
    tc = lax.broadcasted_iota(jnp.int32, (chunk, chunk), 1)
    diff = (tr - tc).astype(F32)
    causal = tr >= tc
    pos = lax.broadcasted_iota(jnp.int32, (chunk, 1), 0)
    posf = pos.astype(F32)
    lgs = [math.log(1.0 - 2.0 ** (-5.0 - h)) for h in range(B_HEADS)]
    intra = [jnp.where(causal, jnp.exp(lg * jnp.maximum(diff, 0.0)), 0.0) for lg in lgs]
    q_dec = [jnp.exp(lg * (posf + 1.0)) for lg in lgs]
    k_dec = [jnp.where(pos < valid, jnp.exp(lg * (valid - 1.0 - posf)), 0.0) for lg in lgs]
    s_dec = [math.exp(lg * valid) for lg in lgs]

    q, k, v, s0, sc = {}, {}, {}, {}, {}
    for i, h in units:
        ui = u_ref[i]
        qh = ui[:, h * B_HD:(h + 1) * B_HD]
        kh = ui[:, BR_W + h * B_HD:BR_W + (h + 1) * B_HD]
        q[i, h] = qh * cos + pltpu.roll(qh, B_HD // 2, axis=1) * sin
        k[i, h] = (kh * cos + pltpu.roll(kh, B_HD // 2, axis=1) * sin) * (B_HD ** -0.5)
        v[i, h] = ui[:, 2 * BR_W + h * B_HD:2 * BR_W + (h + 1) * B_HD]
        s0[i, h] = s_scr[i, h]
    for i, h in units:
        sc[i, h] = _bdot(q[i, h], k[i, h], NT) * intra[h]
    for i, h in units:
        sl = slice(h * B_HD, (h + 1) * B_HD)
        o = _bdot(sc[i, h], v[i, h]) + _bdot(q[i, h] * q_dec[h], s0[i, h])
        s_scr[i, h] = s0[i, h] * s_dec[h] + _bdot(k[i, h] * k_dec[h], v[i, h], TN)
        o_scr[i * chunk:(i + 1) * chunk, sl] = o

    hsum = hsum_ref[...]
    o = o_scr[...]
    d = o - _head_sum(o, hsum, 2) * (1.0 / B_HD)
    var = _head_sum(d * d, hsum, 1) * (1.0 / B_HD)
    gt = u_ref[:, :, 3 * BR_W:4 * BR_W].reshape(bb * chunk, BR_W)
    o = d * lax.rsqrt(var + GN_EPS) * gn_ref[...] * _silu(gt)
    o_ref[...] = o.astype(o_ref.dtype).reshape(bb, chunk, BR_W)

    @pl.when(n == n_chunks - 1)
    def _():
        s_out_ref[...] = s_scr[...]


def _retention(u, cos, sin, state, layer, stack, out_layer, gn, hsum, chunk, valid, bb):
    batch, seq, _ = u.shape
    n_chunks = seq // chunk
    assert valid == chunk or n_chunks == 1
    kern = functools.partial(_ret_kernel, chunk=chunk, valid=valid, n_chunks=n_chunks, bb=bb)
    return pl.pallas_call(
        kern,
        grid=(batch // bb, n_chunks),
        in_specs=[
            pl.BlockSpec((bb, chunk, B_COLS), lambda b, n: (b, n, 0)),
            pl.BlockSpec((chunk, B_HD), lambda b, n: (n, 0)),
            pl.BlockSpec((chunk, B_HD), lambda b, n: (n, 0)),
            pl.BlockSpec((None, bb, B_HEADS, B_HD, B_HD), lambda b, n: (layer, b, 0, 0, 0)),
            pl.BlockSpec((1, BR_W), lambda b, n: (0, 0)),
            pl.BlockSpec((LANES, LANES), lambda b, n: (0, 0)),
            pl.BlockSpec(memory_space=pl.ANY),
        ],
        input_output_aliases={6: 1},
        out_specs=[
            pl.BlockSpec((bb, chunk, BR_W), lambda b, n: (b, n, 0)),
            pl.BlockSpec((None, bb, B_HEADS, B_HD, B_HD), lambda b, n: (out_layer, b, 0, 0, 0)),
        ],
        out_shape=[
            jax.ShapeDtypeStruct((batch, seq, BR_W), _branch_dtype(chunk)),
            jax.ShapeDtypeStruct(stack.shape, F32),
        ],
        scratch_shapes=[pltpu.VMEM((bb, B_HEADS, B_HD, B_HD), F32), pltpu.VMEM((bb * chunk, BR_W), F32)],
        compiler_params=_cparams(("parallel", "arbitrary")),
        name="retention",
    )(u, cos, sin, state, gn, hsum, stack)


def _hgrn_kernel(u_ref, lb_ref, s0_ref, gn_ref, hsum_ref, stack_ref, o_ref, s_out_ref, st_scr, o_scr,
                 *, chunk, valid, n_chunks, bb):
    n = pl.program_id(1)
    rows = bb * chunk
    units = [(i, h) for i in range(bb) for h in range(C_HEADS)]

    @pl.when(n == 0)
    def _():
        for i, h in units:
            st_scr[i, h] = s0_ref[i, h].T

    lb = lb_ref[...]
    row = lax.broadcasted_iota(jnp.int32, (rows, 1), 0)
    sig_f = _sigmoid(u_ref[:, :, BR_W:2 * BR_W].reshape(rows, BR_W))
    lf = jnp.log(jnp.maximum(lb + (1.0 - lb) * sig_f, F_FLOOR))
    kc = (1.0 - lb) * (1.0 - sig_f)
    if valid < chunk:
        ok = (row & (chunk - 1)) < valid
        lf = jnp.where(ok, lf, 0.0)
        kc = jnp.where(ok, kc, 0.0)
    bc = _chunk_cumsum(lf, chunk)
    b_last = _last_rows(bc, chunk, bb)
    e_b = jnp.exp(bc)
    e_last = jnp.exp(b_last)
    k_hat = kc * jnp.exp(b_last - bc)
    q_all = u_ref[:, :, 0:BR_W].reshape(rows, BR_W)
    tr =lax.broadcasted_iota(jnp.int32, (chunk, chunk), 0)
    tc = lax.broadcasted_iota(jnp.int32, (chunk, chunk), 1)
    differ = tr ^ tc
    sub8 = lax.broadcasted_iota(jnp.int32, (8, 1), 0)
    width = min(chunk, C_HD)

    def blk(t, i, h):
        return t[i * chunk:(i + 1) * chunk, h * C_HD:(h + 1) * C_HD]

    def boundary(c):
        pieces = []
        for base in range(0, rows, 8):
            if 2 * c <= 8:
                parts = [jnp.broadcast_to(bc[base + m + c - 1:base + m + c, :], (8, BR_W))
                         for m in range(0, 8, 2 * c)]
                piece = parts[-1]
                for idx in range(len(parts) - 2, -1, -1):
                    piece = jnp.where(sub8 < (idx + 1) * 2 * c, parts[idx], piece)
            else:
                at = (base // (2 * c)) * (2 * c) + c - 1
                piece = jnp.broadcast_to(bc[at:at + 1, :], (8, BR_W))
            pieces.append(piece)
        return jnp.concatenate(pieces, axis=0)

    q, v, st, scores = {}, {}, {}, {}
    diag = _head_sum(q_all * kc, hsum_ref[...], 2)
    for i, h in units:
        v[i, h] = u_ref[i, :, 2 * BR_W + h * C_HD:2 * BR_W + (h + 1) * C_HD]
        q[i, h] = blk(q_all, i, h)
        st[i, h] = st_scr[i, h]
        scores[i, h] = jnp.where(tr == tc, diag[i * chunk:(i + 1) * chunk, h * C_HD:h * C_HD + width], 0.0)
    c = 1
    while c < chunk:
        w = jnp.exp(-jnp.abs(bc - boundary(c)))
        qz = q_all * w
        kz = kc * w
        here = (tr > tc) & (differ >= c) & (differ < 2 * c)
        for i, h in units:
            scores[i, h] = jnp.where(here, _bdot(blk(qz, i, h), blk(kz, i, h), NT), scores[i, h])
        c *= 2

    for i, h in units:
        sl = slice(h * C_HD, (h + 1) * C_HD)
        o = _bdot(scores[i, h], v[i, h]) + _bdot(q[i, h] * blk(e_b, i, h), st[i, h], NT)
        st_scr[i, h] = st[i, h] * e_last[i * chunk:i * chunk + 1, sl] + _bdot(v[i, h], blk(k_hat, i, h), TN)
        o_scr[i * chunk:(i + 1) * chunk, sl] = o

    o = o_scr[...]
    ms = _head_sum(o * o, hsum_ref[...], 1) * (1.0 / C_HD)
    gt = u_ref[:, :, 3 * BR_W:4 * BR_W].reshape(rows, BR_W)
    o = o * lax.rsqrt(ms + GN_EPS) * gn_ref[...] * _silu(gt)
    o_ref[...] = o.astype(o_ref.dtype).reshape(bb, chunk, BR_W)

    @pl.when(n == n_chunks - 1)
    def _():
        for i, h in units:
            s_out_ref[i, h] = st_scr[i, h].T


def _hgrn(u, lb, state, layer, stack, out_layer, gn, hsum, chunk, valid, bb):
    batch, seq, _ = u.shape
    n_chunks = seq // chunk
    assert valid == chunk or n_chunks == 1
    kern = functools.partial(_hgrn_kernel, chunk=chunk, valid=valid, n_chunks=n_chunks, bb=bb)
    return pl.pallas_call(
        kern,
        grid=(batch // bb, n_chunks),
        in_specs=[
            pl.BlockSpec((bb, chunk, C_COLS), lambda b, n: (b, n, 1)),
            pl.BlockSpec((1, BR_W), lambda b, n: (0, 0)),
            pl.BlockSpec((None, bb, C_HEADS, C_HD, C_HD), lambda b, n: (layer, b, 0, 0, 0)),
            pl.BlockSpec((1, BR_W), lambda b, n: (0, 0)),
            pl.BlockSpec((LANES, LANES), lambda b, n: (0, 0)),
            pl.BlockSpec(memory_space=pl.ANY),
        ],
        input_output_aliases={5: 1},
        out_specs=[
            pl.BlockSpec((bb, chunk, BR_W), lambda b, n: (b, n, 0)),
            pl.BlockSpec((None, bb, C_HEADS, C_HD, C_HD), lambda b, n: (out_layer, b, 0, 0, 0)),
        ],
        out_shape=[
            jax.ShapeDtypeStruct((batch, seq, BR_W), _branch_dtype(chunk)),
            jax.ShapeDtypeStruct(stack.shape, F32),
        ],
        scratch_shapes=[pltpu.VMEM((bb, C_HEADS, C_HD, C_HD), F32), pltpu.VMEM((bb * chunk, BR_W), F32)],
        compiler_params=_cparams(("parallel", "arbitrary")),
        name="hgrn2",
    )(u, lb, state, gn, hsum, stack)


def _merge_kernel(oa_ref, ob_ref, oc_ref, g0_ref, g1_ref, g2_ref, x_ref, wb_ref, wo_ref, o_ref):
    m = g0_ref[...] * _bdot(oa_ref[...], wb_ref[0])
    m = m + g1_ref[...] * _bdot(ob_ref[...], wb_ref[1])
    m = m + g2_ref[...] * _bdot(oc_ref[...], wb_ref[2])
    o_ref[...] = x_ref[...] + _bdot(m, wo_ref[...])


def _merge(oa, ob, oc, u, x, wb, wo, layer, tm):
    n = x.shape[0]
    br = pl.BlockSpec((tm, BR_W), lambda i: (i, 0))
    gate = lambda c: pl.BlockSpec((tm, D_MODEL), lambda i: (i, c))
    return pl.pallas_call(
        _merge_kernel,
        grid=(n // tm,),
        in_specs=[br, br, br, gate(0), gate(1), gate(2),
                  pl.BlockSpec((tm, D_MODEL), lambda i: (i, 0)),
                  pl.BlockSpec((None, 3, BR_W, D_MODEL), lambda i: (layer, 0, 0, 0)),
                  pl.BlockSpec((None, D_MODEL, D_MODEL), lambda i: (layer, 0, 0))],
        out_specs=pl.BlockSpec((tm, D_MODEL), lambda i: (i, 0)),
        out_shape=jax.ShapeDtypeStruct((n, D_MODEL), F32),
        compiler_params=_cparams(("parallel",)),
        name="merge",
    )(oa, ob, oc, u, u, u, x, wb, wo)


def _ffn_kernel(x_ref, g_ref, wg_ref, wu_ref, wo_ref, gfin_ref, o_ref, h_ref, *, n_ff, final):
    @pl.when(pl.program_id(1) == 0)
    def _():
        x = x_ref[...]
        ms = jnp.mean(x * x, axis=-1, keepdims=True)
        h_ref[...] = (x * lax.rsqrt(ms + RMS_EPS) * g_ref[...]).astype(BF16)
        o_ref[...] = x

    h = h_ref[...]
    gt = jnp.dot(h, wg_ref[...], preferred_element_type=F32)
    up = jnp.dot(h, wu_ref[...], preferred_element_type=F32)
    o_ref[...] += _bdot(_silu(gt) * up, wo_ref[...])

    if final:
        @pl.when(pl.program_id(1) == n_ff - 1)
        def _():
            x = o_ref[...]
            ms = jnp.mean(x * x, axis=-1, keepdims=True)
            o_ref[...] = x * lax.rsqrt(ms + RMS_EPS) * gfin_ref[...]


def _ffn(x, g, w_in, w_out, layer, g_final, final, tm, tf):
    n = x.shape[0]
    n_ff = D_FF // tf
    resident = {"pipeline_mode": pl.Buffered(1)} if n_ff == 1 else {}
    return pl.pallas_call(
        functools.partial(_ffn_kernel, n_ff=n_ff, final=final),
        grid=(n // tm, n_ff),
        in_specs=[
            pl.BlockSpec((tm, D_MODEL), lambda i, j: (i, 0)),
            pl.BlockSpec((1, D_MODEL), lambda i, j: (0, 0)),
            pl.BlockSpec((None, D_MODEL, tf), lambda i, j: (layer, 0, j), **resident),
            pl.BlockSpec((None, D_MODEL, tf), lambda i, j: (layer, 0, n_ff + j), **resident),
            pl.BlockSpec((None, tf, D_MODEL), lambda i, j: (layer, j, 0), **resident),
            pl.BlockSpec((1, D_MODEL), lambda i, j: (0, 0)),
        ],
        out_specs=pl.BlockSpec((tm, D_MODEL), lambda i, j: (i, 0)),
        out_shape=jax.ShapeDtypeStruct((n, D_MODEL), F32),
        scratch_shapes=[pltpu.VMEM((tm, D_MODEL), BF16)],
        compiler_params=_cparams(("parallel", "arbitrary")),
        name="ffn",
    )(x, g, w_in, w_in, w_out, g_final)


def _rope_tables(pos):
    half = B_HD // 2
    inv = ROPE_BASE ** (-jnp.arange(half, dtype=F32) / half)
    ang = pos.astype(F32)[:, None] * inv[None, :]
    cos, sin = jnp.cos(ang), jnp.sin(ang)
    return jnp.concatenate([cos, cos], axis=-1), jnp.concatenate([-sin, sin], axis=-1)


def _row_tile(n, largest=512):
    t = largest
    while t >= 8:
        if n % t == 0:
            return t
        t //= 2
    raise ValueError(f"row count {n} is not a multiple of 8")


def _layer(x, states, layer, stacks, cos, sin, p, batch, seq, valid, cfg):
    shift, wkv, ret, hg = states
    wkv_stack, ret_stack, hg_stack = stacks
    tm = _row_tile(x.shape[0])
    wl = p["layer"]
    ubc, gates, ua = _in_proj(x, p["norm_mix"], p["w_in"], wl, _row_tile(x.shape[0], IN_ROW_TILE))
    ubc = ubc.reshape(batch, seq, B_COLS + C_COLS)
    ua = ua.reshape(batch, seq, A_COLS)
    full = lambda c: valid if c >= seq else c
    (ca, sa), (cb, sb), (cc, sc) = cfg["rwkv"], cfg["ret"], cfg["hgrn"]
    oa, shift_new, wkv_stack = _rwkv(ua, shift, wkv, layer, wkv_stack, p, ca, full(ca), sa)
    ob, ret_stack = _retention(ubc, cos, sin, ret, layer, ret_stack, wl, p["ret_gn"], p["hsum128"], cb, full(cb), sb)
    oc, hg_stack = _hgrn(ubc, p["lower"], hg, layer, hg_stack, wl, p["hgrn_gn"], p["hsum128"], cc, full(cc), sc)
    flat = lambda o: o.reshape(batch * seq, BR_W)
    x = _merge(flat(oa), flat(ob), flat(oc), gates, x, p["w_branch"], p["w_out"], wl,
               _row_tile(x.shape[0], MERGE_ROW_TILE))
    x = _ffn(x, p["norm_ffn"], p["w_ffn_in"], p["w_ffn_out"], wl, p["norm_final"], p["last"], tm, FF_TILE)
    return x, shift_new, (wkv_stack, ret_stack, hg_stack)


def kernel(x_prompt, x_sample, state_rwkv_shift, state_rwkv_wkv, state_ret, state_hgrn,
           norm_mix, w_in, rwkv_mu, rwkv_w0, rwkv_w2, rwkv_a0, rwkv_a2, rwkv_g2, rwkv_kk,
           rwkv_ka, rwkv_rk, rwkv_ln_w, rwkv_ln_b, ret_gn, hgrn_lb, hgrn_gn, w_branch, w_out,
           norm_ffn, w_ffn_in, w_ffn_out, norm_final):
    depth = w_in.shape[0]
    bp, tp, _ = x_prompt.shape
    bs, ts, _ = x_sample.shape
    assert ts <= SAMPLE_CHUNK
    assert all(tp % c == 0 and bp % s == 0 for c, s in PROMPT_CFG.values())
    assert all(bs % s == 0 for _, s in SAMPLE_CFG.values())
    dt = x_prompt.dtype

    lower = _lower_bounds(hgrn_lb)
    w_in_r = w_in.astype(BF16)
    w_branch_b = w_branch.astype(BF16)
    w_out_b = w_out.astype(BF16)
    w_ffn_in_b = w_ffn_in.astype(BF16)
    w_ffn_out_b = w_ffn_out.astype(BF16)
    lanes = jnp.arange(BR_W)
    hsum = (lanes[:, None] // A_HD == lanes[None, :] // A_HD).astype(BF16)
    hsum128 = (lanes[:, None] // B_HD == lanes[None, :] // B_HD).astype(BF16)
    row = lambda a: a.reshape(1, -1).astype(F32)

    cos_p, sin_p = _rope_tables(jnp.arange(tp, dtype=jnp.int32))
    cos_s, sin_s = _rope_tables(PAST_LEN + jnp.arange(SAMPLE_CHUNK, dtype=jnp.int32))

    xp = x_prompt.reshape(bp * tp, D_MODEL)
    xs = jnp.pad(x_sample, ((0, 0), (0, SAMPLE_CHUNK - ts), (0, 0))).reshape(bs * SAMPLE_CHUNK, D_MODEL)
    zero_states = (jnp.zeros((1, bp, 1, A_COLS), F32), jnp.zeros((1, bp, A_HEADS, A_HD, A_HD), F32),
                   jnp.zeros((1, bp, B_HEADS, B_HD, B_HD), F32), jnp.zeros((1, bp, C_HEADS, C_HD, C_HD), F32))
    sample_states = (state_rwkv_shift.reshape(depth, bs, 1, A_COLS), state_rwkv_wkv, state_ret, state_hgrn)

    new_states = lambda b: (jnp.zeros((depth, b, A_HEADS, A_HD, A_HD), F32),
                            jnp.zeros((depth, b, B_HEADS, B_HD, B_HD), F32),
                            jnp.zeros((depth, b, C_HEADS, C_HD, C_HD), F32))
    stacks_p, stacks_s = new_states(bp), new_states(bs)
    shifts_p, shifts_s = [], []
    for l in range(depth):
        p = {
            "layer": l, "norm_mix": row(norm_mix[l]), "w_in": w_in_r, "mu": row(rwkv_mu[l]),
            "w0": row(rwkv_w0[l]), "w2": rwkv_w2[l].astype(BF16), "a0": row(rwkv_a0[l]),
            "a2": rwkv_a2[l].astype(BF16), "g2": rwkv_g2[l].astype(BF16), "kk": row(rwkv_kk[l]),
            "ka": row(rwkv_ka[l]), "rk": row(rwkv_rk[l]), "lnw": row(rwkv_ln_w[l]),
            "lnb": row(rwkv_ln_b[l]), "hsum": hsum, "hsum128": hsum128, "ret_gn": row(ret_gn[l]),
            "lower": lower[l:l + 1], "hgrn_gn": row(hgrn_gn[l]), "w_branch": w_branch_b,
            "w_out": w_out_b, "norm_ffn": row(norm_ffn[l]), "w_ffn_in": w_ffn_in_b,
            "w_ffn_out": w_ffn_out_b, "norm_final": row(norm_final), "last": l == depth - 1,
        }
        xp, sh_p, stacks_p = _layer(xp, zero_states, 0, stacks_p, cos_p, sin_p, p, bp, tp, tp, PROMPT_CFG)
        xs, sh_s, stacks_s = _layer(xs, sample_states, l, stacks_s, cos_s, sin_s, p, bs, SAMPLE_CHUNK, ts,
                                    SAMPLE_CFG)
        shifts_p.append(sh_p.reshape(bp, A_COLS))
        shifts_s.append(sh_s.reshape(bs, A_COLS))

    y_prompt = xp.reshape(bp, tp, D_MODEL)
    y_sample = xs.reshape(bs, SAMPLE_CHUNK, D_MODEL)[:, :ts]
    cast = lambda a: a.astype(dt)
    return (y_prompt, y_sample, cast(jnp.stack(shifts_p)), *map(cast, stacks_p),
            cast(jnp.stack(shifts_s)), *map(cast, stacks_s))
```

```python
import functools
import math

import jax
import jax.numpy as jnp
from jax import lax
from jax.experimental import pallas as pl
from jax.experimental.pallas import tpu as pltpu

F32 = jnp.float32
BF16 = jnp.bfloat16

D_MODEL = 1024
BR_W = 512
A_HD = 64
A_HEADS = BR_W // A_HD
LORA_W = 64
LORA_A = 64
LORA_G = 128
A_COLS = 3 * BR_W + LORA_W + LORA_A + LORA_G
B_HD = 128
B_HEADS = BR_W // B_HD
C_HD = 128
C_HEADS = BR_W // C_HD
B_COLS = 4 * BR_W
C_COLS = 4 * BR_W
GATE_COLS = 3 * D_MODEL
IN_COLS = A_COLS + B_COLS + C_COLS + GATE_COLS
D_FF = 2816
PAST_LEN = 16384
RMS_EPS = 1e-6
RWKV_GN_EPS = 64e-5
GN_EPS = 1e-5
F_FLOOR = 1e-30
ROPE_BASE = 10000.0

W_B_OFF = A_COLS
W_G_OFF = A_COLS + B_COLS + C_COLS

SAMPLE_CHUNK = 8
PROMPT_CFG = {"rwkv": (64, 8), "ret": (128, 8), "hgrn": (128, 4)}
SAMPLE_CFG = {"rwkv": (SAMPLE_CHUNK, 4), "ret": (SAMPLE_CHUNK, 16), "hgrn": (SAMPLE_CHUNK, 16)}
RWKV_GROUP = 8
IN_ROW_TILE = 256
FF_TILE = D_FF
MERGE_ROW_TILE = 1024

VMEM_LIMIT = 56 * 1024 * 1024
LANES = 128

NN = (((1,), (0,)), ((), ()))
NT = (((1,), (1,)), ((), ()))
TN = (((0,), (0,)), ((), ()))


def _bdot(a, b, dims=NN):
    return lax.dot_general(a.astype(BF16), b.astype(BF16), dims, preferred_element_type=F32)


def _sel_dot(sel, x, passes, sel_first=True):
    sel = sel.astype(BF16)
    acc = None
    for _ in range(passes):
        piece = x.astype(BF16)
        ops = (sel, piece) if sel_first else (piece, sel)
        part = lax.dot_general(*ops, NN, preferred_element_type=F32)
        acc = part if acc is None else acc + part
        x = x - piece.astype(F32)
    return acc


def _branch_dtype(chunk):
    return BF16 if chunk % 16 == 0 else F32


def _head_sum(x, same_head, passes):
    return jnp.concatenate([_sel_dot(same_head, x[:, g:g + LANES], passes, sel_first=False)
                            for g in range(0, x.shape[1], LANES)], axis=1)


def _sigmoid(x):
    return 0.5 * jnp.tanh(0.5 * x) + 0.5


def _silu(x):
    return x * _sigmoid(x)


def _cparams(sem, n_inputs=0, fuse=()):
    fusion = [i in fuse for i in range(n_inputs)] if fuse else None
    return pltpu.CompilerParams(dimension_semantics=sem, vmem_limit_bytes=VMEM_LIMIT, allow_input_fusion=fusion)


def _chunk_cumsum(x, chunk):
    rows = x.shape[0]
    group = max(chunk, min(rows, 64))
    assert rows % group == 0
    tr = lax.broadcasted_iota(jnp.int32, (group, group), 0)
    tc = lax.broadcasted_iota(jnp.int32, (group, group), 1)
    sh = chunk.bit_length() - 1
    tri = jnp.where((tr >= tc) & ((tr >> sh) == (tc >> sh)), 1.0, 0.0)
    return jnp.concatenate([_sel_dot(tri, x[r:r + group], 3) for r in range(0, rows, group)], axis=0)


def _last_rows(x, chunk, bb):
    return jnp.concatenate(
        [jnp.broadcast_to(x[(i + 1) * chunk - 1:(i + 1) * chunk, :], (chunk, x.shape[1])) for i in range(bb)],
        axis=0)


def _lower_kernel(lb_ref, o_ref):
    x = lb_ref[...]
    depth = x.shape[0]
    m = x[0:1]
    for l in range(1, depth):
        m = jnp.maximum(m, x[l:l + 1])
    e = jnp.exp(x - m)
    tot = e[0:1]
    for l in range(1, depth):
        tot = tot + e[l:l + 1]
    sm = e / tot
    acc = jnp.zeros_like(m)
    for l in range(depth):
        acc = acc + sm[l:l + 1]
        o_ref[l:l + 1, :] = acc - sm[0:1]


def _lower_bounds(hgrn_lb):
    return pl.pallas_call(
        _lower_kernel, out_shape=jax.ShapeDtypeStruct(hgrn_lb.shape, F32), name="hgrn_lower",
    )(hgrn_lb.astype(F32))


def _in_proj_kernel(x_ref, g_ref, w_ref, ubc_ref, gate_ref, ua_ref):
    x = x_ref[...]
    ms = jnp.mean(x * x, axis=-1, keepdims=True)
    h = (x * lax.rsqrt(ms + RMS_EPS) * g_ref[...]).astype(BF16)
    proj = lambda lo, hi: jnp.dot(h, w_ref[:, lo:hi], preferred_element_type=F32)
    ua_ref[...] = proj(0, W_B_OFF)
    ubc_ref[...] = proj(W_B_OFF, W_G_OFF)
    gate_ref[...] = _sigmoid(proj(W_G_OFF, IN_COLS)).astype(BF16)


def _in_proj(x, g, w, layer, tm):
    n, d = x.shape
    rows = lambda width: pl.BlockSpec((tm, width), lambda i: (i, 0))
    return pl.pallas_call(
        _in_proj_kernel,
        grid=(n // tm,),
        in_specs=[
            rows(d),
            pl.BlockSpec((1, d), lambda i: (0, 0)),
            pl.BlockSpec((None, d, IN_COLS), lambda i: (layer, 0, 0), pipeline_mode=pl.Buffered(1)),
        ],
        out_specs=[rows(B_COLS + C_COLS), rows(GATE_COLS), rows(A_COLS)],
        out_shape=[
            jax.ShapeDtypeStruct((n, B_COLS + C_COLS), F32),
            jax.ShapeDtypeStruct((n, GATE_COLS), BF16),
            jax.ShapeDtypeStruct((n, A_COLS), F32),
        ],
        compiler_params=_cparams(("parallel",), 3, fuse=(2,)),
        name="in_proj",
    )(x, g, w)


def _rwkv_kernel(u_ref, shift_ref, s0_ref, mu_ref, w0_ref, w2_ref, a0_ref, a2_ref, g2_ref,
                 kk_ref, ka_ref, rk_ref, lnw_ref, lnb_ref, hsum_ref, stack_ref,
                 o_ref, shift_out_ref, s_out_ref, st_scr, prev_scr, o_scr, *, chunk, valid, n_chunks, bb, gb):
    n = pl.program_id(1)

    @pl.when(n == 0)
    def _():
        for i in range(bb):
            for h in range(A_HEADS):
                st_scr[i, h] = s0_ref[i, h].T
        prev_scr[...] = shift_ref[...]

    for g0 in range(0, bb, gb):
        seqs = pl.ds(g0, gb)
        _rwkv_group(u_ref.at[seqs], mu_ref, w0_ref, w2_ref, a0_ref, a2_ref, g2_ref, kk_ref, ka_ref, rk_ref,
                    lnw_ref, lnb_ref, hsum_ref, o_ref.at[seqs], st_scr.at[seqs], prev_scr.at[seqs],
                    o_scr.at[pl.ds(g0 * chunk, gb * chunk)], chunk=chunk, valid=valid, bb=gb)

    @pl.when(n == n_chunks - 1)
    def _():
        last = (valid - 1) if n_chunks == 1 else (chunk - 1)
        for i in range(bb):
            shift_out_ref[i] = u_ref[i, last:last + 1, :]
            for h in range(A_HEADS):
                s_out_ref[i, h] = st_scr[i, h].T


def _rwkv_group(u_ref, mu_ref, w0_ref, w2_ref, a0_ref, a2_ref, g2_ref, kk_ref, ka_ref, rk_ref,
                lnw_ref, lnb_ref, hsum_ref, o_ref, st_scr, prev_scr, o_scr, *, chunk, valid, bb):
    rows = bb * chunk
    units = [(i, h) for i in range(bb) for h in range(A_HEADS)]
    u = u_ref[...].reshape(rows, A_COLS)
    row = lax.broadcasted_iota(jnp.int32, (rows, 1), 0)
    pos = row & (chunk - 1)
    prev = jnp.concatenate([jnp.broadcast_to(prev_scr[i], (chunk, A_COLS)) for i in range(bb)], axis=0)
    shifted = jnp.where(pos == 0, prev, pltpu.roll(u, 1, axis=0))
    xm = u + mu_ref[...] * (shifted - u)
    for i in range(bb):
        prev_scr[i] = u[(i + 1) * chunk - 1:(i + 1) * chunk, :]

    r = xm[:, 0:BR_W]
    k = xm[:, BR_W:2 * BR_W]
    v = xm[:, 2 * BR_W:3 * BR_W]
    wl = xm[:, 3 * BR_W:3 * BR_W + LORA_W]
    al = xm[:, 3 * BR_W + LORA_W:3 * BR_W + LORA_W + LORA_A]
    gl = xm[:, 3 * BR_W + LORA_W + LORA_A:A_COLS]

    z = w0_ref[...] + _bdot(jnp.tanh(wl), w2_ref[...])
    ld = -math.exp(-0.5) * _sigmoid(z)
    a = _sigmoid(a0_ref[...] + _bdot(al, a2_ref[...]))
    g = _bdot(_sigmoid(gl), g2_ref[...])
    kk = k * kk_ref[...]
    ss = _head_sum(kk * kk, hsum_ref[...], 2)
    kk = kk * lax.rsqrt(jnp.maximum(ss, 1e-24))
    k = k * (1.0 + (a - 1.0) * ka_ref[...])
    if valid < chunk:
        ok = pos < valid
        ld = jnp.where(ok, ld, 0.0)
        kk = jnp.where(ok, kk, 0.0)
        k = jnp.where(ok, k, 0.0)
    bv = kk * a

    cl = _chunk_cumsum(ld, chunk)
    cl_last = _last_rows(cl, chunk, bb)
    e_ncl = jnp.exp(-cl)
    e_rem = jnp.exp(cl_last - cl)
    alpha_t = -kk * jnp.exp(cl - ld)
    r_t = r * jnp.exp(cl)
    b_bar = bv * e_ncl
    k_bar = k * e_ncl
    b_hat = bv * e_rem
    k_hat = k * e_rem
    e_last = jnp.exp(cl_last)
    rkk = r * k * rk_ref[...]

    qr = lax.broadcasted_iota(jnp.int32, (2 * chunk, 2 * chunk), 0)
    qc = lax.broadcasted_iota(jnp.int32, (2 * chunk, 2 * chunk), 1)
    qt, qs = qr & (chunk - 1), qc & (chunk - 1)
    keep = (qt > qs) | ((qr >= chunk) & (qt == qs))

    def blk(t, i, h):
        return t[i * chunk:(i + 1) * chunk, h * A_HD:(h + 1) * A_HD]

    aa, st0, vv, x, p = {}, {}, {}, {}, {}
    for i, h in units:
        lhs = jnp.concatenate([blk(alpha_t, i, h), blk(r_t, i, h)], axis=0)
        rhs = jnp.concatenate([blk(b_bar, i, h), blk(k_bar, i, h)], axis=0)
        aa[i, h] = jnp.where(keep, _bdot(lhs, rhs, NT), 0.0)
        st0[i, h] = st_scr[i, h]
        vv[i, h] = blk(v, i, h)
    for i, h in units:
        lhs = jnp.concatenate([blk(alpha_t, i, h), aa[i, h][0:chunk, chunk:2 * chunk]], axis=1)
        x[i, h] = _bdot(lhs, jnp.concatenate([st0[i, h], vv[i, h]], axis=0))
        p[i, h] = aa[i, h][0:chunk, 0:chunk]
    for i, h in units:
        x[i, h] = x[i, h] + _bdot(p[i, h], x[i, h])
    span = 2
    while span < chunk:
        for i, h in units:
            p[i, h] = _bdot(p[i, h], p[i, h])
        for i, h in units:
            x[i, h] = x[i, h] + _bdot(p[i, h], x[i, h])
        span *= 2
    for i, h in units:
        sl = slice(h * A_HD, (h + 1) * A_HD)
        lhs = jnp.concatenate([blk(r_t, i, h), aa[i, h][chunk:2 * chunk, :]], axis=1)
        o = _bdot(lhs, jnp.concatenate([st0[i, h], x[i, h], vv[i, h]], axis=0))
        upd = _bdot(jnp.concatenate([blk(b_hat, i, h), blk(k_hat, i, h)], axis=0),
                    jnp.concatenate([x[i, h], vv[i, h]], axis=0), TN)
        dec = jnp.broadcast_to(e_last[i * chunk:i * chunk + 1, sl], (A_HD, A_HD)).T
        st_scr[i, h] = st0[i, h] * dec + upd
        o_scr[i * chunk:(i + 1) * chunk, sl] = o

    hsum = hsum_ref[...]
    o = o_scr[...]
    d = o - _head_sum(o, hsum, 2) * (1.0 / A_HD)
    var = _head_sum(d * d, hsum, 1) * (1.0 / A_HD)
    o = d * lax.rsqrt(var + RWKV_GN_EPS) * lnw_ref[...] + lnb_ref[...]
    o = o + _head_sum(rkk, hsum, 1) * v
    o_ref[...] = (o * g).astype(o_ref.dtype).reshape(bb, chunk, BR_W)


def _rwkv(u, shift, wkv, layer, stack, p, chunk, valid, bb):
    batch, seq, _ = u.shape
    n_chunks = seq // chunk
    assert valid == chunk or n_chunks == 1
    a_blk = 0
    vec = lambda w: pl.BlockSpec((1, w), lambda b, n: (0, 0))
    full = lambda s: pl.BlockSpec(s, lambda b, n: (0,) * len(s))
    gb = math.gcd(bb, RWKV_GROUP)
    kern = functools.partial(_rwkv_kernel, chunk=chunk, valid=valid, n_chunks=n_chunks, bb=bb, gb=gb)
    return pl.pallas_call(
        kern,
        grid=(batch // bb, n_chunks),
        in_specs=[
            pl.BlockSpec((bb, chunk, A_COLS), lambda b, n: (b, n, a_blk)),
            pl.BlockSpec((None, bb, 1, A_COLS), lambda b, n: (layer, b, 0, 0)),
            pl.BlockSpec((None, bb, A_HEADS, A_HD, A_HD), lambda b, n: (layer, b, 0, 0, 0)),
            vec(A_COLS), vec(BR_W), full((LORA_W, BR_W)), vec(BR_W), full((LORA_A, BR_W)),
            full((LORA_G, BR_W)), vec(BR_W), vec(BR_W), vec(BR_W), vec(BR_W), vec(BR_W),
            pl.BlockSpec((LANES, LANES), lambda b, n: (0, 0)),
            pl.BlockSpec(memory_space=pl.ANY),
        ],
        input_output_aliases={15: 2},
        out_specs=[
            pl.BlockSpec((bb, chunk, BR_W), lambda b, n: (b, n, 0)),
            pl.BlockSpec((bb, 1, A_COLS), lambda b, n: (b, 0, 0)),
            pl.BlockSpec((None, bb, A_HEADS, A_HD, A_HD), lambda b, n: (p["layer"], b, 0, 0, 0)),
        ],
        out_shape=[
            jax.ShapeDtypeStruct((batch, seq, BR_W), _branch_dtype(chunk)),
            jax.ShapeDtypeStruct((batch, 1, A_COLS), F32),
            jax.ShapeDtypeStruct(stack.shape, F32),
        ],
        scratch_shapes=[pltpu.VMEM((bb, A_HEADS, A_HD, A_HD), F32), pltpu.VMEM((bb, 1, A_COLS), F32),
                        pltpu.VMEM((bb * chunk, BR_W), F32)],
        compiler_params=_cparams(("parallel", "arbitrary")),
        name="rwkv7",
    )(u, shift, wkv, p["mu"], p["w0"], p["w2"], p["a0"], p["a2"], p["g2"], p["kk"], p["ka"],
      p["rk"], p["lnw"], p["lnb"], p["hsum"], stack)


def _ret_kernel(u_ref, cos_ref, sin_ref, s0_ref, gn_ref, hsum_ref, stack_ref, o_ref, s_out_ref, s_scr, o_scr,
                *, chunk, valid, n_chunks, bb):
    n = pl.program_id(1)
    units = [(i, h) for i in range(bb) for h in range(B_HEADS)]

    @pl.when(n == 0)
    def _():
        s_scr[...] = s0_ref[...]

    cos = cos_ref[...]
    sin = sin_ref[...]
    tr = lax.broadcasted_iota(jnp.int32, (chunk, chunk), 0)
    tc = lax.broadcasted_iota(jnp.int32, (chunk, chunk), 1)
    diff = (tr - tc).astype(F32)
    causal = tr >= tc
    pos = lax.broadcasted_iota(jnp.int32, (chunk, 1), 0)
    posf = pos.astype(F32)
    lgs = [math.log(1.0 - 2.0 ** (-5.0 - h)) for h in range(B_HEADS)]
    intra = [jnp.where(causal, jnp.exp(lg * jnp.maximum(diff, 0.0)), 0.0) for lg in lgs]
    q_dec = [jnp.exp(lg * (posf + 1.0)) for lg in lgs]
    k_dec = [jnp.where(pos < valid, jnp.exp(lg * (valid - 1.0 - posf)), 0.0) for lg in lgs]
    s_dec = [math.exp(lg * valid) for lg in lgs]

    q, k, v, s0, sc = {}, {}, {}, {}, {}
    for i, h in units:
        ui = u_ref[i]
        qh = ui[:, h * B_HD:(h + 1) * B_HD]
        kh = ui[:, BR_W + h * B_HD:BR_W + (h + 1) * B_HD]
        q[i, h] = qh * cos + pltpu.roll(qh, B_HD // 2, axis=1) * sin
        k[i, h] = (kh * cos + pltpu.roll(kh, B_HD // 2, axis=1) * sin) * (B_HD ** -0.5)
        v[i, h] = ui[:, 2 * BR_W + h * B_HD:2 * BR_W + (h + 1) * B_HD]
        s0[i, h] = s_scr[i, h]
    for i, h in units:
        sc[i, h] = _bdot(q[i, h], k[i, h], NT) * intra[h]
    for i, h in units:
        sl = slice(h * B_HD, (h + 1) * B_HD)
        o = _bdot(sc[i, h], v[i, h]) + _bdot(q[i, h] * q_dec[h], s0[i, h])
        s_scr[i, h] = s0[i, h] * s_dec[h] + _bdot(k[i, h] * k_dec[h], v[i, h], TN)
        o_scr[i * chunk:(i + 1) * chunk, sl] = o

    hsum = hsum_ref[...]
    o = o_scr[...]
    d = o - _head_sum(o, hsum, 2) * (1.0 / B_HD)
    var = _head_sum(d * d, hsum, 1) * (1.0 / B_HD)
    gt = u_ref[:, :, 3 * BR_W:4 * BR_W].reshape(bb * chunk, BR_W)
    o = d * lax.rsqrt(var + GN_EPS) * gn_ref[...] * _silu(gt)
    o_ref[...] = o.astype(o_ref.dtype).reshape(bb, chunk, BR_W)

    @pl.when(n == n_chunks - 1)
    def _():
        s_out_ref[...] = s_scr[...]


def _retention(u, cos, sin, state, layer, stack, out_layer, gn, hsum, chunk, valid, bb):
    batch, seq, _ = u.shape
    n_chunks = seq // chunk
    assert valid == chunk or n_chunks == 1
    kern = functools.partial(_ret_kernel, chunk=chunk, valid=valid, n_chunks=n_chunks, bb=bb)
    return pl.pallas_call(
        kern,
        grid=(batch // bb, n_chunks),
        in_specs=[
            pl.BlockSpec((bb, chunk, B_COLS), lambda b, n: (b, n, 0)),
            pl.BlockSpec((chunk, B_HD), lambda b, n: (n, 0)),
            pl.BlockSpec((chunk, B_HD), lambda b, n: (n, 0)),
            pl.BlockSpec((None, bb, B_HEADS, B_HD, B_HD), lambda b, n: (layer, b, 0, 0, 0)),
            pl.BlockSpec((1, BR_W), lambda b, n: (0, 0)),
            pl.BlockSpec((LANES, LANES), lambda b, n: (0, 0)),
            pl.BlockSpec(memory_space=pl.ANY),
        ],
        input_output_aliases={6: 1},
        out_specs=[
            pl.BlockSpec((bb, chunk, BR_W), lambda b, n: (b, n, 0)),
            pl.BlockSpec((None, bb, B_HEADS, B_HD, B_HD), lambda b, n: (out_layer, b, 0, 0, 0)),
        ],
        out_shape=[
            jax.ShapeDtypeStruct((batch, seq, BR_W), _branch_dtype(chunk)),
            jax.ShapeDtypeStruct(stack.shape, F32),
        ],
        scratch_shapes=[pltpu.VMEM((bb, B_HEADS, B_HD, B_HD), F32), pltpu.VMEM((bb * chunk, BR_W), F32)],
        compiler_params=_cparams(("parallel", "arbitrary")),
        name="retention",
    )(u, cos, sin, state, gn, hsum, stack)


def _hgrn_kernel(u_ref, lb_ref, s0_ref, gn_ref, hsum_ref, stack_ref, o_ref, s_out_ref, st_scr, o_scr,
                 *, chunk, valid, n_chunks, bb):
    n = pl.program_id(1)
    rows = bb * chunk
    units = [(i, h) for i in range(bb) for h in range(C_HEADS)]

    @pl.when(n == 0)
    def _():
        for i, h in units:
            st_scr[i, h] = s0_ref[i, h].T

    lb = lb_ref[...]
    row = lax.broadcasted_iota(jnp.int32, (rows, 1), 0)
    sig_f = _sigmoid(u_ref[:, :, BR_W:2 * BR_W].reshape(rows, BR_W))
    lf = jnp.log(jnp.maximum(lb + (1.0 - lb) * sig_f, F_FLOOR))
    kc = (1.0 - lb) * (1.0 - sig_f)
    if valid < chunk:
        ok = (row & (chunk - 1)) < valid
        lf = jnp.where(ok, lf, 0.0)
        kc = jnp.where(ok, kc, 0.0)
    bc = _chunk_cumsum(lf, chunk)
    b_last = _last_rows(bc, chunk, bb)
    e_b = jnp.exp(bc)
    e_last = jnp.exp(b_last)
    k_hat = kc * jnp.exp(b_last - bc)
    q_all = u_ref[:, :, 0:BR_W].reshape(rows, BR_W)
    tr =lax.broadcasted_iota(jnp.int32, (chunk, chunk), 0)
    tc = lax.broadcasted_iota(jnp.int32, (chunk, chunk), 1)
    differ = tr ^ tc
    sub8 = lax.broadcasted_iota(jnp.int32, (8, 1), 0)
    width = min(chunk, C_HD)

    def blk(t, i, h):
        return t[i * chunk:(i + 1) * chunk, h * C_HD:(h + 1) * C_HD]

    def boundary(c):
        pieces = []
        for base in range(0, rows, 8):
            if 2 * c <= 8:
                parts = [jnp.broadcast_to(bc[base + m + c - 1:base + m + c, :], (8, BR_W))
                         for m in range(0, 8, 2 * c)]
                piece = parts[-1]
                for idx in range(len(parts) - 2, -1, -1):
                    piece = jnp.where(sub8 < (idx + 1) * 2 * c, parts[idx], piece)
            else:
                at = (base // (2 * c)) * (2 * c) + c - 1
                piece = jnp.broadcast_to(bc[at:at + 1, :], (8, BR_W))
            pieces.append(piece)
        return jnp.concatenate(pieces, axis=0)

    q, v, st, scores = {}, {}, {}, {}
    diag = _head_sum(q_all * kc, hsum_ref[...], 2)
    for i, h in units:
        v[i, h] = u_ref[i, :, 2 * BR_W + h * C_HD:2 * BR_W + (h + 1) * C_HD]
        q[i, h] = blk(q_all, i, h)
        st[i, h] = st_scr[i, h]
        scores[i, h] = jnp.where(tr == tc, diag[i * chunk:(i + 1) * chunk, h * C_HD:h * C_HD + width], 0.0)
    c = 1
    while c < chunk:
        w = jnp.exp(-jnp.abs(bc - boundary(c)))
        qz = q_all * w
        kz = kc * w
        here = (tr > tc) & (differ >= c) & (differ < 2 * c)
        for i, h in units:
            scores[i, h] = jnp.where(here, _bdot(blk(qz, i, h), blk(kz, i, h), NT), scores[i, h])
        c *= 2

    for i, h in units:
        sl = slice(h * C_HD, (h + 1) * C_HD)
        o = _bdot(scores[i, h], v[i, h]) + _bdot(q[i, h] * blk(e_b, i, h), st[i, h], NT)
        st_scr[i, h] = st[i, h] * e_last[i * chunk:i * chunk + 1, sl] + _bdot(v[i, h], blk(k_hat, i, h), TN)
        o_scr[i * chunk:(i + 1) * chunk, sl] = o

    o = o_scr[...]
    ms = _head_sum(o * o, hsum_ref[...], 1) * (1.0 / C_HD)
    gt = u_ref[:, :, 3 * BR_W:4 * BR_W].reshape(rows, BR_W)
    o = o * lax.rsqrt(ms + GN_EPS) * gn_ref[...] * _silu(gt)
    o_ref[...] = o.astype(o_ref.dtype).reshape(bb, chunk, BR_W)

    @pl.when(n == n_chunks - 1)
    def _():
        for i, h in units:
            s_out_ref[i, h] = st_scr[i, h].T


def _hgrn(u, lb, state, layer, stack, out_layer, gn, hsum, chunk, valid, bb):
    batch, seq, _ = u.shape
    n_chunks = seq // chunk
    assert valid == chunk or n_chunks == 1
    kern = functools.partial(_hgrn_kernel, chunk=chunk, valid=valid, n_chunks=n_chunks, bb=bb)
    return pl.pallas_call(
        kern,
        grid=(batch // bb, n_chunks),
        in_specs=[
            pl.BlockSpec((bb, chunk, C_COLS), lambda b, n: (b, n, 1)),
            pl.BlockSpec((1, BR_W), lambda b, n: (0, 0)),
            pl.BlockSpec((None, bb, C_HEADS, C_HD, C_HD), lambda b, n: (layer, b, 0, 0, 0)),
            pl.BlockSpec((1, BR_W), lambda b, n: (0, 0)),
            pl.BlockSpec((LANES, LANES), lambda b, n: (0, 0)),
            pl.BlockSpec(memory_space=pl.ANY),
        ],
        input_output_aliases={5: 1},
        out_specs=[
            pl.BlockSpec((bb, chunk, BR_W), lambda b, n: (b, n, 0)),
            pl.BlockSpec((None, bb, C_HEADS, C_HD, C_HD), lambda b, n: (out_layer, b, 0, 0, 0)),
        ],
        out_shape=[
            jax.ShapeDtypeStruct((batch, seq, BR_W), _branch_dtype(chunk)),
            jax.ShapeDtypeStruct(stack.shape, F32),
        ],
        scratch_shapes=[pltpu.VMEM((bb, C_HEADS, C_HD, C_HD), F32), pltpu.VMEM((bb * chunk, BR_W), F32)],
        compiler_params=_cparams(("parallel", "arbitrary")),
        name="hgrn2",
    )(u, lb, state, gn, hsum, stack)


def _merge_kernel(oa_ref, ob_ref, oc_ref, g0_ref, g1_ref, g2_ref, x_ref, wb_ref, wo_ref, o_ref):
    m = g0_ref[...] * _bdot(oa_ref[...], wb_ref[0])
    m = m + g1_ref[...] * _bdot(ob_ref[...], wb_ref[1])
    m = m + g2_ref[...] * _bdot(oc_ref[...], wb_ref[2])
    o_ref[...] = x_ref[...] + _bdot(m, wo_ref[...])


def _merge(oa, ob, oc, u, x, wb, wo, layer, tm):
    n = x.shape[0]
    br = pl.BlockSpec((tm, BR_W), lambda i: (i, 0))
    gate = lambda c: pl.BlockSpec((tm, D_MODEL), lambda i: (i, c))
    return pl.pallas_call(
        _merge_kernel,
        grid=(n // tm,),
        in_specs=[br, br, br, gate(0), gate(1), gate(2),
                  pl.BlockSpec((tm, D_MODEL), lambda i: (i, 0)),
                  pl.BlockSpec((None, 3, BR_W, D_MODEL), lambda i: (layer, 0, 0, 0)),
                  pl.BlockSpec((None, D_MODEL, D_MODEL), lambda i: (layer, 0, 0))],
        out_specs=pl.BlockSpec((tm, D_MODEL), lambda i: (i, 0)),
        out_shape=jax.ShapeDtypeStruct((n, D_MODEL), F32),
        compiler_params=_cparams(("parallel",), 9, fuse=(7, 8)),
        name="merge",
    )(oa, ob, oc, u, u, u, x, wb, wo)


def _ffn_kernel(x_ref, g_ref, wg_ref, wu_ref, wo_ref, gfin_ref, o_ref, h_ref, *, n_ff, final):
    @pl.when(pl.program_id(1) == 0)
    def _():
        x = x_ref[...]
        ms = jnp.mean(x * x, axis=-1, keepdims=True)
        h_ref[...] = (x * lax.rsqrt(ms + RMS_EPS) * g_ref[...]).astype(BF16)
        o_ref[...] = x

    h = h_ref[...]
    gt = jnp.dot(h, wg_ref[...], preferred_element_type=F32)
    up = jnp.dot(h, wu_ref[...], preferred_element_type=F32)
    o_ref[...] += _bdot(_silu(gt) * up, wo_ref[...])

    if final:
        @pl.when(pl.program_id(1) == n_ff - 1)
        def _():
            x = o_ref[...]
            ms = jnp.mean(x * x, axis=-1, keepdims=True)
            o_ref[...] = x * lax.rsqrt(ms + RMS_EPS) * gfin_ref[...]


def _ffn(x, g, w_in, w_out, layer, g_final, final, tm, tf):
    n = x.shape[0]
    n_ff = D_FF // tf
    resident = {"pipeline_mode": pl.Buffered(1)} if n_ff == 1 else {}
    return pl.pallas_call(
        functools.partial(_ffn_kernel, n_ff=n_ff, final=final),
        grid=(n // tm, n_ff),
        in_specs=[
            pl.BlockSpec((tm, D_MODEL), lambda i, j: (i, 0)),
            pl.BlockSpec((1, D_MODEL), lambda i, j: (0, 0)),
            pl.BlockSpec((None, D_MODEL, tf), lambda i, j: (layer, 0, j), **resident),
            pl.BlockSpec((None, D_MODEL, tf), lambda i, j: (layer, 0, n_ff + j), **resident),
            pl.BlockSpec((None, tf, D_MODEL), lambda i, j: (layer, j, 0), **resident),
            pl.BlockSpec((1, D_MODEL), lambda i, j: (0, 0)),
        ],
        out_specs=pl.BlockSpec((tm, D_MODEL), lambda i, j: (i, 0)),
        out_shape=jax.ShapeDtypeStruct((n, D_MODEL), F32),
        scratch_shapes=[pltpu.VMEM((tm, D_MODEL), BF16)],
        compiler_params=_cparams(("parallel", "arbitrary"), 6, fuse=(2, 3, 4)),
        name="ffn",
    )(x, g, w_in, w_in, w_out, g_final)


def _rope_tables(pos):
    half = B_HD // 2
    inv = ROPE_BASE ** (-jnp.arange(half, dtype=F32) / half)
    ang = pos.astype(F32)[:, None] * inv[None, :]
    cos, sin = jnp.cos(ang), jnp.sin(ang)
    return jnp.concatenate([cos, cos], axis=-1), jnp.concatenate([-sin, sin], axis=-1)


def _row_tile(n, largest=512):
    t = largest
    while t >= 8:
        if n % t == 0:
            return t
        t //= 2
    raise ValueError(f"row count {n} is not a multiple of 8")


def _layer(x, states, layer, stacks, cos, sin, p, batch, seq, valid, cfg):
    shift, wkv, ret, hg = states
    wkv_stack, ret_stack, hg_stack = stacks
    tm = _row_tile(x.shape[0])
    wl = p["layer"]
    ubc, gates, ua = _in_proj(x, p["norm_mix"], p["w_in"], wl, _row_tile(x.shape[0], IN_ROW_TILE))
    ubc = ubc.reshape(batch, seq, B_COLS + C_COLS)
    ua = ua.reshape(batch, seq, A_COLS)
    full = lambda c: valid if c >= seq else c
    (ca, sa), (cb, sb), (cc, sc) = cfg["rwkv"], cfg["ret"], cfg["hgrn"]
    oa, shift_new, wkv_stack = _rwkv(ua, shift, wkv, layer, wkv_stack, p, ca, full(ca), sa)
    ob, ret_stack = _retention(ubc, cos, sin, ret, layer, ret_stack, wl, p["ret_gn"], p["hsum128"], cb, full(cb), sb)
    oc, hg_stack = _hgrn(ubc, p["lower"], hg, layer, hg_stack, wl, p["hgrn_gn"], p["hsum128"], cc, full(cc), sc)
    flat = lambda o: o.reshape(batch * seq, BR_W)
    x = _merge(flat(oa), flat(ob), flat(oc), gates, x, p["w_branch"], p["w_out"], wl,
               _row_tile(x.shape[0], MERGE_ROW_TILE))
    x = _ffn(x, p["norm_ffn"], p["w_ffn_in"], p["w_ffn_out"], wl, p["norm_final"], p["last"], tm, FF_TILE)
    return x, shift_new, (wkv_stack, ret_stack, hg_stack)


def kernel(x_prompt, x_sample, state_rwkv_shift, state_rwkv_wkv, state_ret, state_hgrn,
           norm_mix, w_in, rwkv_mu, rwkv_w0, rwkv_w2, rwkv_a0, rwkv_a2, rwkv_g2, rwkv_kk,
           rwkv_ka, rwkv_rk, rwkv_ln_w, rwkv_ln_b, ret_gn, hgrn_lb, hgrn_gn, w_branch, w_out,
           norm_ffn, w_ffn_in, w_ffn_out, norm_final):
    depth = w_in.shape[0]
    bp, tp, _ = x_prompt.shape
    bs, ts, _ = x_sample.shape
    assert ts <= SAMPLE_CHUNK
    assert all(tp % c == 0 and bp % s == 0 for c, s in PROMPT_CFG.values())
    assert all(bs % s == 0 for _, s in SAMPLE_CFG.values())
    dt = x_prompt.dtype

    lower = _lower_bounds(hgrn_lb)
    w_in_r = w_in.astype(BF16)
    w_branch_b = w_branch.astype(BF16)
    w_out_b = w_out.astype(BF16)
    w_ffn_in_b = w_ffn_in.astype(BF16)
    w_ffn_out_b = w_ffn_out.astype(BF16)
    lanes = jnp.arange(BR_W)
    hsum = (lanes[:, None] // A_HD == lanes[None, :] // A_HD).astype(BF16)
    hsum128 = (lanes[:, None] // B_HD == lanes[None, :] // B_HD).astype(BF16)
    row = lambda a: a.reshape(1, -1).astype(F32)

    cos_p, sin_p = _rope_tables(jnp.arange(tp, dtype=jnp.int32))
    cos_s, sin_s = _rope_tables(PAST_LEN + jnp.arange(SAMPLE_CHUNK, dtype=jnp.int32))

    xp = x_prompt.reshape(bp * tp, D_MODEL)
    xs = jnp.pad(x_sample, ((0, 0), (0, SAMPLE_CHUNK - ts), (0, 0))).reshape(bs * SAMPLE_CHUNK, D_MODEL)
    zero_states = (jnp.zeros((1, bp, 1, A_COLS), F32), jnp.zeros((1, bp, A_HEADS, A_HD, A_HD), F32),
                   jnp.zeros((1, bp, B_HEADS, B_HD, B_HD), F32), jnp.zeros((1, bp, C_HEADS, C_HD, C_HD), F32))
    sample_states = (state_rwkv_shift.reshape(depth, bs, 1, A_COLS), state_rwkv_wkv, state_ret, state_hgrn)

    new_states = lambda b: (jnp.zeros((depth, b, A_HEADS, A_HD, A_HD), F32),
                            jnp.zeros((depth, b, B_HEADS, B_HD, B_HD), F32),
                            jnp.zeros((depth, b, C_HEADS, C_HD, C_HD), F32))
    stacks_p, stacks_s = new_states(bp), new_states(bs)
    shifts_p, shifts_s = [], []
    for l in range(depth):
        p = {
            "layer": l, "norm_mix": row(norm_mix[l]), "w_in": w_in_r, "mu": row(rwkv_mu[l]),
            "w0": row(rwkv_w0[l]), "w2": rwkv_w2[l].astype(BF16), "a0": row(rwkv_a0[l]),
            "a2": rwkv_a2[l].astype(BF16), "g2": rwkv_g2[l].astype(BF16), "kk": row(rwkv_kk[l]),
            "ka": row(rwkv_ka[l]), "rk": row(rwkv_rk[l]), "lnw": row(rwkv_ln_w[l]),
            "lnb": row(rwkv_ln_b[l]), "hsum": hsum, "hsum128": hsum128, "ret_gn": row(ret_gn[l]),
            "lower": lower[l:l + 1], "hgrn_gn": row(hgrn_gn[l]), "w_branch": w_branch_b,
            "w_out": w_out_b, "norm_ffn": row(norm_ffn[l]), "w_ffn_in": w_ffn_in_b,
            "w_ffn_out": w_ffn_out_b, "norm_final": row(norm_final), "last": l == depth - 1,
        }
        xp, sh_p, stacks_p = _layer(xp, zero_states, 0, stacks_p, cos_p, sin_p, p, bp, tp, tp, PROMPT_CFG)
        xs, sh_s, stacks_s = _layer(xs, sample_states, l, stacks_s, cos_s, sin_s, p, bs, SAMPLE_CHUNK, ts,
                                    SAMPLE_CFG)
        shifts_p.append(sh_p.reshape(bp, A_COLS))
        shifts_s.append(sh_s.reshape(bs, A_COLS))

    y_prompt = xp.reshape(bp, tp, D_MODEL)
    y_sample = xs.reshape(bs, SAMPLE_CHUNK, D_MODEL)[:, :ts]
    cast = lambda a: a.astype(dt)
    return (y_prompt, y_sample, cast(jnp.stack(shifts_p)), *map(cast, stacks_p),
            cast(jnp.stack(shifts_s)), *map(cast, stacks_s))
```
